```python
import jax, jax.numpy as jnp
from jax import lax
import numpy as np

D_MODEL = 1024
BATCH = 8
SEQ = 2048
DEPTH = 1
DEC_BATCH = 128
DEC_SEQ = 4
PAST_LEN = 16384
PAGE_SIZE = 128

D_MIX = D_MODEL
D_RWKV = D_MIX // 2
D_CONV = D_MIX - D_RWKV
HEAD_SIZE = 64
N_HEADS_RWKV = D_RWKV // HEAD_SIZE
LORA_W = 64
LORA_A = 64
LORA_G = 128
CONV_W = 3
D_FF = 2816
ALPHA = (2.0 * DEPTH) ** 0.25
BETA = (8.0 * DEPTH) ** -0.25
LN_EPS = 1e-5
GN_EPS = 1e-5 * HEAD_SIZE

OFF_R = 0
OFF_K = OFF_R + D_RWKV
OFF_V = OFF_K + D_RWKV
OFF_WD = OFF_V + D_RWKV
OFF_AD = OFF_WD + LORA_W
OFF_GD = OFF_AD + LORA_A
P_SHIFT = OFF_GD + LORA_G
OFF_CB = P_SHIFT
OFF_CC = OFF_CB + D_CONV
OFF_CH = OFF_CC + D_CONV
P_TOTAL = OFF_CH + D_CONV

kernel_name = "hymba_rwkv7_shortconv_macaron_deepnorm_step"


def layer_norm(x, g, b):
    xf = x.astype(jnp.float32)
    mu = jnp.mean(xf, axis=-1, keepdims=True)
    var = jnp.mean(jnp.square(xf - mu), axis=-1, keepdims=True)
    return ((xf - mu) * lax.rsqrt(var + LN_EPS) * g.astype(jnp.float32) + b.astype(jnp.float32)).astype(x.dtype)


def swiglu(x, wg, wu, wd):
    return (jax.nn.silu(x @ wg) * (x @ wu)) @ wd


def wkv7_recurrence(r, w, k, v, kk, a, s0):
    def step(s, inp):
        r_t, w_t, k_t, v_t, kk_t, a_t = inp
        sa = jnp.einsum("bhij,bhj->bhi", s, -kk_t)
        s = (s * w_t[:, :, None, :]
             + sa[..., None] * (kk_t * a_t)[:, :, None, :]
             + v_t[..., None] * k_t[:, :, None, :])
        o_t = jnp.einsum("bhij,bhj->bhi", s, r_t)
        return s, o_t
    xs = (jnp.moveaxis(r, 1, 0), jnp.moveaxis(w, 1, 0), jnp.moveaxis(k, 1, 0),
          jnp.moveaxis(v, 1, 0), jnp.moveaxis(kk, 1, 0), jnp.moveaxis(a, 1, 0))
    s_final, o = lax.scan(step, s0, xs)
    return jnp.moveaxis(o, 0, 1), s_final


def trunk_layer(x, wkv0, shift0, conv0, p):
    f32 = jnp.float32
    bsz, T, _ = x.shape
    x = layer_norm(ALPHA * x + 0.5 * swiglu(x, p["ffn1_wg"], p["ffn1_wu"], p["ffn1_wd"]), p["ln1_g"], p["ln1_b"])

    proj = x @ p["w_in"]

    ps = proj[..., :P_SHIFT]
    prev = jnp.concatenate([shift0[:, None, :].astype(ps.dtype), ps[:, :-1]], axis=1)
    ps_mix = ps + p["mu_shift"] * (prev - ps)
    new_shift = ps[:, -1]
    r = ps_mix[..., OFF_R:OFF_K]
    k = ps_mix[..., OFF_K:OFF_V]
    v = ps_mix[..., OFF_V:OFF_WD]
    wd = ps_mix[..., OFF_WD:OFF_AD]
    ad = ps_mix[..., OFF_AD:OFF_GD]
    gd = ps_mix[..., OFF_GD:P_SHIFT]

    w_log = -jax.nn.softplus(-(p["w0"] + jnp.tanh(wd) @ p["w_lora_up"]).astype(f32)) - 0.5
    decay = jnp.exp(-jnp.exp(w_log))
    a = jax.nn.sigmoid((p["a0"] + ad @ p["a_lora_up"]).astype(f32))
    g = jax.nn.sigmoid(gd) @ p["g_lora_up"]

    def heads(t):
        return t.astype(f32).reshape(bsz, T, N_HEADS_RWKV, HEAD_SIZE)

    def headvec(t):
        return t.astype(f32).reshape(N_HEADS_RWKV, HEAD_SIZE)

    r_h, k_h, v_h, a_h, w_h = heads(r), heads(k), heads(v), heads(a), heads(decay)
    kk = k_h * headvec(p["k_k"])
    kk = kk / jnp.maximum(jnp.sqrt(jnp.sum(kk * kk, axis=-1, keepdims=True)), 1e-12)
    k_h = k_h * (1.0 + (a_h - 1.0) * headvec(p["k_a"]))

    o, wkv_new = wkv7_recurrence(r_h, w_h, k_h, v_h, kk, a_h, wkv0.astype(f32))
    o_mu = jnp.mean(o, axis=-1, keepdims=True)
    o_var = jnp.mean(jnp.square(o - o_mu), axis=-1, keepdims=True)
    o = (o - o_mu) * lax.rsqrt(o_var + GN_EPS) * headvec(p["lnx_g"]) + headvec(p["lnx_b"])
    o = o + jnp.sum(r_h * k_h * p["r_k"].astype(f32), axis=-1, keepdims=True) * v_h
    y_rwkv = o.reshape(bsz, T, D_RWKV).astype(x.dtype) * g

    c_b = proj[..., OFF_CB:OFF_CC]
    c_c = proj[..., OFF_CC:OFF_CH]
    c_h = proj[..., OFF_CH:P_TOTAL]
    u = c_c * c_h
    u_full = jnp.concatenate([conv0.astype(u.dtype), u], axis=1)
    cw = p["conv_w"]
    z = cw[0] * u_full[:, 0:T]
    for i in range(1, CONV_W):
        z = z + cw[i] * u_full[:, i:i + T]
    new_conv = u_full[:, T:]
    y_conv = c_b * z

    mix = jnp.concatenate([y_rwkv, y_conv], axis=-1) @ p["w_out"]
    x = layer_norm(ALPHA * x + mix, p["ln2_g"], p["ln2_b"])
    x = layer_norm(ALPHA * x + 0.5 * swiglu(x, p["ffn2_wg"], p["ffn2_wu"], p["ffn2_wd"]), p["ln3_g"], p["ln3_b"])
    return x, wkv_new, new_shift, new_conv


def setup_inputs(seed: int = 0) -> dict:
    key = jax.random.key(seed)
    ks = iter(jax.random.split(key, 40))

    def nrm(shape, scale):
        return jax.random.normal(next(ks), shape, jnp.float32) * scale

    d = D_MODEL
    col_scale = jnp.ones((P_TOTAL,), jnp.float32).at[OFF_V:OFF_WD].set(BETA)
    inp = {}
    inp["x_prompt"] = nrm((BATCH, SEQ, d), 1.0)
    inp["x_sample"] = nrm((DEC_BATCH, DEC_SEQ, d), 1.0)
    inp["state_wkv"] = nrm((DEPTH, DEC_BATCH, N_HEADS_RWKV, HEAD_SIZE, HEAD_SIZE), 0.5)
    inp["state_shift"] = nrm((DEPTH, DEC_BATCH, P_SHIFT), 1.0)
    inp["state_conv"] = nrm((DEPTH, DEC_BATCH, CONV_W - 1, D_CONV), 1.0)
    inp["ln1_g"] = 1.0 + nrm((DEPTH, d), 0.05)
    inp["ln1_b"] = nrm((DEPTH, d), 0.02)
    inp["ffn1_wg"] = nrm((DEPTH, d, D_FF), d ** -0.5)
    inp["ffn1_wu"] = nrm((DEPTH, d, D_FF), d ** -0.5)
    inp["ffn1_wd"] = nrm((DEPTH, D_FF, d), BETA * D_FF ** -0.5)
    inp["w_in"] = nrm((DEPTH, d, P_TOTAL), d ** -0.5) * col_scale
    inp["mu_shift"] = jax.random.uniform(next(ks), (DEPTH, P_SHIFT), jnp.float32)
    inp["w0"] = -3.0 + nrm((DEPTH, D_RWKV), 1.0)
    inp["w_lora_up"] = nrm((DEPTH, LORA_W, D_RWKV), 0.5 * LORA_W ** -0.5)
    inp["a0"] = nrm((DEPTH, D_RWKV), 0.3)
    inp["a_lora_up"] = nrm((DEPTH, LORA_A, D_RWKV), 0.5 * LORA_A ** -0.5)
    inp["g_lora_up"] = nrm((DEPTH, LORA_G, D_RWKV), LORA_G ** -0.5)
    inp["k_k"] = 0.85 + nrm((DEPTH, D_RWKV), 0.05)
    inp["k_a"] = 1.0 + nrm((DEPTH, D_RWKV), 0.05)
    inp["r_k"] = nrm((DEPTH, N_HEADS_RWKV, HEAD_SIZE), 0.1)
    inp["lnx_g"] = 1.0 + nrm((DEPTH, D_RWKV), 0.05)
    inp["lnx_b"] = nrm((DEPTH, D_RWKV), 0.02)
    inp["conv_w"] = nrm((DEPTH, CONV_W, D_CONV), CONV_W ** -0.5)
    inp["w_out"] = nrm((DEPTH, D_MIX, d), BETA * D_MIX ** -0.5)
    inp["ln2_g"] = 1.0 + nrm((DEPTH, d), 0.05)
    inp["ln2_b"] = nrm((DEPTH, d), 0.02)
    inp["ffn2_wg"] = nrm((DEPTH, d, D_FF), d ** -0.5)
    inp["ffn2_wu"] = nrm((DEPTH, d, D_FF), d ** -0.5)
    inp["ffn2_wd"] = nrm((DEPTH, D_FF, d), BETA * D_FF ** -0.5)
    inp["ln3_g"] = 1.0 + nrm((DEPTH, d), 0.05)
    inp["ln3_b"] = nrm((DEPTH, d), 0.02)
    return inp


def reference(x_prompt, x_sample, state_wkv, state_shift, state_conv,
              ln1_g, ln1_b, ffn1_wg, ffn1_wu, ffn1_wd, w_in, mu_shift, w0, w_lora_up,
              a0, a_lora_up, g_lora_up, k_k, k_a, r_k, lnx_g, lnx_b, conv_w, w_out,
              ln2_g, ln2_b, ffn2_wg, ffn2_wu, ffn2_wd, ln3_g, ln3_b):
    xp = x_prompt
    xs = x_sample
    bp = x_prompt.shape[0]
    wkv_p, shift_p, conv_p = [], [], []
    wkv_s, shift_s, conv_s = [], [], []
    for l in range(DEPTH):
        p = {
            "ln1_g": ln1_g[l], "ln1_b": ln1_b[l],
            "ffn1_wg": ffn1_wg[l], "ffn1_wu": ffn1_wu[l], "ffn1_wd": ffn1_wd[l],
            "w_in": w_in[l], "mu_shift": mu_shift[l], "w0": w0[l], "w_lora_up": w_lora_up[l],
            "a0": a0[l], "a_lora_up": a_lora_up[l], "g_lora_up": g_lora_up[l],
            "k_k": k_k[l], "k_a": k_a[l], "r_k": r_k[l], "lnx_g": lnx_g[l], "lnx_b": lnx_b[l],
            "conv_w": conv_w[l], "w_out": w_out[l],
            "ln2_g": ln2_g[l], "ln2_b": ln2_b[l],
            "ffn2_wg": ffn2_wg[l], "ffn2_wu": ffn2_wu[l], "ffn2_wd": ffn2_wd[l],
            "ln3_g": ln3_g[l], "ln3_b": ln3_b[l],
        }
        wkv0_p = jnp.zeros((bp, N_HEADS_RWKV, HEAD_SIZE, HEAD_SIZE), jnp.float32)
        shift0_p = jnp.zeros((bp, P_SHIFT), xp.dtype)
        conv0_p = jnp.zeros((bp, CONV_W - 1, D_CONV), xp.dtype)
        xp, wp, sp, cp = trunk_layer(xp, wkv0_p, shift0_p, conv0_p, p)
        xs, wsm, ssm, csm = trunk_layer(xs, state_wkv[l], state_shift[l], state_conv[l], p)
        wkv_p.append(wp); shift_p.append(sp); conv_p.append(cp)
        wkv_s.append(wsm); shift_s.append(ssm); conv_s.append(csm)
    return (xp, xs,
            jnp.stack(wkv_p), jnp.stack(shift_p), jnp.stack(conv_p),
            jnp.stack(wkv_s), jnp.stack(shift_s), jnp.stack(conv_s))
```

```python
import functools

import jax
import jax.numpy as jnp
from jax import lax
from jax.experimental import pallas as pl
from jax.experimental.pallas import tpu as pltpu

F32 = jnp.float32
BF16 = jnp.bfloat16
HI = lax.Precision.HIGHEST

HEAD = 64
LORA_W = 64
LORA_A = 64
LORA_G = 128
CONV_W = 3
LN_EPS = 1e-5
GN_EPS = 1e-5 * HEAD

VMEM_LIMIT_BYTES = 56 * 1024 * 1024
TOKEN_TILE = 512
CHUNK = 64
SAMPLE_PAD_T = 8


def _resident(shape):
    nd = len(shape)
    return pl.BlockSpec(shape, lambda *_: (0,) * nd, pipeline_mode=pl.Buffered(1))


def _layer_norm(x, g, b):
    mu = jnp.mean(x, axis=-1, keepdims=True)
    d = x - mu
    var = jnp.mean(d * d, axis=-1, keepdims=True)
    return d * lax.rsqrt(var + LN_EPS) * g + b


def _swiglu(xb, wg_ref, wu_ref, wd_ref):
    g = jnp.dot(xb, wg_ref[...], preferred_element_type=F32)
    u = jnp.dot(xb, wu_ref[...], preferred_element_type=F32)
    h = (jax.nn.silu(g) * u).astype(BF16)
    return jnp.dot(h, wd_ref[...], preferred_element_type=F32)


def _ffn_ln_kernel(x_ref, wg_ref, wu_ref, wd_ref, g_ref, b_ref, o_ref, *, alpha):
    x = x_ref[...]
    y = alpha * x + 0.5 * _swiglu(x.astype(BF16), wg_ref, wu_ref, wd_ref)
    o_ref[...] = _layer_norm(y, g_ref[...], b_ref[...])


def _ffn_ln(x, wg, wu, wd, g, b, alpha):
    n, d = x.shape
    dff = wg.shape[1]
    tm = TOKEN_TILE
    assert n % tm == 0
    row = pl.BlockSpec((tm, d), lambda i: (i, 0))
    return pl.pallas_call(
        functools.partial(_ffn_ln_kernel, alpha=alpha),
        grid=(n // tm,),
        in_specs=[row, _resident((d, dff)), _resident((d, dff)), _resident((dff, d)),
                  _resident((1, d)), _resident((1, d))],
        out_specs=row,
        out_shape=jax.ShapeDtypeStruct((n, d), F32),
        compiler_params=pltpu.CompilerParams(
            dimension_semantics=("arbitrary",), vmem_limit_bytes=VMEM_LIMIT_BYTES),
        name="ffn_ln",
    )(x, wg, wu, wd, g, b)


def _shift_rows(x, k):
    return pltpu.roll(x, k, axis=0)


def _premix_kernel(*refs, seq_len, n_real, tiles_per_seq, d_rwkv, d_conv, full_out):
    (x_ref, win_ref, mu_ref, w0_ref, wup_ref, a0_ref, aup_ref, gup_ref, kk_ref, ka_ref,
     cw_ref, seg_ref) = refs[:12]
    refs = refs[12:]
    if tiles_per_seq == 0:
        sh0_ref, cv1_ref, cv2_ref = refs[:3]
        refs = refs[3:]
    (r_ref, nkk_ref, b_ref, k_ref, v_ref, lw_ref, g_ref, yc_ref) = refs[:8]
    refs = refs[8:]
    if full_out:
        ps_out_ref, u_out_ref = refs[:2]
        refs = refs[2:]
    else:
        shift_ref, tail_ref = refs[:2]
        refs = refs[2:]
    if tiles_per_seq:
        ps_carry, u_carry = refs

    p_shift = 3 * d_rwkv + LORA_W + LORA_A + LORA_G
    tm = x_ref.shape[0]
    proj = jnp.dot(x_ref[...].astype(BF16), win_ref[...], preferred_element_type=F32)
    ps = proj[:, :p_shift]
    c_b = proj[:, p_shift:p_shift + d_conv]
    u = proj[:, p_shift + d_conv:p_shift + 2 * d_conv] * proj[:, p_shift + 2 * d_conv:]

    row = lax.broadcasted_iota(jnp.int32, (tm, 1), 0)
    if tiles_per_seq:
        @pl.when(pl.program_id(0) % tiles_per_seq == 0)
        def _():
            ps_carry[...] = jnp.zeros_like(ps_carry)
            u_carry[...] = jnp.zeros_like(u_carry)

        prev = jnp.where(row == 0, ps_carry[...], _shift_rows(ps, 1))
        um1 = jnp.where(row == 0, u_carry[1:2, :], _shift_rows(u, 1))
        um2 = jnp.where(row == 0, u_carry[0:1, :],
                        jnp.where(row == 1, u_carry[1:2, :], _shift_rows(u, 2)))
        ps_carry[...] = ps[tm - 1:tm, :]
        u_carry[...] = u[tm - 2:tm, :]
        t_in = None
    else:
        t_in = row % seq_len
        prev = jnp.where(t_in == 0, sh0_ref[...], _shift_rows(ps, 1))
        um1 = jnp.where(t_in == 0, cv1_ref[...], _shift_rows(u, 1))
        um2 = jnp.where(t_in <= 1, cv2_ref[...], _shift_rows(u, 2))

    mix = ps + mu_ref[...] * (prev - ps)
    r = mix[:, 0:d_rwkv]
    k = mix[:, d_rwkv:2 * d_rwkv]
    v = mix[:, 2 * d_rwkv:3 * d_rwkv]
    o = 3 * d_rwkv
    wd = mix[:, o:o + LORA_W]
    ad = mix[:, o + LORA_W:o + LORA_W + LORA_A]
    gd = mix[:, o + LORA_W + LORA_A:p_shift]

    w_log = -jax.nn.softplus(-(w0_ref[...] + jnp.dot(
        jnp.tanh(wd).astype(BF16), wup_ref[...], preferred_element_type=F32))) - 0.5
    log_decay = -jnp.exp(w_log)
    a = jax.nn.sigmoid(a0_ref[...] + jnp.dot(
        ad.astype(BF16), aup_ref[...], preferred_element_type=F32))
    g = jnp.dot(jax.nn.sigmoid(gd).astype(BF16), gup_ref[...], preferred_element_type=F32)

    kk = k * kk_ref[...]
    ss = jnp.dot(kk * kk, seg_ref[...], preferred_element_type=F32, precision=HI)
    kk = kk / jnp.maximum(jnp.sqrt(ss), 1e-12)
    k = k * (1.0 + (a - 1.0) * ka_ref[...])
    nkk = -kk
    b = kk * a

    if n_real != seq_len:
        live = t_in < n_real
        zero = jnp.zeros_like(r)
        nkk = jnp.where(live, nkk, zero)
        b = jnp.where(live, b, zero)
        k = jnp.where(live, k, zero)
        v = jnp.where(live, v, zero)
        log_decay = jnp.where(live, log_decay, zero)

    r_ref[...] = r
    nkk_ref[...] = nkk
    b_ref[...] = b
    k_ref[...] = k
    v_ref[...] = v
    lw_ref[...] = log_decay
    g_ref[...] = g

    cw = cw_ref[...]
    z = cw[0:1, :] * um2
    z = z + cw[1:2, :] * um1
    z = z + cw[2:3, :] * u
    yc_ref[...] = (c_b * z).astype(BF16)

    if full_out:
        ps_out_ref[...] = ps
        u_out_ref[...] = u
    else:
        shift_ref[0] = ps[tm - 1:tm, :]
        tail_ref[0] = u[tm - (CONV_W - 1):tm, :]


def _premix(x1, p, seg, *, seq_len, n_real, init=None):
    n, d = x1.shape
    tm = TOKEN_TILE
    d_rwkv = p["w0"].shape[1]
    d_conv = p["conv_w"].shape[1]
    p_shift = 3 * d_rwkv + LORA_W + LORA_A + LORA_G
    p_total = p_shift + 3 * d_conv
    n_seq = n // seq_len
    if init is None:
        assert seq_len % tm == 0
        tiles_per_seq = seq_len // tm
    else:
        assert tm % seq_len == 0
        tiles_per_seq = 0
    full_out = init is not None

    def row(c):
        return pl.BlockSpec((tm, c), lambda i: (i, 0))

    in_specs = [row(d), _resident((d, p_total)), _resident((1, p_shift)),
                _resident((1, d_rwkv)), _resident((LORA_W, d_rwkv)),
                _resident((1, d_rwkv)), _resident((LORA_A, d_rwkv)),
                _resident((LORA_G, d_rwkv)), _resident((1, d_rwkv)), _resident((1, d_rwkv)),
                _resident((CONV_W, d_conv)), _resident((d_rwkv, d_rwkv))]
    args = [x1, p["w_in"], p["mu_shift"], p["w0"], p["w_lora_up"], p["a0"], p["a_lora_up"],
            p["g_lora_up"], p["k_k"], p["k_a"], p["conv_w"], seg]
    scratch = []
    if init is not None:
        in_specs += [row(p_shift), row(d_conv), row(d_conv)]
        args += list(init)
    else:
        scratch = [pltpu.VMEM((1, p_shift), F32), pltpu.VMEM((CONV_W - 1, d_conv), F32)]

    out_specs = [row(d_rwkv)] * 7 + [row(d_conv)]
    out_shape = [jax.ShapeDtypeStruct((n, d_rwkv), F32)] * 7 + [
        jax.ShapeDtypeStruct((n, d_conv), BF16)]
    if full_out:
        out_specs += [row(p_shift), row(d_conv)]
        out_shape += [jax.ShapeDtypeStruct((n, p_shift), F32),
                      jax.ShapeDtypeStruct((n, d_conv), F32)]
    else:
        out_specs += [
            pl.BlockSpec((1, 1, p_shift), lambda i: (i // tiles_per_seq, 0, 0)),
            pl.BlockSpec((1, CONV_W - 1, d_conv), lambda i: (i // tiles_per_seq, 0, 0))]
        out_shape += [jax.ShapeDtypeStruct((n_seq, 1, p_shift), F32),
                      jax.ShapeDtypeStruct((n_seq, CONV_W - 1, d_conv), F32)]

    return pl.pallas_call(
        functools.partial(_premix_kernel, seq_len=seq_len, n_real=n_real,
                          tiles_per_seq=tiles_per_seq, d_rwkv=d_rwkv, d_conv=d_conv,
                          full_out=full_out),
        grid=(n // tm,),
        in_specs=in_specs,
        out_specs=out_specs,
        out_shape=out_shape,
        scratch_shapes=scratch,
        compiler_params=pltpu.CompilerParams(
            dimension_semantics=("arbitrary",), vmem_limit_bytes=VMEM_LIMIT_BYTES),
        name="premix",
    )(*args)


def _dot(a, b):
    return jnp.dot(a, b, preferred_element_type=F32, precision=HI)


def _dot_nt(a, b):
    return lax.dot_general(a, b, (((1,), (1,)), ((), ())),
                           preferred_element_type=F32, precision=HI)


def _dot_tn(a, b):
    return lax.dot_general(a, b, (((0,), (0,)), ((), ())),
                           preferred_element_type=F32, precision=HI)


def _wkv_kernel(r_ref, nkk_ref, b_ref, k_ref, v_ref, lw_ref, g_ref, s0_ref,
                rk_ref, lng_ref, lnb_ref, y_ref, sout_ref, s_scr, *, seqs, chunk, n_heads):
    c = pl.program_id(1)

    @pl.when(c == 0)
    def _():
        s_scr[...] = s0_ref[...]

    ii = lax.broadcasted_iota(jnp.int32, (chunk, chunk), 0)
    jj = lax.broadcasted_iota(jnp.int32, (chunk, chunk), 1)
    incl = ii >= jj
    strict = ii > jj
    tri = incl.astype(F32)
    n_doublings = chunk.bit_length() - 1

    y_rows = []
    for q in range(seqs):
        rows = slice(q * chunk, (q + 1) * chunk)
        y_heads = []
        lw = lw_ref[rows, :]
        cum = _dot(tri, lw)
        last = cum[chunk - 1:chunk, :]
        e_in = jnp.exp(cum)
        e_out = jnp.exp(-cum)
        e_tail = jnp.exp(last - cum)
        a_t = nkk_ref[rows, :] * jnp.exp(cum - lw)
        r_t = r_ref[rows, :] * e_in
        b_t = b_ref[rows, :] * e_out
        k_t = k_ref[rows, :] * e_out
        b_h = b_ref[rows, :] * e_tail
        k_h = k_ref[rows, :] * e_tail
        gamma = jnp.exp(last)
        r_all = r_ref[rows, :]
        k_all = k_ref[rows, :]
        v_all = v_ref[rows, :]
        g_all = g_ref[rows, :]
        for h in range(n_heads):
            hs = slice(h * HEAD, (h + 1) * HEAD)
            s = s_scr[q, h]
            ar = jnp.concatenate([a_t[:, hs], r_t[:, hs]], axis=0)
            bk = jnp.concatenate([b_t[:, hs], k_t[:, hs]], axis=0)
            v = v_all[:, hs]
            m = _dot_nt(ar, bk)
            zs = _dot_nt(ar, s)
            l_ab = jnp.where(strict, m[:chunk, :chunk], 0.0)
            l_ak = jnp.where(strict, m[:chunk, chunk:], 0.0)
            l_rb = jnp.where(incl, m[chunk:, :chunk], 0.0)
            l_rk = jnp.where(incl, m[chunk:, chunk:], 0.0)
            u = zs[:chunk] + _dot(l_ak, v)
            pw = l_ab
            u = u + _dot(pw, u)
            for _ in range(n_doublings - 1):
                pw = _dot(pw, pw)
                u = u + _dot(pw, u)
            uv = jnp.concatenate([u, v], axis=0)
            o = zs[chunk:] + _dot(jnp.concatenate([l_rb, l_rk], axis=1), uv)
            bkh = jnp.concatenate([b_h[:, hs], k_h[:, hs]], axis=0)
            s_scr[q, h] = s * gamma[:, hs] + _dot_tn(uv, bkh)

            mu = jnp.mean(o, axis=-1, keepdims=True)
            d = o - mu
            var = jnp.mean(d * d, axis=-1, keepdims=True)
            o = d * lax.rsqrt(var + GN_EPS) * lng_ref[:, hs] + lnb_ref[:, hs]
            bonus = jnp.sum(r_all[:, hs] * k_all[:, hs] * rk_ref[:, hs], axis=-1, keepdims=True)
            y_heads.append((o + bonus * v) * g_all[:, hs])
        y_rows.append(jnp.concatenate(y_heads, axis=1))
    y_ref[...] = jnp.concatenate(y_rows, axis=0).astype(BF16)

    @pl.when(c == pl.num_programs(1) - 1)
    def _():
        sout_ref[...] = s_scr[...]


def _wkv(ops, s0, rk, lng, lnb, *, seq_len, chunk, seqs):
    n, d_rwkv = ops[0].shape
    n_seq, n_heads = s0.shape[0], s0.shape[1]
    assert seq_len % chunk == 0 and n_seq % seqs == 0
    n_chunks = seq_len // chunk
    assert seqs == 1 or n_chunks == 1
    rows = seqs * chunk
    tok = pl.BlockSpec((rows, d_rwkv), lambda q, c: (q * n_chunks + c, 0))
    state = pl.BlockSpec((seqs, n_heads, HEAD, HEAD), lambda q, c: (q, 0, 0, 0))
    return pl.pallas_call(
        functools.partial(_wkv_kernel, seqs=seqs, chunk=chunk, n_heads=n_heads),
        grid=(n_seq // seqs, n_chunks),
        in_specs=[tok] * 7 + [state] + [pl.BlockSpec((1, d_rwkv), lambda q, c: (0, 0))] * 3,
        out_specs=[tok, state],
        out_shape=[jax.ShapeDtypeStruct((n, d_rwkv), BF16),
                   jax.ShapeDtypeStruct(s0.shape, F32)],
        scratch_shapes=[pltpu.VMEM((seqs, n_heads, HEAD, HEAD), F32)],
        compiler_params=pltpu.CompilerParams(
            dimension_semantics=("arbitrary", "arbitrary"),
            vmem_limit_bytes=VMEM_LIMIT_BYTES),
        name="wkv",
    )(*ops, s0, rk, lng, lnb)


def _out_ffn_kernel(x_ref, yr_ref, yc_ref, wo_r_ref, wo_c_ref, g2_ref, b2_ref,
                    wg_ref, wu_ref, wd_ref, g3_ref, b3_ref, o_ref, *, alpha):
    mix = jnp.dot(yr_ref[...], wo_r_ref[...], preferred_element_type=F32)
    mix = mix + jnp.dot(yc_ref[...], wo_c_ref[...], preferred_element_type=F32)
    x = _layer_norm(alpha * x_ref[...] + mix, g2_ref[...], b2_ref[...])
    y = alpha * x + 0.5 * _swiglu(x.astype(BF16), wg_ref, wu_ref, wd_ref)
    o_ref[...] = _layer_norm(y, g3_ref[...], b3_ref[...])


def _out_ffn(x1, y_rwkv, y_conv, wo_r, wo_c, g2, b2, wg, wu, wd, g3, b3, alpha):
    n, d = x1.shape
    dff = wg.shape[1]
    d_rwkv, d_conv = y_rwkv.shape[1], y_conv.shape[1]
    tm = TOKEN_TILE
    assert n % tm == 0

    def row(c):
        return pl.BlockSpec((tm, c), lambda i: (i, 0))

    return pl.pallas_call(
        functools.partial(_out_ffn_kernel, alpha=alpha),
        grid=(n // tm,),
        in_specs=[row(d), row(d_rwkv), row(d_conv), _resident((d_rwkv, d)), _resident((d_conv, d)),
                  _resident((1, d)), _resident((1, d)),
                  _resident((d, dff)), _resident((d, dff)), _resident((dff, d)),
                  _resident((1, d)), _resident((1, d))],
        out_specs=row(d),
        out_shape=jax.ShapeDtypeStruct((n, d), F32),
        compiler_params=pltpu.CompilerParams(
            dimension_semantics=("arbitrary",), vmem_limit_bytes=VMEM_LIMIT_BYTES),
        name="out_ffn",
    )(x1, y_rwkv, y_conv, wo_r, wo_c, g2, b2, wg, wu, wd, g3, b3)


def _trunk_layer(x, p, seg, alpha, *, seq_len, n_real, chunk, seqs, wkv0, init):
    x1 = _ffn_ln(x, p["ffn1_wg"], p["ffn1_wu"], p["ffn1_wd"], p["ln1_g"], p["ln1_b"], alpha)
    pm = _premix(x1, p, seg, seq_len=seq_len, n_real=n_real, init=init)
    ops, y_conv, st_a, st_b = pm[:7], pm[7], pm[8], pm[9]
    y_rwkv, wkv_new = _wkv(ops, wkv0, p["r_k"], p["lnx_g"], p["lnx_b"],
                           seq_len=seq_len, chunk=chunk, seqs=seqs)
    d_rwkv = y_rwkv.shape[1]
    y = _out_ffn(x1, y_rwkv, y_conv, p["w_out"][:d_rwkv], p["w_out"][d_rwkv:],
                 p["ln2_g"], p["ln2_b"], p["ffn2_wg"], p["ffn2_wu"], p["ffn2_wd"],
                 p["ln3_g"], p["ln3_b"], alpha)
    return y, wkv_new, st_a, st_b


def _expand_rows(x, seq_len):
    n_seq, k, c = x.shape
    return jnp.pad(x, ((0, 0), (0, seq_len - k), (0, 0))).reshape(n_seq * seq_len, c)


def kernel(x_prompt, x_sample, state_wkv, state_shift, state_conv, ln1_g, ln1_b, ffn1_wg, ffn1_wu, ffn1_wd, w_in, mu_shift, w0, w_lora_up, a0, a_lora_up, g_lora_up, k_k, k_a, r_k, lnx_g, lnx_b, conv_w, w_out, ln2_g, ln2_b, ffn2_wg, ffn2_wu, ffn2_wd, ln3_g, ln3_b):
    depth = ln1_g.shape[0]
    bp, tp, d = x_prompt.shape
    bs, ts, _ = x_sample.shape
    n_heads = state_wkv.shape[2]
    d_rwkv = n_heads * HEAD
    alpha = (2.0 * depth) ** 0.25
    assert ts <= SAMPLE_PAD_T

    head_id = jnp.arange(d_rwkv) // HEAD
    seg = (head_id[:, None] == head_id[None, :]).astype(F32)

    xp = x_prompt.reshape(bp * tp, d)
    xs = jnp.pad(x_sample, ((0, 0), (0, SAMPLE_PAD_T - ts), (0, 0))).reshape(bs * SAMPLE_PAD_T, d)

    outs = {k: [] for k in ("wkv_p", "shift_p", "conv_p", "wkv_s", "shift_s", "conv_s")}
    for l in range(depth):
        vec = lambda a: a[l].reshape(1, -1).astype(F32)
        p = {
            "ln1_g": vec(ln1_g), "ln1_b": vec(ln1_b),
            "ffn1_wg": ffn1_wg[l].astype(BF16), "ffn1_wu": ffn1_wu[l].astype(BF16),
            "ffn1_wd": ffn1_wd[l].astype(BF16),
            "w_in": w_in[l].astype(BF16), "mu_shift": vec(mu_shift), "w0": vec(w0),
            "w_lora_up": w_lora_up[l].astype(BF16), "a0": vec(a0),
            "a_lora_up": a_lora_up[l].astype(BF16), "g_lora_up": g_lora_up[l].astype(BF16),
            "k_k": vec(k_k), "k_a": vec(k_a), "r_k": vec(r_k),
            "lnx_g": vec(lnx_g), "lnx_b": vec(lnx_b),
            "conv_w": conv_w[l].astype(F32), "w_out": w_out[l].astype(BF16),
            "ln2_g": vec(ln2_g), "ln2_b": vec(ln2_b),
            "ffn2_wg": ffn2_wg[l].astype(BF16), "ffn2_wu": ffn2_wu[l].astype(BF16),
            "ffn2_wd": ffn2_wd[l].astype(BF16),
            "ln3_g": vec(ln3_g), "ln3_b": vec(ln3_b),
        }
        wkv0_p = jnp.zeros((bp, n_heads, HEAD, HEAD), F32)
        xp, wp, sp, cp = _trunk_layer(xp, p, seg, alpha, seq_len=tp, n_real=tp, chunk=CHUNK,
                                      seqs=1, wkv0=wkv0_p, init=None)
        conv0 = state_conv[l].astype(F32)
        init = (_expand_rows(state_shift[l].astype(F32)[:, None, :], SAMPLE_PAD_T),
                _expand_rows(conv0[:, 1:2, :], SAMPLE_PAD_T),
                _expand_rows(conv0, SAMPLE_PAD_T))
        xs, wsm, ps_s, u_s = _trunk_layer(xs, p, seg, alpha, seq_len=SAMPLE_PAD_T, n_real=ts,
                                          chunk=SAMPLE_PAD_T, seqs=8,
                                          wkv0=state_wkv[l].astype(F32), init=init)
        outs["wkv_p"].append(wp)
        outs["shift_p"].append(sp[:, 0, :])
        outs["conv_p"].append(cp)
        outs["wkv_s"].append(wsm)
        outs["shift_s"].append(ps_s.reshape(bs, SAMPLE_PAD_T, -1)[:, ts - 1, :])
        outs["conv_s"].append(u_s.reshape(bs, SAMPLE_PAD_T, -1)[:, ts - (CONV_W - 1):ts, :])

    y_prompt = xp.reshape(bp, tp, d)
    y_sample = xs.reshape(bs, SAMPLE_PAD_T, d)[:, :ts, :]
    return (y_prompt, y_sample,
            jnp.stack(outs["wkv_p"]), jnp.stack(outs["shift_p"]), jnp.stack(outs["conv_p"]),
            jnp.stack(outs["wkv_s"]), jnp.stack(outs["shift_s"]), jnp.stack(outs["conv_s"]))
```

```python
import functools

import jax
import jax.numpy as jnp
from jax import lax
from jax.experimental import pallas as pl
from jax.experimental.pallas import tpu as pltpu

F32 = jnp.float32
BF16 = jnp.bfloat16
HI = lax.Precision.HIGHEST

HEAD = 64
LORA_W = 64
LORA_A = 64
LORA_G = 128
CONV_W = 3
LN_EPS = 1e-5
GN_EPS = 1e-5 * HEAD

VMEM_LIMIT_BYTES = 56 * 1024 * 1024
TOKEN_TILE = 512
CHUNK = 64
SAMPLE_PAD_T = 8


def _resident(shape):
    nd = len(shape)
    return pl.BlockSpec(shape, lambda *_: (0,) * nd, pipeline_mode=pl.Buffered(1))


def _layer_norm(x, g, b):
    mu = jnp.mean(x, axis=-1, keepdims=True)
    d = x - mu
    var = jnp.mean(d * d, axis=-1, keepdims=True)
    return d * lax.rsqrt(var + LN_EPS) * g + b


def _swiglu(xb, wg_ref, wu_ref, wd_ref):
    g = jnp.dot(xb, wg_ref[...], preferred_element_type=F32)
    u = jnp.dot(xb, wu_ref[...], preferred_element_type=F32)
    h = (jax.nn.silu(g) * u).astype(BF16)
    return jnp.dot(h, wd_ref[...], preferred_element_type=F32)


def _ffn_ln_kernel(x_ref, wg_ref, wu_ref, wd_ref, g_ref, b_ref, o_ref, *, alpha):
    x = x_ref[...]
    y = alpha * x + 0.5 * _swiglu(x.astype(BF16), wg_ref, wu_ref, wd_ref)
    o_ref[...] = _layer_norm(y, g_ref[...], b_ref[...])


def _ffn_ln(x, wg, wu, wd, g, b, alpha):
    n, d = x.shape
    dff = wg.shape[1]
    tm = TOKEN_TILE
    assert n % tm == 0
    row = pl.BlockSpec((tm, d), lambda i: (i, 0))
    return pl.pallas_call(
        functools.partial(_ffn_ln_kernel, alpha=alpha),
        grid=(n // tm,),
        in_specs=[row, _resident((d, dff)), _resident((d, dff)), _resident((dff, d)),
                  _resident((1, d)), _resident((1, d))],
        out_specs=row,
        out_shape=jax.ShapeDtypeStruct((n, d), F32),
        compiler_params=pltpu.CompilerParams(
            dimension_semantics=("arbitrary",), vmem_limit_bytes=VMEM_LIMIT_BYTES),
        name="ffn_ln",
    )(x, wg, wu, wd, g, b)


def _shift_rows(x, k):
    return pltpu.roll(x, k, axis=0)


def _premix_kernel(*refs, seq_len, n_real, tiles_per_seq, d_rwkv, d_conv, full_out):
    (x_ref, win_ref, mu_ref, w0_ref, wup_ref, a0_ref, aup_ref, gup_ref, kk_ref, ka_ref,
     cw_ref, seg_ref) = refs[:12]
    refs = refs[12:]
    if tiles_per_seq == 0:
        sh0_ref, cv1_ref, cv2_ref = refs[:3]
        refs = refs[3:]
    (r_ref, nkk_ref, b_ref, k_ref, v_ref, lw_ref, g_ref, yc_ref) = refs[:8]
    refs = refs[8:]
    if full_out:
        ps_out_ref, u_out_ref = refs[:2]
        refs = refs[2:]
    else:
        shift_ref, tail_ref = refs[:2]
        refs = refs[2:]
    if tiles_per_seq:
        ps_carry, u_carry = refs

    p_shift = 3 * d_rwkv + LORA_W + LORA_A + LORA_G
    tm = x_ref.shape[0]
    proj = jnp.dot(x_ref[...].astype(BF16), win_ref[...], preferred_element_type=F32)
    ps = proj[:, :p_shift]
    c_b = proj[:, p_shift:p_shift + d_conv]
    u = proj[:, p_shift + d_conv:p_shift + 2 * d_conv] * proj[:, p_shift + 2 * d_conv:]

    row = lax.broadcasted_iota(jnp.int32, (tm, 1), 0)
    if tiles_per_seq:
        @pl.when(pl.program_id(0) % tiles_per_seq == 0)
        def _():
            ps_carry[...] = jnp.zeros_like(ps_carry)
            u_carry[...] = jnp.zeros_like(u_carry)

        prev = jnp.where(row == 0, ps_carry[...], _shift_rows(ps, 1))
        um1 = jnp.where(row == 0, u_carry[1:2, :], _shift_rows(u, 1))
        um2 = jnp.where(row == 0, u_carry[0:1, :],
                        jnp.where(row == 1, u_carry[1:2, :], _shift_rows(u, 2)))
        ps_carry[...] = ps[tm - 1:tm, :]
        u_carry[...] = u[tm - 2:tm, :]
        t_in = None
    else:
        t_in = row % seq_len
        prev = jnp.where(t_in == 0, sh0_ref[...], _shift_rows(ps, 1))
        um1 = jnp.where(t_in == 0, cv1_ref[...], _shift_rows(u, 1))
        um2 = jnp.where(t_in <= 1, cv2_ref[...], _shift_rows(u, 2))

    mix = ps + mu_ref[...] * (prev - ps)
    r = mix[:, 0:d_rwkv]
    k = mix[:, d_rwkv:2 * d_rwkv]
    v = mix[:, 2 * d_rwkv:3 * d_rwkv]
    o = 3 * d_rwkv
    wd = mix[:, o:o + LORA_W]
    ad = mix[:, o + LORA_W:o + LORA_W + LORA_A]
    gd = mix[:, o + LORA_W + LORA_A:p_shift]

    w_log = -jax.nn.softplus(-(w0_ref[...] + jnp.dot(
        jnp.tanh(wd).astype(BF16), wup_ref[...], preferred_element_type=F32))) - 0.5
    log_decay = -jnp.exp(w_log)
    a = jax.nn.sigmoid(a0_ref[...] + jnp.dot(
        ad.astype(BF16), aup_ref[...], preferred_element_type=F32))
    g = jnp.dot(jax.nn.sigmoid(gd).astype(BF16), gup_ref[...], preferred_element_type=F32)

    kk = k * kk_ref[...]
    ss = jnp.dot(kk * kk, seg_ref[...], preferred_element_type=F32, precision=HI)
    kk = kk / jnp.maximum(jnp.sqrt(ss), 1e-12)
    k = k * (1.0 + (a - 1.0) * ka_ref[...])
    nkk = -kk
    b = kk * a

    if n_real != seq_len:
        live = t_in < n_real
        zero = jnp.zeros_like(r)
        nkk = jnp.where(live, nkk, zero)
        b = jnp.where(live, b, zero)
        k = jnp.where(live, k, zero)
        v = jnp.where(live, v, zero)
        log_decay = jnp.where(live, log_decay, zero)

    r_ref[...] = r
    nkk_ref[...] = nkk
    b_ref[...] = b
    k_ref[...] = k
    v_ref[...] = v
    lw_ref[...] = log_decay
    g_ref[...] = g

    cw = cw_ref[...]
    z = cw[0:1, :] * um2
    z = z + cw[1:2, :] * um1
    z = z + cw[2:3, :] * u
    yc_ref[...] = (c_b * z).astype(BF16)

    if full_out:
        ps_out_ref[...] = ps
        u_out_ref[...] = u
    else:
        shift_ref[0] = ps[tm - 1:tm, :]
        tail_ref[0] = u[tm - (CONV_W - 1):tm, :]


def _premix(x1, p, seg, *, seq_len, n_real, init=None):
    n, d = x1.shape
    tm = TOKEN_TILE
    d_rwkv = p["w0"].shape[1]
    d_conv = p["conv_w"].shape[1]
    p_shift = 3 * d_rwkv + LORA_W + LORA_A + LORA_G
    p_total = p_shift + 3 * d_conv
    n_seq = n // seq_len
    if init is None:
        assert seq_len % tm == 0
        tiles_per_seq = seq_len // tm
    else:
        assert tm % seq_len == 0
        tiles_per_seq = 0
    full_out = init is not None

    def row(c):
        return pl.BlockSpec((tm, c), lambda i: (i, 0))

    in_specs = [row(d), _resident((d, p_total)), _resident((1, p_shift)),
                _resident((1, d_rwkv)), _resident((LORA_W, d_rwkv)),
                _resident((1, d_rwkv)), _resident((LORA_A, d_rwkv)),
                _resident((LORA_G, d_rwkv)), _resident((1, d_rwkv)), _resident((1, d_rwkv)),
                _resident((CONV_W, d_conv)), _resident((d_rwkv, d_rwkv))]
    args = [x1, p["w_in"], p["mu_shift"], p["w0"], p["w_lora_up"], p["a0"], p["a_lora_up"],
            p["g_lora_up"], p["k_k"], p["k_a"], p["conv_w"], seg]
    scratch = []
    if init is not None:
        in_specs += [row(p_shift), row(d_conv), row(d_conv)]
        args += list(init)
    else:
        scratch = [pltpu.VMEM((1, p_shift), F32), pltpu.VMEM((CONV_W - 1, d_conv), F32)]

    out_specs = [row(d_rwkv)] * 7 + [row(d_conv)]
    out_shape = [jax.ShapeDtypeStruct((n, d_rwkv), F32)] * 7 + [
        jax.ShapeDtypeStruct((n, d_conv), BF16)]
    if full_out:
        out_specs += [row(p_shift), row(d_conv)]
        out_shape += [jax.ShapeDtypeStruct((n, p_shift), F32),
                      jax.ShapeDtypeStruct((n, d_conv), F32)]
    else:
        out_specs += [
            pl.BlockSpec((1, 1, p_shift), lambda i: (i // tiles_per_seq, 0, 0)),
            pl.BlockSpec((1, CONV_W - 1, d_conv), lambda i: (i // tiles_per_seq, 0, 0))]
        out_shape += [jax.ShapeDtypeStruct((n_seq, 1, p_shift), F32),
                      jax.ShapeDtypeStruct((n_seq, CONV_W - 1, d_conv), F32)]

    return pl.pallas_call(
        functools.partial(_premix_kernel, seq_len=seq_len, n_real=n_real,
                          tiles_per_seq=tiles_per_seq, d_rwkv=d_rwkv, d_conv=d_conv,
                          full_out=full_out),
        grid=(n // tm,),
        in_specs=in_specs,
        out_specs=out_specs,
        out_shape=out_shape,
        scratch_shapes=scratch,
        compiler_params=pltpu.CompilerParams(
            dimension_semantics=("arbitrary",), vmem_limit_bytes=VMEM_LIMIT_BYTES),
        name="premix",
    )(*args)


def _bdot(a, b):
    return jnp.dot(a.astype(BF16), b.astype(BF16), preferred_element_type=F32)


def _bdot_nt(a, b):
    return lax.dot_general(a.astype(BF16), b.astype(BF16), (((1,), (1,)), ((), ())),
                           preferred_element_type=F32)


def _bdot_tn(a, b):
    return lax.dot_general(a.astype(BF16), b.astype(BF16), (((0,), (0,)), ((), ())),
                           preferred_element_type=F32)


def _wkv_kernel(r_ref, nkk_ref, b_ref, k_ref, v_ref, lw_ref, g_ref, s0_ref,
                rk_ref, lng_ref, lnb_ref, y_ref, sout_ref, s_scr, *, seqs, chunk, n_heads):
    c = pl.program_id(1)

    @pl.when(c == 0)
    def _():
        s_scr[...] = s0_ref[...]

    ii = lax.broadcasted_iota(jnp.int32, (chunk, chunk), 0)
    jj = lax.broadcasted_iota(jnp.int32, (chunk, chunk), 1)
    incl = ii >= jj
    strict = ii > jj
    tri = incl.astype(F32)
    n_doublings = chunk.bit_length() - 1

    at, rt, bt, kt, bh, kh, v, gam = [], [], [], [], [], [], [], []
    for q in range(seqs):
        lw = lw_ref[q]
        cum = jnp.dot(tri, lw, preferred_element_type=F32, precision=HI)
        last = cum[chunk - 1:chunk, :]
        e_in = jnp.exp(cum)
        e_out = jnp.exp(-cum)
        e_tail = jnp.exp(last - cum)
        a_t = nkk_ref[q] * jnp.exp(cum - lw)
        r_t = r_ref[q] * e_in
        b_t = b_ref[q] * e_out
        k_t = k_ref[q] * e_out
        b_h = b_ref[q] * e_tail
        k_h = k_ref[q] * e_tail
        gamma = jnp.exp(last)
        v_all = v_ref[q]
        for h in range(n_heads):
            hs = slice(h * HEAD, (h + 1) * HEAD)
            at.append(a_t[:, hs])
            rt.append(r_t[:, hs])
            bt.append(b_t[:, hs])
            kt.append(k_t[:, hs])
            bh.append(b_h[:, hs])
            kh.append(k_h[:, hs])
            v.append(v_all[:, hs])
            gam.append(gamma[:, hs])

    units = range(seqs * n_heads)
    m = [_bdot_nt(jnp.concatenate([at[u], rt[u]], axis=0),
                  jnp.concatenate([bt[u], kt[u]], axis=0)) for u in units]
    l_ab = [jnp.where(strict, m[u][:chunk, :chunk], 0.0) for u in units]
    l_ak = [jnp.where(strict, m[u][:chunk, chunk:], 0.0) for u in units]
    l_rb = [jnp.where(incl, m[u][chunk:, :chunk], 0.0) for u in units]
    l_rk = [jnp.where(incl, m[u][chunk:, chunk:], 0.0) for u in units]
    y = [jnp.concatenate([at[u], _bdot(l_ak[u], v[u])], axis=1) for u in units]
    pw = l_ab
    for step in range(n_doublings):
        if step < n_doublings - 1:
            py = [_bdot(pw[u], jnp.concatenate([pw[u], y[u]], axis=1)) for u in units]
            pw = [py[u][:, :chunk] for u in units]
            y = [y[u] + py[u][:, chunk:] for u in units]
        else:
            y = [y[u] + _bdot(pw[u], y[u]) for u in units]
    gh = [_bdot_tn(y[u], bh[u]) for u in units]
    vk = [_bdot_tn(v[u], kh[u]) for u in units]
    ly = [_bdot(l_rb[u], y[u]) for u in units]
    lv = [_bdot(l_rk[u], v[u]) for u in units]
    s = [s_scr[u // n_heads, u % n_heads] for u in units]
    o = [_bdot_nt(rt[u] + ly[u][:, :HEAD], s[u]) for u in units]
    sg = [_bdot(s[u], gh[u][:HEAD]) for u in units]
    for u in units:
        s_scr[u // n_heads, u % n_heads] = s[u] * gam[u] + sg[u] + gh[u][HEAD:] + vk[u]

    hsl = [slice((u % n_heads) * HEAD, (u % n_heads + 1) * HEAD) for u in units]
    o = [o[u] + ly[u][:, HEAD:] + lv[u] for u in units]
    mu = [jnp.mean(o[u], axis=-1, keepdims=True) for u in units]
    d = [o[u] - mu[u] for u in units]
    var = [jnp.mean(d[u] * d[u], axis=-1, keepdims=True) for u in units]
    bonus = [jnp.sum(r_ref[u // n_heads][:, hsl[u]] * k_ref[u // n_heads][:, hsl[u]]
                     * rk_ref[:, hsl[u]], axis=-1, keepdims=True) for u in units]
    for q in range(seqs):
        g_all = g_ref[q]
        y_heads = []
        for h in range(n_heads):
            u = q * n_heads + h
            gn = d[u] * lax.rsqrt(var[u] + GN_EPS) * lng_ref[:, hsl[u]] + lnb_ref[:, hsl[u]]
            y_heads.append((gn + bonus[u] * v[u]) * g_all[:, hsl[u]])
        y_ref[q] = jnp.concatenate(y_heads, axis=1).astype(BF16)

    @pl.when(c == pl.num_programs(1) - 1)
    def _():
        sout_ref[...] = s_scr[...]


def _wkv(ops, s0, rk, lng, lnb, *, seq_len, chunk, seqs):
    n, d_rwkv = ops[0].shape
    n_seq, n_heads = s0.shape[0], s0.shape[1]
    assert seq_len % chunk == 0 and n_seq % seqs == 0
    n_chunks = seq_len // chunk
    ops = [o.reshape(n_seq, seq_len, d_rwkv) for o in ops]
    tok = pl.BlockSpec((seqs, chunk, d_rwkv), lambda q, c: (q, c, 0))
    state = pl.BlockSpec((seqs, n_heads, HEAD, HEAD), lambda q, c: (q, 0, 0, 0))
    y, s_new = pl.pallas_call(
        functools.partial(_wkv_kernel, seqs=seqs, chunk=chunk, n_heads=n_heads),
        grid=(n_seq // seqs, n_chunks),
        in_specs=[tok] * 7 + [state] + [pl.BlockSpec((1, d_rwkv), lambda q, c: (0, 0))] * 3,
        out_specs=[tok, state],
        out_shape=[jax.ShapeDtypeStruct((n_seq, seq_len, d_rwkv), BF16),
                   jax.ShapeDtypeStruct(s0.shape, F32)],
        scratch_shapes=[pltpu.VMEM((seqs, n_heads, HEAD, HEAD), F32)],
        compiler_params=pltpu.CompilerParams(
            dimension_semantics=("arbitrary", "arbitrary"),
            vmem_limit_bytes=VMEM_LIMIT_BYTES),
        name="wkv",
    )(*ops, s0, rk, lng, lnb)
    return y.reshape(n, d_rwkv), s_new


def _out_ffn_kernel(x_ref, yr_ref, yc_ref, wo_r_ref, wo_c_ref, g2_ref, b2_ref,
                    wg_ref, wu_ref, wd_ref, g3_ref, b3_ref, o_ref, *, alpha):
    mix = jnp.dot(yr_ref[...], wo_r_ref[...], preferred_element_type=F32)
    mix = mix + jnp.dot(yc_ref[...], wo_c_ref[...], preferred_element_type=F32)
    x = _layer_norm(alpha * x_ref[...] + mix, g2_ref[...], b2_ref[...])
    y = alpha * x + 0.5 * _swiglu(x.astype(BF16), wg_ref, wu_ref, wd_ref)
    o_ref[...] = _layer_norm(y, g3_ref[...], b3_ref[...])


def _out_ffn(x1, y_rwkv, y_conv, wo_r, wo_c, g2, b2, wg, wu, wd, g3, b3, alpha):
    n, d = x1.shape
    dff = wg.shape[1]
    d_rwkv, d_conv = y_rwkv.shape[1], y_conv.shape[1]
    tm = TOKEN_TILE
    assert n % tm == 0

    def row(c):
        return pl.BlockSpec((tm, c), lambda i: (i, 0))

    return pl.pallas_call(
        functools.partial(_out_ffn_kernel, alpha=alpha),
        grid=(n // tm,),
        in_specs=[row(d), row(d_rwkv), row(d_conv), _resident((d_rwkv, d)), _resident((d_conv, d)),
                  _resident((1, d)), _resident((1, d)),
                  _resident((d, dff)), _resident((d, dff)), _resident((dff, d)),
                  _resident((1, d)), _resident((1, d))],
        out_specs=row(d),
        out_shape=jax.ShapeDtypeStruct((n, d), F32),
        compiler_params=pltpu.CompilerParams(
            dimension_semantics=("arbitrary",), vmem_limit_bytes=VMEM_LIMIT_BYTES),
        name="out_ffn",
    )(x1, y_rwkv, y_conv, wo_r, wo_c, g2, b2, wg, wu, wd, g3, b3)


def _trunk_layer(x, p, seg, alpha, *, seq_len, n_real, chunk, seqs, wkv0, init):
    x1 = _ffn_ln(x, p["ffn1_wg"], p["ffn1_wu"], p["ffn1_wd"], p["ln1_g"], p["ln1_b"], alpha)
    pm = _premix(x1, p, seg, seq_len=seq_len, n_real=n_real, init=init)
    ops, y_conv, st_a, st_b = pm[:7], pm[7], pm[8], pm[9]
    y_rwkv, wkv_new = _wkv(ops, wkv0, p["r_k"], p["lnx_g"], p["lnx_b"],
                           seq_len=seq_len, chunk=chunk, seqs=seqs)
    d_rwkv = y_rwkv.shape[1]
    y = _out_ffn(x1, y_rwkv, y_conv, p["w_out"][:d_rwkv], p["w_out"][d_rwkv:],
                 p["ln2_g"], p["ln2_b"], p["ffn2_wg"], p["ffn2_wu"], p["ffn2_wd"],
                 p["ln3_g"], p["ln3_b"], alpha)
    return y, wkv_new, st_a, st_b


def _expand_rows(x, seq_len):
    n_seq, k, c = x.shape
    return jnp.pad(x, ((0, 0), (0, seq_len - k), (0, 0))).reshape(n_seq * seq_len, c)


def kernel(x_prompt, x_sample, state_wkv, state_shift, state_conv, ln1_g, ln1_b, ffn1_wg, ffn1_wu, ffn1_wd, w_in, mu_shift, w0, w_lora_up, a0, a_lora_up, g_lora_up, k_k, k_a, r_k, lnx_g, lnx_b, conv_w, w_out, ln2_g, ln2_b, ffn2_wg, ffn2_wu, ffn2_wd, ln3_g, ln3_b):
    depth = ln1_g.shape[0]
    bp, tp, d = x_prompt.shape
    bs, ts, _ = x_sample.shape
    n_heads = state_wkv.shape[2]
    d_rwkv = n_heads * HEAD
    alpha = (2.0 * depth) ** 0.25
    assert ts <= SAMPLE_PAD_T

    head_id = jnp.arange(d_rwkv) // HEAD
    seg = (head_id[:, None] == head_id[None, :]).astype(F32)

    xp = x_prompt.reshape(bp * tp, d)
    xs = jnp.pad(x_sample, ((0, 0), (0, SAMPLE_PAD_T - ts), (0, 0))).reshape(bs * SAMPLE_PAD_T, d)

    outs = {k: [] for k in ("wkv_p", "shift_p", "conv_p", "wkv_s", "shift_s", "conv_s")}
    for l in range(depth):
        vec = lambda a: a[l].reshape(1, -1).astype(F32)
        p = {
            "ln1_g": vec(ln1_g), "ln1_b": vec(ln1_b),
            "ffn1_wg": ffn1_wg[l].astype(BF16), "ffn1_wu": ffn1_wu[l].astype(BF16),
            "ffn1_wd": ffn1_wd[l].astype(BF16),
            "w_in": w_in[l].astype(BF16), "mu_shift": vec(mu_shift), "w0": vec(w0),
            "w_lora_up": w_lora_up[l].astype(BF16), "a0": vec(a0),
            "a_lora_up": a_lora_up[l].astype(BF16), "g_lora_up": g_lora_up[l].astype(BF16),
            "k_k": vec(k_k), "k_a": vec(k_a), "r_k": vec(r_k),
            "lnx_g": vec(lnx_g), "lnx_b": vec(lnx_b),
            "conv_w": conv_w[l].astype(F32), "w_out": w_out[l].astype(BF16),
            "ln2_g": vec(ln2_g), "ln2_b": vec(ln2_b),
            "ffn2_wg": ffn2_wg[l].astype(BF16), "ffn2_wu": ffn2_wu[l].astype(BF16),
            "ffn2_wd": ffn2_wd[l].astype(BF16),
            "ln3_g": vec(ln3_g), "ln3_b": vec(ln3_b),
        }
        wkv0_p = jnp.zeros((bp, n_heads, HEAD, HEAD), F32)
        xp, wp, sp, cp = _trunk_layer(xp, p, seg, alpha, seq_len=tp, n_real=tp, chunk=CHUNK,
                                      seqs=2, wkv0=wkv0_p, init=None)
        conv0 = state_conv[l].astype(F32)
        init = (_expand_rows(state_shift[l].astype(F32)[:, None, :], SAMPLE_PAD_T),
                _expand_rows(conv0[:, 1:2, :], SAMPLE_PAD_T),
                _expand_rows(conv0, SAMPLE_PAD_T))
        xs, wsm, ps_s, u_s = _trunk_layer(xs, p, seg, alpha, seq_len=SAMPLE_PAD_T, n_real=ts,
                                          chunk=SAMPLE_PAD_T, seqs=8,
                                          wkv0=state_wkv[l].astype(F32), init=init)
        outs["wkv_p"].append(wp)
        outs["shift_p"].append(sp[:, 0, :])
        outs["conv_p"].append(cp)
        outs["wkv_s"].append(wsm)
        outs["shift_s"].append(ps_s.reshape(bs, SAMPLE_PAD_T, -1)[:, ts - 1, :])
        outs["conv_s"].append(u_s.reshape(bs, SAMPLE_PAD_T, -1)[:, ts - (CONV_W - 1):ts, :])

    y_prompt = xp.reshape(bp, tp, d)
    y_sample = xs.reshape(bs, SAMPLE_PAD_T, d)[:, :ts, :]
    return (y_prompt, y_sample,
            jnp.stack(outs["wkv_p"]), jnp.stack(outs["shift_p"]), jnp.stack(outs["conv_p"]),
            jnp.stack(outs["wkv_s"]), jnp.stack(outs["shift_s"]), jnp.stack(outs["conv_s"]))
```

```python
import functools

import jax
import jax.numpy as jnp
from jax import lax
from jax.experimental import pallas as pl
from jax.experimental.pallas import tpu as pltpu

F32 = jnp.float32
BF16 = jnp.bfloat16
HI = lax.Precision.HIGHEST

HEAD = 64
LORA_W = 64
LORA_A = 64
LORA_G = 128
CONV_W = 3
LN_EPS = 1e-5
GN_EPS = 1e-5 * HEAD

VMEM_LIMIT_BYTES = 56 * 1024 * 1024
TOKEN_TILE = 512
CHUNK = 64
SAMPLE_PAD_T = 8


def _resident(shape):
    nd = len(shape)
    return pl.BlockSpec(shape, lambda *_: (0,) * nd, pipeline_mode=pl.Buffered(1))


def _layer_norm(x, g, b):
    mu = jnp.mean(x, axis=-1, keepdims=True)
    d = x - mu
    var = jnp.mean(d * d, axis=-1, keepdims=True)
    return d * lax.rsqrt(var + LN_EPS) * g + b


def _swiglu(xb, wg_ref, wu_ref, wd_ref):
    g = jnp.dot(xb, wg_ref[...], preferred_element_type=F32)
    u = jnp.dot(xb, wu_ref[...], preferred_element_type=F32)
    h = (jax.nn.silu(g) * u).astype(BF16)
    return jnp.dot(h, wd_ref[...], preferred_element_type=F32)


def _ffn_ln_kernel(x_ref, wg_ref, wu_ref, wd_ref, g_ref, b_ref, o_ref, *, alpha):
    x = x_ref[...]
    y = alpha * x + 0.5 * _swiglu(x.astype(BF16), wg_ref, wu_ref, wd_ref)
    o_ref[...] = _layer_norm(y, g_ref[...], b_ref[...])


def _ffn_ln(x, wg, wu, wd, g, b, alpha):
    n, d = x.shape
    dff = wg.shape[1]
    tm = TOKEN_TILE
    assert n % tm == 0
    row = pl.BlockSpec((tm, d), lambda i: (i, 0))
    return pl.pallas_call(
        functools.partial(_ffn_ln_kernel, alpha=alpha),
        grid=(n // tm,),
        in_specs=[row, _resident((d, dff)), _resident((d, dff)), _resident((dff, d)),
                  _resident((1, d)), _resident((1, d))],
        out_specs=row,
        out_shape=jax.ShapeDtypeStruct((n, d), F32),
        compiler_params=pltpu.CompilerParams(
            dimension_semantics=("arbitrary",), vmem_limit_bytes=VMEM_LIMIT_BYTES),
        name="ffn_ln",
    )(x, wg, wu, wd, g, b)


def _shift_rows(x, k):
    return pltpu.roll(x, k, axis=0)


def _premix_kernel(*refs, seq_len, n_real, tiles_per_seq, d_rwkv, d_conv, full_out):
    (x_ref, win_ref, mu_ref, w0_ref, wup_ref, a0_ref, aup_ref, gup_ref, kk_ref, ka_ref,
     cw_ref, seg_ref) = refs[:12]
    refs = refs[12:]
    if tiles_per_seq == 0:
        sh0_ref, cv1_ref, cv2_ref = refs[:3]
        refs = refs[3:]
    (r_ref, nkk_ref, b_ref, k_ref, v_ref, lw_ref, g_ref, yc_ref) = refs[:8]
    refs = refs[8:]
    if full_out:
        ps_out_ref, u_out_ref = refs[:2]
        refs = refs[2:]
    else:
        shift_ref, tail_ref = refs[:2]
        refs = refs[2:]
    if tiles_per_seq:
        ps_carry, u_carry = refs

    p_shift = 3 * d_rwkv + LORA_W + LORA_A + LORA_G
    tm = x_ref.shape[0]
    proj = jnp.dot(x_ref[...].astype(BF16), win_ref[...], preferred_element_type=F32)
    ps = proj[:, :p_shift]
    c_b = proj[:, p_shift:p_shift + d_conv]
    u = proj[:, p_shift + d_conv:p_shift + 2 * d_conv] * proj[:, p_shift + 2 * d_conv:]

    row = lax.broadcasted_iota(jnp.int32, (tm, 1), 0)
    if tiles_per_seq:
        @pl.when(pl.program_id(0) % tiles_per_seq == 0)
        def _():
            ps_carry[...] = jnp.zeros_like(ps_carry)
            u_carry[...] = jnp.zeros_like(u_carry)

        prev = jnp.where(row == 0, ps_carry[...], _shift_rows(ps, 1))
        um1 = jnp.where(row == 0, u_carry[1:2, :], _shift_rows(u, 1))
        um2 = jnp.where(row == 0, u_carry[0:1, :],
                        jnp.where(row == 1, u_carry[1:2, :], _shift_rows(u, 2)))
        ps_carry[...] = ps[tm - 1:tm, :]
        u_carry[...] = u[tm - 2:tm, :]
        t_in = None
    else:
        t_in = row % seq_len
        prev = jnp.where(t_in == 0, sh0_ref[...], _shift_rows(ps, 1))
        um1 = jnp.where(t_in == 0, cv1_ref[...], _shift_rows(u, 1))
        um2 = jnp.where(t_in <= 1, cv2_ref[...], _shift_rows(u, 2))

    mix = ps + mu_ref[...] * (prev - ps)
    r = mix[:, 0:d_rwkv]
    k = mix[:, d_rwkv:2 * d_rwkv]
    v = mix[:, 2 * d_rwkv:3 * d_rwkv]
    o = 3 * d_rwkv
    wd = mix[:, o:o + LORA_W]
    ad = mix[:, o + LORA_W:o + LORA_W + LORA_A]
    gd = mix[:, o + LORA_W + LORA_A:p_shift]

    w_log = -jax.nn.softplus(-(w0_ref[...] + jnp.dot(
        jnp.tanh(wd).astype(BF16), wup_ref[...], preferred_element_type=F32))) - 0.5
    log_decay = -jnp.exp(w_log)
    a = jax.nn.sigmoid(a0_ref[...] + jnp.dot(
        ad.astype(BF16), aup_ref[...], preferred_element_type=F32))
    g = jnp.dot(jax.nn.sigmoid(gd).astype(BF16), gup_ref[...], preferred_element_type=F32)

    kk = k * kk_ref[...]
    ss = jnp.dot(kk * kk, seg_ref[...], preferred_element_type=F32, precision=HI)
    kk = kk / jnp.maximum(jnp.sqrt(ss), 1e-12)
    k = k * (1.0 + (a - 1.0) * ka_ref[...])
    nkk = -kk
    b = kk * a

    if n_real != seq_len:
        live = t_in < n_real
        zero = jnp.zeros_like(r)
        nkk = jnp.where(live, nkk, zero)
        b = jnp.where(live, b, zero)
        k = jnp.where(live, k, zero)
        v = jnp.where(live, v, zero)
        log_decay = jnp.where(live, log_decay, zero)

    r_ref[...] = r
    nkk_ref[...] = nkk
    b_ref[...] = b
    k_ref[...] = k
    v_ref[...] = v
    lw_ref[...] = log_decay
    g_ref[...] = g

    cw = cw_ref[...]
    z = cw[0:1, :] * um2
    z = z + cw[1:2, :] * um1
    z = z + cw[2:3, :] * u
    yc_ref[...] = (c_b * z).astype(BF16)

    if full_out:
        ps_out_ref[...] = ps
        u_out_ref[...] = u
    else:
        shift_ref[0] = ps[tm - 1:tm, :]
        tail_ref[0] = u[tm - (CONV_W - 1):tm, :]


def _premix(x1, p, seg, *, seq_len, n_real, init=None):
    n, d = x1.shape
    tm = TOKEN_TILE
    d_rwkv = p["w0"].shape[1]
    d_conv = p["conv_w"].shape[1]
    p_shift = 3 * d_rwkv + LORA_W + LORA_A + LORA_G
    p_total = p_shift + 3 * d_conv
    n_seq = n // seq_len
    if init is None:
        assert seq_len % tm == 0
        tiles_per_seq = seq_len // tm
    else:
        assert tm % seq_len == 0
        tiles_per_seq = 0
    full_out = init is not None

    def row(c):
        return pl.BlockSpec((tm, c), lambda i: (i, 0))

    in_specs = [row(d), _resident((d, p_total)), _resident((1, p_shift)),
                _resident((1, d_rwkv)), _resident((LORA_W, d_rwkv)),
                _resident((1, d_rwkv)), _resident((LORA_A, d_rwkv)),
                _resident((LORA_G, d_rwkv)), _resident((1, d_rwkv)), _resident((1, d_rwkv)),
                _resident((CONV_W, d_conv)), _resident((d_rwkv, d_rwkv))]
    args = [x1, p["w_in"], p["mu_shift"], p["w0"], p["w_lora_up"], p["a0"], p["a_lora_up"],
            p["g_lora_up"], p["k_k"], p["k_a"], p["conv_w"], seg]
    scratch = []
    if init is not None:
        in_specs += [row(p_shift), row(d_conv), row(d_conv)]
        args += list(init)
    else:
        scratch = [pltpu.VMEM((1, p_shift), F32), pltpu.VMEM((CONV_W - 1, d_conv), F32)]

    out_specs = [row(d_rwkv)] * 7 + [row(d_conv)]
    out_shape = [jax.ShapeDtypeStruct((n, d_rwkv), F32)] * 7 + [
        jax.ShapeDtypeStruct((n, d_conv), BF16)]
    if full_out:
        out_specs += [row(p_shift), row(d_conv)]
        out_shape += [jax.ShapeDtypeStruct((n, p_shift), F32),
                      jax.ShapeDtypeStruct((n, d_conv), F32)]
    else:
        out_specs += [
            pl.BlockSpec((1, 1, p_shift), lambda i: (i // tiles_per_seq, 0, 0)),
            pl.BlockSpec((1, CONV_W - 1, d_conv), lambda i: (i // tiles_per_seq, 0, 0))]
        out_shape += [jax.ShapeDtypeStruct((n_seq, 1, p_shift), F32),
                      jax.ShapeDtypeStruct((n_seq, CONV_W - 1, d_conv), F32)]

    return pl.pallas_call(
        functools.partial(_premix_kernel, seq_len=seq_len, n_real=n_real,
                          tiles_per_seq=tiles_per_seq, d_rwkv=d_rwkv, d_conv=d_conv,
                          full_out=full_out),
        grid=(n // tm,),
        in_specs=in_specs,
        out_specs=out_specs,
        out_shape=out_shape,
        scratch_shapes=scratch,
        compiler_params=pltpu.CompilerParams(
            dimension_semantics=("arbitrary",), vmem_limit_bytes=VMEM_LIMIT_BYTES),
        name="premix",
    )(*args)


def _bdot(a, b):
    return jnp.dot(a.astype(BF16), b.astype(BF16), preferred_element_type=F32)


def _bdot_nt(a, b):
    return lax.dot_general(a.astype(BF16), b.astype(BF16), (((1,), (1,)), ((), ())),
                           preferred_element_type=F32)


def _bdot_tn(a, b):
    return lax.dot_general(a.astype(BF16), b.astype(BF16), (((0,), (0,)), ((), ())),
                           preferred_element_type=F32)


def _wkv_kernel(r_ref, nkk_ref, b_ref, k_ref, v_ref, lw_ref, g_ref, s0_ref,
                rk_ref, lng_ref, lnb_ref, y_ref, sout_ref, s_scr, *, seqs, chunk, n_heads):
    c = pl.program_id(1)
    n_pairs = n_heads // 2
    c2 = 2 * chunk
    lanes = 2 * HEAD

    @pl.when(c == 0)
    def _():
        for q in range(seqs):
            for p in range(n_pairs):
                s_scr[q, p] = jnp.concatenate([s0_ref[q, 2 * p], s0_ref[q, 2 * p + 1]], axis=1)

    ii = lax.broadcasted_iota(jnp.int32, (chunk, chunk), 0)
    jj = lax.broadcasted_iota(jnp.int32, (chunk, chunk), 1)
    tri = (ii >= jj).astype(F32)
    n_doublings = chunk.bit_length() - 1

    ri = lax.broadcasted_iota(jnp.int32, (c2, c2), 0)
    ci = lax.broadcasted_iota(jnp.int32, (c2, c2), 1)
    same = (ri // chunk) == (ci // chunk)
    strict = same & (ri % chunk > ci % chunk)
    incl = same & (ri % chunk >= ci % chunk)
    row_head = lax.broadcasted_iota(jnp.int32, (c2, lanes), 0) // chunk
    lane_head = lax.broadcasted_iota(jnp.int32, (c2, lanes), 1) // HEAD
    own = row_head == lane_head
    lo = lane_head == 0
    g_own = (lax.broadcasted_iota(jnp.int32, (lanes, lanes), 0) // HEAD
             == lax.broadcasted_iota(jnp.int32, (lanes, lanes), 1) // HEAD)
    lo_c = lax.broadcasted_iota(jnp.int32, (chunk, lanes), 1) < HEAD

    def stack(x):
        return jnp.concatenate([x, x], axis=0)

    def bd(x):
        return jnp.where(own, stack(x), 0.0)

    def own_half(x, low):
        swapped = pltpu.roll(x, HEAD, axis=1)
        rows = jnp.concatenate([x, swapped] if low else [swapped, x], axis=0)
        return jnp.where(lo if low else ~lo, rows, 0.0)

    ar_bd, bk_st, a_lo, r_lo, v_hi, bkh_bd, gam, unit = [], [], [], [], [], [], [], []
    for q in range(seqs):
        lw = lw_ref[q]
        cum = jnp.dot(tri, lw, preferred_element_type=F32, precision=HI)
        last = cum[chunk - 1:chunk, :]
        e_in = jnp.exp(cum)
        e_out = jnp.exp(-cum)
        e_tail = jnp.exp(last - cum)
        a_t = nkk_ref[q] * jnp.exp(cum - lw)
        r_t = r_ref[q] * e_in
        b_t = b_ref[q] * e_out
        k_t = k_ref[q] * e_out
        b_h = b_ref[q] * e_tail
        k_h = k_ref[q] * e_tail
        gamma = jnp.exp(last)
        v_all = v_ref[q]
        for p in range(n_pairs):
            ps = slice(p * lanes, (p + 1) * lanes)
            ar_bd.append(jnp.concatenate([bd(a_t[:, ps]), bd(r_t[:, ps])], axis=0))
            bk_st.append(jnp.concatenate([stack(b_t[:, ps]), stack(k_t[:, ps])], axis=0))
            a_lo.append(own_half(a_t[:, ps], True))
            r_lo.append(own_half(r_t[:, ps], True))
            v_hi.append(own_half(v_all[:, ps], False))
            bkh_bd.append(jnp.concatenate([bd(b_h[:, ps]), bd(k_h[:, ps])], axis=0))
            gam.append(gamma[:, ps])
            unit.append((q, p, ps))

    units = range(seqs * n_pairs)
    m = [_bdot_nt(ar_bd[u], bk_st[u]) for u in units]
    l_ab = [jnp.where(strict, m[u][:c2, :c2], 0.0) for u in units]
    l_ak = [jnp.where(strict, m[u][:c2, c2:], 0.0) for u in units]
    l_rbk = [jnp.concatenate([jnp.where(incl, m[u][c2:, :c2], 0.0),
                              jnp.where(incl, m[u][c2:, c2:], 0.0)], axis=1) for u in units]
    y = [a_lo[u] + _bdot(l_ak[u], v_hi[u]) for u in units]
    pw = l_ab
    for step in range(n_doublings):
        if step < n_doublings - 1:
            py = [_bdot(pw[u], jnp.concatenate([y[u], pw[u]], axis=1)) for u in units]
            pw = [py[u][:, lanes:] for u in units]
            y = [y[u] + py[u][:, :lanes] for u in units]
        else:
            y = [y[u] + _bdot(pw[u], y[u]) for u in units]
    yv = [jnp.concatenate([y[u], v_hi[u]], axis=0) for u in units]
    gh = [_bdot_tn(yv[u], bkh_bd[u]) for u in units]
    lyv = [_bdot(l_rbk[u], yv[u]) for u in units]
    q_lo = [jnp.where(lo, r_lo[u] + lyv[u], 0.0) for u in units]
    q_bd = [jnp.where(own, q_lo[u] + pltpu.roll(q_lo[u], HEAD, axis=1), 0.0) for u in units]
    s = [s_scr[unit[u][0], unit[u][1]] for u in units]
    zero_s = jnp.zeros((HEAD, lanes), F32)
    o = [_bdot_nt(q_bd[u], jnp.concatenate([zero_s, s[u]], axis=0)) for u in units]
    g_bd = [jnp.where(g_own, stack(gh[u][:HEAD]), 0.0) for u in units]
    sg = [_bdot(s[u], g_bd[u]) for u in units]
    for u in units:
        s_scr[unit[u][0], unit[u][1]] = s[u] * gam[u] + sg[u] + gh[u][HEAD:]

    o = [jnp.where(lo, 0.0, o[u] + lyv[u]) for u in units]
    mu = [jnp.sum(o[u], axis=-1, keepdims=True) * (1.0 / HEAD) for u in units]
    d = [jnp.where(lo, 0.0, o[u] - mu[u]) for u in units]
    var = [jnp.sum(d[u] * d[u], axis=-1, keepdims=True) * (1.0 / HEAD) for u in units]
    rk_prod = [r_ref[unit[u][0]][:, unit[u][2]] * k_ref[unit[u][0]][:, unit[u][2]]
               * rk_ref[:, unit[u][2]] for u in units]
    bonus_0 = [jnp.sum(jnp.where(lo_c, rk_prod[u], 0.0), axis=-1, keepdims=True) for u in units]
    bonus_1 = [jnp.sum(jnp.where(lo_c, 0.0, rk_prod[u]), axis=-1, keepdims=True) for u in units]
    for q in range(seqs):
        y_pairs = []
        for p in range(n_pairs):
            u = q * n_pairs + p
            ps = unit[u][2]
            gn = d[u] * lax.rsqrt(var[u] + GN_EPS)
            gn = jnp.where(lo_c, pltpu.roll(gn[:chunk], HEAD, axis=1), gn[chunk:])
            gn = gn * lng_ref[:, ps] + lnb_ref[:, ps]
            bonus = jnp.where(lo_c, bonus_0[u], bonus_1[u])
            y_pairs.append((gn + bonus * v_ref[q][:, ps]) * g_ref[q][:, ps])
        y_ref[q] = jnp.concatenate(y_pairs, axis=1).astype(BF16)

    @pl.when(c == pl.num_programs(1) - 1)
    def _():
        for q in range(seqs):
            for p in range(n_pairs):
                sp = s_scr[q, p]
                sout_ref[q, 2 * p] = sp[:, :HEAD]
                sout_ref[q, 2 * p + 1] = sp[:, HEAD:]


def _wkv(ops, s0, rk, lng, lnb, *, seq_len, chunk, seqs):
    n, d_rwkv = ops[0].shape
    n_seq, n_heads = s0.shape[0], s0.shape[1]
    assert seq_len % chunk == 0 and n_seq % seqs == 0
    n_chunks = seq_len // chunk
    ops = [o.reshape(n_seq, seq_len, d_rwkv) for o in ops]
    tok = pl.BlockSpec((seqs, chunk, d_rwkv), lambda q, c: (q, c, 0))
    state = pl.BlockSpec((seqs, n_heads, HEAD, HEAD), lambda q, c: (q, 0, 0, 0))
    y, s_new = pl.pallas_call(
        functools.partial(_wkv_kernel, seqs=seqs, chunk=chunk, n_heads=n_heads),
        grid=(n_seq // seqs, n_chunks),
        in_specs=[tok] * 7 + [state] + [pl.BlockSpec((1, d_rwkv), lambda q, c: (0, 0))] * 3,
        out_specs=[tok, state],
        out_shape=[jax.ShapeDtypeStruct((n_seq, seq_len, d_rwkv), BF16),
                   jax.ShapeDtypeStruct(s0.shape, F32)],
        scratch_shapes=[pltpu.VMEM((seqs, n_heads // 2, HEAD, 2 * HEAD), F32)],
        compiler_params=pltpu.CompilerParams(
            dimension_semantics=("arbitrary", "arbitrary"),
            vmem_limit_bytes=VMEM_LIMIT_BYTES),
        name="wkv",
    )(*ops, s0, rk, lng, lnb)
    return y.reshape(n, d_rwkv), s_new


def _out_ffn_kernel(x_ref, yr_ref, yc_ref, wo_r_ref, wo_c_ref, g2_ref, b2_ref,
                    wg_ref, wu_ref, wd_ref, g3_ref, b3_ref, o_ref, *, alpha):
    mix = jnp.dot(yr_ref[...], wo_r_ref[...], preferred_element_type=F32)
    mix = mix + jnp.dot(yc_ref[...], wo_c_ref[...], preferred_element_type=F32)
    x = _layer_norm(alpha * x_ref[...] + mix, g2_ref[...], b2_ref[...])
    y = alpha * x + 0.5 * _swiglu(x.astype(BF16), wg_ref, wu_ref, wd_ref)
    o_ref[...] = _layer_norm(y, g3_ref[...], b3_ref[...])


def _out_ffn(x1, y_rwkv, y_conv, wo_r, wo_c, g2, b2, wg, wu, wd, g3, b3, alpha):
    n, d = x1.shape
    dff = wg.shape[1]
    d_rwkv, d_conv = y_rwkv.shape[1], y_conv.shape[1]
    tm = TOKEN_TILE
    assert n % tm == 0

    def row(c):
        return pl.BlockSpec((tm, c), lambda i: (i, 0))

    return pl.pallas_call(
        functools.partial(_out_ffn_kernel, alpha=alpha),
        grid=(n // tm,),
        in_specs=[row(d), row(d_rwkv), row(d_conv), _resident((d_rwkv, d)), _resident((d_conv, d)),
                  _resident((1, d)), _resident((1, d)),
                  _resident((d, dff)), _resident((d, dff)), _resident((dff, d)),
                  _resident((1, d)), _resident((1, d))],
        out_specs=row(d),
        out_shape=jax.ShapeDtypeStruct((n, d), F32),
        compiler_params=pltpu.CompilerParams(
            dimension_semantics=("arbitrary",), vmem_limit_bytes=VMEM_LIMIT_BYTES),
        name="out_ffn",
    )(x1, y_rwkv, y_conv, wo_r, wo_c, g2, b2, wg, wu, wd, g3, b3)


def _trunk_layer(x, p, seg, alpha, *, seq_len, n_real, chunk, seqs, wkv0, init):
    x1 = _ffn_ln(x, p["ffn1_wg"], p["ffn1_wu"], p["ffn1_wd"], p["ln1_g"], p["ln1_b"], alpha)
    pm = _premix(x1, p, seg, seq_len=seq_len, n_real=n_real, init=init)
    ops, y_conv, st_a, st_b = pm[:7], pm[7], pm[8], pm[9]
    y_rwkv, wkv_new = _wkv(ops, wkv0, p["r_k"], p["lnx_g"], p["lnx_b"],
                           seq_len=seq_len, chunk=chunk, seqs=seqs)
    d_rwkv = y_rwkv.shape[1]
    y = _out_ffn(x1, y_rwkv, y_conv, p["w_out"][:d_rwkv], p["w_out"][d_rwkv:],
                 p["ln2_g"], p["ln2_b"], p["ffn2_wg"], p["ffn2_wu"], p["ffn2_wd"],
                 p["ln3_g"], p["ln3_b"], alpha)
    return y, wkv_new, st_a, st_b


def _expand_rows(x, seq_len):
    n_seq, k, c = x.shape
    return jnp.pad(x, ((0, 0), (0, seq_len - k), (0, 0))).reshape(n_seq * seq_len, c)


def kernel(x_prompt, x_sample, state_wkv, state_shift, state_conv, ln1_g, ln1_b, ffn1_wg, ffn1_wu, ffn1_wd, w_in, mu_shift, w0, w_lora_up, a0, a_lora_up, g_lora_up, k_k, k_a, r_k, lnx_g, lnx_b, conv_w, w_out, ln2_g, ln2_b, ffn2_wg, ffn2_wu, ffn2_wd, ln3_g, ln3_b):
    depth = ln1_g.shape[0]
    bp, tp, d = x_prompt.shape
    bs, ts, _ = x_sample.shape
    n_heads = state_wkv.shape[2]
    d_rwkv = n_heads * HEAD
    alpha = (2.0 * depth) ** 0.25
    assert ts <= SAMPLE_PAD_T

    head_id = jnp.arange(d_rwkv) // HEAD
    seg = (head_id[:, None] == head_id[None, :]).astype(F32)

    xp = x_prompt.reshape(bp * tp, d)
    xs = jnp.pad(x_sample, ((0, 0), (0, SAMPLE_PAD_T - ts), (0, 0))).reshape(bs * SAMPLE_PAD_T, d)

    outs = {k: [] for k in ("wkv_p", "shift_p", "conv_p", "wkv_s", "shift_s", "conv_s")}
    for l in range(depth):
        vec = lambda a: a[l].reshape(1, -1).astype(F32)
        p = {
            "ln1_g": vec(ln1_g), "ln1_b": vec(ln1_b),
            "ffn1_wg": ffn1_wg[l].astype(BF16), "ffn1_wu": ffn1_wu[l].astype(BF16),
            "ffn1_wd": ffn1_wd[l].astype(BF16),
            "w_in": w_in[l].astype(BF16), "mu_shift": vec(mu_shift), "w0": vec(w0),
            "w_lora_up": w_lora_up[l].astype(BF16), "a0": vec(a0),
            "a_lora_up": a_lora_up[l].astype(BF16), "g_lora_up": g_lora_up[l].astype(BF16),
            "k_k": vec(k_k), "k_a": vec(k_a), "r_k": vec(r_k),
            "lnx_g": vec(lnx_g), "lnx_b": vec(lnx_b),
            "conv_w": conv_w[l].astype(F32), "w_out": w_out[l].astype(BF16),
            "ln2_g": vec(ln2_g), "ln2_b": vec(ln2_b),
            "ffn2_wg": ffn2_wg[l].astype(BF16), "ffn2_wu": ffn2_wu[l].astype(BF16),
            "ffn2_wd": ffn2_wd[l].astype(BF16),
            "ln3_g": vec(ln3_g), "ln3_b": vec(ln3_b),
        }
        wkv0_p = jnp.zeros((bp, n_heads, HEAD, HEAD), F32)
        xp, wp, sp, cp = _trunk_layer(xp, p, seg, alpha, seq_len=tp, n_real=tp, chunk=CHUNK,
                                      seqs=2, wkv0=wkv0_p, init=None)
        conv0 = state_conv[l].astype(F32)
        init = (_expand_rows(state_shift[l].astype(F32)[:, None, :], SAMPLE_PAD_T),
                _expand_rows(conv0[:, 1:2, :], SAMPLE_PAD_T),
                _expand_rows(conv0, SAMPLE_PAD_T))
        xs, wsm, ps_s, u_s = _trunk_layer(xs, p, seg, alpha, seq_len=SAMPLE_PAD_T, n_real=ts,
                                          chunk=SAMPLE_PAD_T, seqs=8,
                                          wkv0=state_wkv[l].astype(F32), init=init)
        outs["wkv_p"].append(wp)
        outs["shift_p"].append(sp[:, 0, :])
        outs["conv_p"].append(cp)
        outs["wkv_s"].append(wsm)
        outs["shift_s"].append(ps_s.reshape(bs, SAMPLE_PAD_T, -1)[:, ts - 1, :])
        outs["conv_s"].append(u_s.reshape(bs, SAMPLE_PAD_T, -1)[:, ts - (CONV_W - 1):ts, :])

    y_prompt = xp.reshape(bp, tp, d)
    y_sample = xs.reshape(bs, SAMPLE_PAD_T, d)[:, :ts, :]
    return (y_prompt, y_sample,
            jnp.stack(outs["wkv_p"]), jnp.stack(outs["shift_p"]), jnp.stack(outs["conv_p"]),
            jnp.stack(outs["wkv_s"]), jnp.stack(outs["shift_s"]), jnp.stack(outs["conv_s"]))
```

```python
import functools

import jax
import jax.numpy as jnp
from jax import lax
from jax.experimental import pallas as pl
from jax.experimental.pallas import tpu as pltpu

F32 = jnp.float32
BF16 = jnp.bfloat16

HEAD = 64
LORA_W = 64
LORA_A = 64
LORA_G = 128
CONV_W = 3
LN_EPS = 1e-5
GN_EPS = 1e-5 * HEAD

VMEM_LIMIT_BYTES = 56 * 1024 * 1024
TOKEN_TILE = 512
CHUNK = 64
SAMPLE_PAD_T = 8


def _resident(shape):
    nd = len(shape)
    return pl.BlockSpec(shape, lambda *_: (0,) * nd, pipeline_mode=pl.Buffered(1))


def _layer_norm(x, g, b):
    mu = jnp.mean(x, axis=-1, keepdims=True)
    d = x - mu
    var = jnp.mean(d * d, axis=-1, keepdims=True)
    return d * lax.rsqrt(var + LN_EPS) * g + b


def _swiglu(xb, wg_ref, wu_ref, wd_ref):
    g = jnp.dot(xb, wg_ref[...], preferred_element_type=F32)
    u = jnp.dot(xb, wu_ref[...], preferred_element_type=F32)
    h = (jax.nn.silu(g) * u).astype(BF16)
    return jnp.dot(h, wd_ref[...], preferred_element_type=F32)


def _ffn_ln_kernel(x_ref, wg_ref, wu_ref, wd_ref, g_ref, b_ref, o_ref, *, alpha):
    x = x_ref[...]
    y = alpha * x + 0.5 * _swiglu(x.astype(BF16), wg_ref, wu_ref, wd_ref)
    o_ref[...] = _layer_norm(y, g_ref[...], b_ref[...])


def _ffn_ln(x, wg, wu, wd, g, b, alpha):
    n, d = x.shape
    dff = wg.shape[1]
    tm = TOKEN_TILE
    assert n % tm == 0
    row = pl.BlockSpec((tm, d), lambda i: (i, 0))
    return pl.pallas_call(
        functools.partial(_ffn_ln_kernel, alpha=alpha),
        grid=(n // tm,),
        in_specs=[row, _resident((d, dff)), _resident((d, dff)), _resident((dff, d)),
                  _resident((1, d)), _resident((1, d))],
        out_specs=row,
        out_shape=jax.ShapeDtypeStruct((n, d), F32),
        compiler_params=pltpu.CompilerParams(
            dimension_semantics=("arbitrary",), vmem_limit_bytes=VMEM_LIMIT_BYTES),
        name="ffn_ln",
    )(x, wg, wu, wd, g, b)


def _shift_rows(x, k):
    return pltpu.roll(x, k, axis=0)


def _premix_kernel(*refs, seq_len, n_real, tiles_per_seq, d_rwkv, d_conv, full_out):
    (x_ref, win_ref, mu_ref, w0_ref, wup_ref, a0_ref, aup_ref, gup_ref, kk_ref, ka_ref,
     cw_ref, seg_ref) = refs[:12]
    refs = refs[12:]
    if tiles_per_seq == 0:
        sh0_ref, cv1_ref, cv2_ref = refs[:3]
        refs = refs[3:]
    (r_ref, nkk_ref, b_ref, k_ref, v_ref, lw_ref, g_ref, yc_ref) = refs[:8]
    refs = refs[8:]
    if full_out:
        ps_out_ref, u_out_ref = refs[:2]
        refs = refs[2:]
    else:
        shift_ref, tail_ref = refs[:2]
        refs = refs[2:]
    if tiles_per_seq:
        ps_carry, u_carry = refs

    p_shift = 3 * d_rwkv + LORA_W + LORA_A + LORA_G
    tm = x_ref.shape[0]
    proj = jnp.dot(x_ref[...].astype(BF16), win_ref[...], preferred_element_type=F32)
    ps = proj[:, :p_shift]
    c_b = proj[:, p_shift:p_shift + d_conv]
    u = proj[:, p_shift + d_conv:p_shift + 2 * d_conv] * proj[:, p_shift + 2 * d_conv:]

    row = lax.broadcasted_iota(jnp.int32, (tm, 1), 0)
    if tiles_per_seq:
        @pl.when(pl.program_id(0) % tiles_per_seq == 0)
        def _():
            ps_carry[...] = jnp.zeros_like(ps_carry)
            u_carry[...] = jnp.zeros_like(u_carry)

        prev = jnp.where(row == 0, ps_carry[...], _shift_rows(ps, 1))
        um1 = jnp.where(row == 0, u_carry[1:2, :], _shift_rows(u, 1))
        um2 = jnp.where(row == 0, u_carry[0:1, :],
                        jnp.where(row == 1, u_carry[1:2, :], _shift_rows(u, 2)))
        ps_carry[...] = ps[tm - 1:tm, :]
        u_carry[...] = u[tm - 2:tm, :]
        t_in = None
    else:
        t_in = row % seq_len
        prev = jnp.where(t_in == 0, sh0_ref[...], _shift_rows(ps, 1))
        um1 = jnp.where(t_in == 0, cv1_ref[...], _shift_rows(u, 1))
        um2 = jnp.where(t_in <= 1, cv2_ref[...], _shift_rows(u, 2))

    mix = ps + mu_ref[...] * (prev - ps)
    r = mix[:, 0:d_rwkv]
    k = mix[:, d_rwkv:2 * d_rwkv]
    v = mix[:, 2 * d_rwkv:3 * d_rwkv]
    o = 3 * d_rwkv
    wd = mix[:, o:o + LORA_W]
    ad = mix[:, o + LORA_W:o + LORA_W + LORA_A]
    gd = mix[:, o + LORA_W + LORA_A:p_shift]

    w_log = -jax.nn.softplus(-(w0_ref[...] + jnp.dot(
        jnp.tanh(wd).astype(BF16), wup_ref[...], preferred_element_type=F32))) - 0.5
    log_decay = -jnp.exp(w_log)
    a = jax.nn.sigmoid(a0_ref[...] + jnp.dot(
        ad.astype(BF16), aup_ref[...], preferred_element_type=F32))
    g = jnp.dot(jax.nn.sigmoid(gd).astype(BF16), gup_ref[...], preferred_element_type=F32)

    kk = k * kk_ref[...]
    ss = jnp.dot((kk * kk).astype(BF16), seg_ref[...], preferred_element_type=F32)
    kk = kk / jnp.maximum(jnp.sqrt(ss), 1e-12)
    k = k * (1.0 + (a - 1.0) * ka_ref[...])
    nkk = -kk
    b = kk * a

    if n_real != seq_len:
        live = t_in < n_real
        zero = jnp.zeros_like(r)
        nkk = jnp.where(live, nkk, zero)
        b = jnp.where(live, b, zero)
        k = jnp.where(live, k, zero)
        v = jnp.where(live, v, zero)
        log_decay = jnp.where(live, log_decay, zero)

    r_ref[...] = r
    nkk_ref[...] = nkk
    b_ref[...] = b
    k_ref[...] = k
    v_ref[...] = v
    lw_ref[...] = log_decay
    g_ref[...] = g

    cw = cw_ref[...]
    z = cw[0:1, :] * um2
    z = z + cw[1:2, :] * um1
    z = z + cw[2:3, :] * u
    yc_ref[...] = (c_b * z).astype(BF16)

    if full_out:
        ps_out_ref[...] = ps
        u_out_ref[...] = u
    else:
        shift_ref[0] = ps[tm - 1:tm, :]
        tail_ref[0] = u[tm - (CONV_W - 1):tm, :]


def _premix(x1, p, seg, *, seq_len, n_real, init=None):
    n, d = x1.shape
    tm = TOKEN_TILE
    d_rwkv = p["w0"].shape[1]
    d_conv = p["conv_w"].shape[1]
    p_shift = 3 * d_rwkv + LORA_W + LORA_A + LORA_G
    p_total = p_shift + 3 * d_conv
    n_seq = n // seq_len
    if init is None:
        assert seq_len % tm == 0
        tiles_per_seq = seq_len // tm
    else:
        assert tm % seq_len == 0
        tiles_per_seq = 0
    full_out = init is not None

    def row(c):
        return pl.BlockSpec((tm, c), lambda i: (i, 0))

    in_specs = [row(d), _resident((d, p_total)), _resident((1, p_shift)),
                _resident((1, d_rwkv)), _resident((LORA_W, d_rwkv)),
                _resident((1, d_rwkv)), _resident((LORA_A, d_rwkv)),
                _resident((LORA_G, d_rwkv)), _resident((1, d_rwkv)), _resident((1, d_rwkv)),
                _resident((CONV_W, d_conv)), _resident((d_rwkv, d_rwkv))]
    args = [x1, p["w_in"], p["mu_shift"], p["w0"], p["w_lora_up"], p["a0"], p["a_lora_up"],
            p["g_lora_up"], p["k_k"], p["k_a"], p["conv_w"], seg]
    scratch = []
    if init is not None:
        in_specs += [row(p_shift), row(d_conv), row(d_conv)]
        args += list(init)
    else:
        scratch = [pltpu.VMEM((1, p_shift), F32), pltpu.VMEM((CONV_W - 1, d_conv), F32)]

    out_specs = [row(d_rwkv)] * 7 + [row(d_conv)]
    out_shape = [jax.ShapeDtypeStruct((n, d_rwkv), F32)] * 7 + [
        jax.ShapeDtypeStruct((n, d_conv), BF16)]
    if full_out:
        out_specs += [row(p_shift), row(d_conv)]
        out_shape += [jax.ShapeDtypeStruct((n, p_shift), F32),
                      jax.ShapeDtypeStruct((n, d_conv), F32)]
    else:
        out_specs += [
            pl.BlockSpec((1, 1, p_shift), lambda i: (i // tiles_per_seq, 0, 0)),
            pl.BlockSpec((1, CONV_W - 1, d_conv), lambda i: (i // tiles_per_seq, 0, 0))]
        out_shape += [jax.ShapeDtypeStruct((n_seq, 1, p_shift), F32),
                      jax.ShapeDtypeStruct((n_seq, CONV_W - 1, d_conv), F32)]

    return pl.pallas_call(
        functools.partial(_premix_kernel, seq_len=seq_len, n_real=n_real,
                          tiles_per_seq=tiles_per_seq, d_rwkv=d_rwkv, d_conv=d_conv,
                          full_out=full_out),
        grid=(n // tm,),
        in_specs=in_specs,
        out_specs=out_specs,
        out_shape=out_shape,
        scratch_shapes=scratch,
        compiler_params=pltpu.CompilerParams(
            dimension_semantics=("arbitrary",), vmem_limit_bytes=VMEM_LIMIT_BYTES),
        name="premix",
    )(*args)


def _bdot(a, b):
    return jnp.dot(a.astype(BF16), b.astype(BF16), preferred_element_type=F32)


def _bdot_nt(a, b):
    return lax.dot_general(a.astype(BF16), b.astype(BF16), (((1,), (1,)), ((), ())),
                           preferred_element_type=F32)


def _bdot_tn(a, b):
    return lax.dot_general(a.astype(BF16), b.astype(BF16), (((0,), (0,)), ((), ())),
                           preferred_element_type=F32)


def _wkv_kernel(r_ref, nkk_ref, b_ref, k_ref, v_ref, lw_ref, g_ref, s0_ref,
                rk_ref, lng_ref, lnb_ref, y_ref, sout_ref, s_scr, *, seqs, chunk, n_heads):
    c = pl.program_id(1)
    n_pairs = n_heads // 2
    c2 = 2 * chunk
    lanes = 2 * HEAD

    @pl.when(c == 0)
    def _():
        for q in range(seqs):
            for p in range(n_pairs):
                s_scr[q, p] = jnp.concatenate([s0_ref[q, 2 * p], s0_ref[q, 2 * p + 1]], axis=1)

    ii = lax.broadcasted_iota(jnp.int32, (chunk, chunk), 0)
    jj = lax.broadcasted_iota(jnp.int32, (chunk, chunk), 1)
    tri = (ii >= jj).astype(BF16)
    n_doublings = chunk.bit_length() - 1

    ri = lax.broadcasted_iota(jnp.int32, (c2, c2), 0)
    ci = lax.broadcasted_iota(jnp.int32, (c2, c2), 1)
    same = (ri // chunk) == (ci // chunk)
    strict = same & (ri % chunk > ci % chunk)
    incl = same & (ri % chunk >= ci % chunk)
    row_head = lax.broadcasted_iota(jnp.int32, (c2, lanes), 0) // chunk
    lane_head = lax.broadcasted_iota(jnp.int32, (c2, lanes), 1) // HEAD
    own = row_head == lane_head
    lo = lane_head == 0
    g_own = (lax.broadcasted_iota(jnp.int32, (lanes, lanes), 0) // HEAD
             == lax.broadcasted_iota(jnp.int32, (lanes, lanes), 1) // HEAD)
    lo_c = lax.broadcasted_iota(jnp.int32, (chunk, lanes), 1) < HEAD

    def stack(x):
        return jnp.concatenate([x, x], axis=0)

    def bd(x):
        return jnp.where(own, stack(x), 0.0)

    def own_half(x, low):
        swapped = pltpu.roll(x, HEAD, axis=1)
        rows = jnp.concatenate([x, swapped] if low else [swapped, x], axis=0)
        return jnp.where(lo if low else ~lo, rows, 0.0)

    ar_bd, bk_st, a_lo, r_lo, v_hi, bkh_bd, gam, unit = [], [], [], [], [], [], [], []
    for q in range(seqs):
        lw = lw_ref[q]
        lw_hi = lw.astype(BF16)
        lw_r1 = lw - lw_hi.astype(F32)
        lw_mid = lw_r1.astype(BF16)
        lw_lo = (lw_r1 - lw_mid.astype(F32)).astype(BF16)
        cum = (jnp.dot(tri, lw_hi, preferred_element_type=F32)
               + jnp.dot(tri, lw_mid, preferred_element_type=F32)
               + jnp.dot(tri, lw_lo, preferred_element_type=F32))
        last = cum[chunk - 1:chunk, :]
        e_in = jnp.exp(cum)
        e_out = jnp.exp(-cum)
        e_tail = jnp.exp(last - cum)
        a_t = nkk_ref[q] * jnp.exp(cum - lw)
        r_t = r_ref[q] * e_in
        b_t = b_ref[q] * e_out
        k_t = k_ref[q] * e_out
        b_h = b_ref[q] * e_tail
        k_h = k_ref[q] * e_tail
        gamma = jnp.exp(last)
        v_all = v_ref[q]
        for p in range(n_pairs):
            ps = slice(p * lanes, (p + 1) * lanes)
            ar_bd.append(jnp.concatenate([bd(a_t[:, ps]), bd(r_t[:, ps])], axis=0))
            bk_st.append(jnp.concatenate([stack(b_t[:, ps]), stack(k_t[:, ps])], axis=0))
            a_lo.append(own_half(a_t[:, ps], True))
            r_lo.append(own_half(r_t[:, ps], True))
            v_hi.append(own_half(v_all[:, ps], False))
            bkh_bd.append(jnp.concatenate([bd(b_h[:, ps]), bd(k_h[:, ps])], axis=0))
            gam.append(gamma[:, ps])
            unit.append((q, p, ps))

    units = range(seqs * n_pairs)
    m = [_bdot_nt(ar_bd[u], bk_st[u]) for u in units]
    l_ab = [jnp.where(strict, m[u][:c2, :c2], 0.0) for u in units]
    l_ak = [jnp.where(strict, m[u][:c2, c2:], 0.0) for u in units]
    l_rbk = [jnp.concatenate([jnp.where(incl, m[u][c2:, :c2], 0.0),
                              jnp.where(incl, m[u][c2:, c2:], 0.0)], axis=1) for u in units]
    y = [a_lo[u] + _bdot(l_ak[u], v_hi[u]) for u in units]
    pw = l_ab
    for step in range(n_doublings):
        if step < n_doublings - 1:
            py = [_bdot(pw[u], jnp.concatenate([y[u], pw[u]], axis=1)) for u in units]
            pw = [py[u][:, lanes:] for u in units]
            y = [y[u] + py[u][:, :lanes] for u in units]
        else:
            y = [y[u] + _bdot(pw[u], y[u]) for u in units]
    yv = [jnp.concatenate([y[u], v_hi[u]], axis=0) for u in units]
    gh = [_bdot_tn(yv[u], bkh_bd[u]) for u in units]
    lyv = [_bdot(l_rbk[u], yv[u]) for u in units]
    q_lo = [jnp.where(lo, r_lo[u] + lyv[u], 0.0) for u in units]
    q_bd = [jnp.where(own, q_lo[u] + pltpu.roll(q_lo[u], HEAD, axis=1), 0.0) for u in units]
    s = [s_scr[unit[u][0], unit[u][1]] for u in units]
    zero_s = jnp.zeros((HEAD, lanes), F32)
    o = [_bdot_nt(q_bd[u], jnp.concatenate([zero_s, s[u]], axis=0)) for u in units]
    g_bd = [jnp.where(g_own, stack(gh[u][:HEAD]), 0.0) for u in units]
    sg = [_bdot(s[u], g_bd[u]) for u in units]
    for u in units:
        s_scr[unit[u][0], unit[u][1]] = s[u] * gam[u] + sg[u] + gh[u][HEAD:]

    o = [jnp.where(lo, 0.0, o[u] + lyv[u]) for u in units]
    mu = [jnp.sum(o[u], axis=-1, keepdims=True) * (1.0 / HEAD) for u in units]
    d = [jnp.where(lo, 0.0, o[u] - mu[u]) for u in units]
    var = [jnp.sum(d[u] * d[u], axis=-1, keepdims=True) * (1.0 / HEAD) for u in units]
    rk_prod = [r_ref[unit[u][0]][:, unit[u][2]] * k_ref[unit[u][0]][:, unit[u][2]]
               * rk_ref[:, unit[u][2]] for u in units]
    bonus_0 = [jnp.sum(jnp.where(lo_c, rk_prod[u], 0.0), axis=-1, keepdims=True) for u in units]
    bonus_1 = [jnp.sum(jnp.where(lo_c, 0.0, rk_prod[u]), axis=-1, keepdims=True) for u in units]
    for q in range(seqs):
        y_pairs = []
        for p in range(n_pairs):
            u = q * n_pairs + p
            ps = unit[u][2]
            gn = d[u] * lax.rsqrt(var[u] + GN_EPS)
            gn = jnp.where(lo_c, pltpu.roll(gn[:chunk], HEAD, axis=1), gn[chunk:])
            gn = gn * lng_ref[:, ps] + lnb_ref[:, ps]
            bonus = jnp.where(lo_c, bonus_0[u], bonus_1[u])
            y_pairs.append((gn + bonus * v_ref[q][:, ps]) * g_ref[q][:, ps])
        y_ref[q] = jnp.concatenate(y_pairs, axis=1).astype(BF16)

    @pl.when(c == pl.num_programs(1) - 1)
    def _():
        for q in range(seqs):
            for p in range(n_pairs):
                sp = s_scr[q, p]
                sout_ref[q, 2 * p] = sp[:, :HEAD]
                sout_ref[q, 2 * p + 1] = sp[:, HEAD:]


def _wkv(ops, s0, rk, lng, lnb, *, seq_len, chunk, seqs):
    n, d_rwkv = ops[0].shape
    n_seq, n_heads = s0.shape[0], s0.shape[1]
    assert seq_len % chunk == 0 and n_seq % seqs == 0
    n_chunks = seq_len // chunk
    ops = [o.reshape(n_seq, seq_len, d_rwkv) for o in ops]
    tok = pl.BlockSpec((seqs, chunk, d_rwkv), lambda q, c: (q, c, 0))
    state = pl.BlockSpec((seqs, n_heads, HEAD, HEAD), lambda q, c: (q, 0, 0, 0))
    y, s_new = pl.pallas_call(
        functools.partial(_wkv_kernel, seqs=seqs, chunk=chunk, n_heads=n_heads),
        grid=(n_seq // seqs, n_chunks),
        in_specs=[tok] * 7 + [state] + [pl.BlockSpec((1, d_rwkv), lambda q, c: (0, 0))] * 3,
        out_specs=[tok, state],
        out_shape=[jax.ShapeDtypeStruct((n_seq, seq_len, d_rwkv), BF16),
                   jax.ShapeDtypeStruct(s0.shape, F32)],
        scratch_shapes=[pltpu.VMEM((seqs, n_heads // 2, HEAD, 2 * HEAD), F32)],
        compiler_params=pltpu.CompilerParams(
            dimension_semantics=("arbitrary", "arbitrary"),
            vmem_limit_bytes=VMEM_LIMIT_BYTES),
        name="wkv",
    )(*ops, s0, rk, lng, lnb)
    return y.reshape(n, d_rwkv), s_new


def _out_ffn_kernel(x_ref, yr_ref, yc_ref, wo_r_ref, wo_c_ref, g2_ref, b2_ref,
                    wg_ref, wu_ref, wd_ref, g3_ref, b3_ref, o_ref, *, alpha):
    mix = jnp.dot(yr_ref[...], wo_r_ref[...], preferred_element_type=F32)
    mix = mix + jnp.dot(yc_ref[...], wo_c_ref[...], preferred_element_type=F32)
    x = _layer_norm(alpha * x_ref[...] + mix, g2_ref[...], b2_ref[...])
    y = alpha * x + 0.5 * _swiglu(x.astype(BF16), wg_ref, wu_ref, wd_ref)
    o_ref[...] = _layer_norm(y, g3_ref[...], b3_ref[...])


def _out_ffn(x1, y_rwkv, y_conv, wo_r, wo_c, g2, b2, wg, wu, wd, g3, b3, alpha):
    n, d = x1.shape
    dff = wg.shape[1]
    d_rwkv, d_conv = y_rwkv.shape[1], y_conv.shape[1]
    tm = TOKEN_TILE
    assert n % tm == 0

    def row(c):
        return pl.BlockSpec((tm, c), lambda i: (i, 0))

    return pl.pallas_call(
        functools.partial(_out_ffn_kernel, alpha=alpha),
        grid=(n // tm,),
        in_specs=[row(d), row(d_rwkv), row(d_conv), _resident((d_rwkv, d)), _resident((d_conv, d)),
                  _resident((1, d)), _resident((1, d)),
                  _resident((d, dff)), _resident((d, dff)), _resident((dff, d)),
                  _resident((1, d)), _resident((1, d))],
        out_specs=row(d),
        out_shape=jax.ShapeDtypeStruct((n, d), F32),
        compiler_params=pltpu.CompilerParams(
            dimension_semantics=("arbitrary",), vmem_limit_bytes=VMEM_LIMIT_BYTES),
        name="out_ffn",
    )(x1, y_rwkv, y_conv, wo_r, wo_c, g2, b2, wg, wu, wd, g3, b3)


def _trunk_layer(x, p, seg, alpha, *, seq_len, n_real, chunk, seqs, wkv0, init):
    x1 = _ffn_ln(x, p["ffn1_wg"], p["ffn1_wu"], p["ffn1_wd"], p["ln1_g"], p["ln1_b"], alpha)
    pm = _premix(x1, p, seg, seq_len=seq_len, n_real=n_real, init=init)
    ops, y_conv, st_a, st_b = pm[:7], pm[7], pm[8], pm[9]
    y_rwkv, wkv_new = _wkv(ops, wkv0, p["r_k"], p["lnx_g"], p["lnx_b"],
                           seq_len=seq_len, chunk=chunk, seqs=seqs)
    d_rwkv = y_rwkv.shape[1]
    y = _out_ffn(x1, y_rwkv, y_conv, p["w_out"][:d_rwkv], p["w_out"][d_rwkv:],
                 p["ln2_g"], p["ln2_b"], p["ffn2_wg"], p["ffn2_wu"], p["ffn2_wd"],
                 p["ln3_g"], p["ln3_b"], alpha)
    return y, wkv_new, st_a, st_b


def _expand_rows(x, seq_len):
    n_seq, k, c = x.shape
    return jnp.pad(x, ((0, 0), (0, seq_len - k), (0, 0))).reshape(n_seq * seq_len, c)


def kernel(x_prompt, x_sample, state_wkv, state_shift, state_conv, ln1_g, ln1_b, ffn1_wg, ffn1_wu, ffn1_wd, w_in, mu_shift, w0, w_lora_up, a0, a_lora_up, g_lora_up, k_k, k_a, r_k, lnx_g, lnx_b, conv_w, w_out, ln2_g, ln2_b, ffn2_wg, ffn2_wu, ffn2_wd, ln3_g, ln3_b):
    depth = ln1_g.shape[0]
    bp, tp, d = x_prompt.shape
    bs, ts, _ = x_sample.shape
    n_heads = state_wkv.shape[2]
    d_rwkv = n_heads * HEAD
    alpha = (2.0 * depth) ** 0.25
    assert ts <= SAMPLE_PAD_T

    head_id = jnp.arange(d_rwkv) // HEAD
    seg = (head_id[:, None] == head_id[None, :]).astype(BF16)

    xp = x_prompt.reshape(bp * tp, d)
    xs = jnp.pad(x_sample, ((0, 0), (0, SAMPLE_PAD_T - ts), (0, 0))).reshape(bs * SAMPLE_PAD_T, d)

    outs = {k: [] for k in ("wkv_p", "shift_p", "conv_p", "wkv_s", "shift_s", "conv_s")}
    for l in range(depth):
        vec = lambda a: a[l].reshape(1, -1).astype(F32)
        p = {
            "ln1_g": vec(ln1_g), "ln1_b": vec(ln1_b),
            "ffn1_wg": ffn1_wg[l].astype(BF16), "ffn1_wu": ffn1_wu[l].astype(BF16),
            "ffn1_wd": ffn1_wd[l].astype(BF16),
            "w_in": w_in[l].astype(BF16), "mu_shift": vec(mu_shift), "w0": vec(w0),
            "w_lora_up": w_lora_up[l].astype(BF16), "a0": vec(a0),
            "a_lora_up": a_lora_up[l].astype(BF16), "g_lora_up": g_lora_up[l].astype(BF16),
            "k_k": vec(k_k), "k_a": vec(k_a), "r_k": vec(r_k),
            "lnx_g": vec(lnx_g), "lnx_b": vec(lnx_b),
            "conv_w": conv_w[l].astype(F32), "w_out": w_out[l].astype(BF16),
            "ln2_g": vec(ln2_g), "ln2_b": vec(ln2_b),
            "ffn2_wg": ffn2_wg[l].astype(BF16), "ffn2_wu": ffn2_wu[l].astype(BF16),
            "ffn2_wd": ffn2_wd[l].astype(BF16),
            "ln3_g": vec(ln3_g), "ln3_b": vec(ln3_b),
        }
        wkv0_p = jnp.zeros((bp, n_heads, HEAD, HEAD), F32)
        xp, wp, sp, cp = _trunk_layer(xp, p, seg, alpha, seq_len=tp, n_real=tp, chunk=CHUNK,
                                      seqs=8, wkv0=wkv0_p, init=None)
        conv0 = state_conv[l].astype(F32)
        init = (_expand_rows(state_shift[l].astype(F32)[:, None, :], SAMPLE_PAD_T),
                _expand_rows(conv0[:, 1:2, :], SAMPLE_PAD_T),
                _expand_rows(conv0, SAMPLE_PAD_T))
        xs, wsm, ps_s, u_s = _trunk_layer(xs, p, seg, alpha, seq_len=SAMPLE_PAD_T, n_real=ts,
                                          chunk=SAMPLE_PAD_T, seqs=8,
                                          wkv0=state_wkv[l].astype(F32), init=init)
        outs["wkv_p"].append(wp)
        outs["shift_p"].append(sp[:, 0, :])
        outs["conv_p"].append(cp)
        outs["wkv_s"].append(wsm)
        outs["shift_s"].append(ps_s.reshape(bs, SAMPLE_PAD_T, -1)[:, ts - 1, :])
        outs["conv_s"].append(u_s.reshape(bs, SAMPLE_PAD_T, -1)[:, ts - (CONV_W - 1):ts, :])

    y_prompt = xp.reshape(bp, tp, d)
    y_sample = xs.reshape(bs, SAMPLE_PAD_T, d)[:, :ts, :]
    return (y_prompt, y_sample,
            jnp.stack(outs["wkv_p"]), jnp.stack(outs["shift_p"]), jnp.stack(outs["conv_p"]),
            jnp.stack(outs["wkv_s"]), jnp.stack(outs["shift_s"]), jnp.stack(outs["conv_s"]))
```

```python
import functools
import math

import jax
import jax.numpy as jnp
from jax import lax
from jax.experimental import pallas as pl
from jax.experimental.pallas import tpu as pltpu

F32 = jnp.float32
BF16 = jnp.bfloat16

HEAD = 64
LORA_W = 64
LORA_A = 64
LORA_G = 128
CONV_W = 3
LN_EPS = 1e-5
GN_EPS = 1e-5 * HEAD

VMEM_LIMIT_BYTES = 56 * 1024 * 1024
TOKEN_TILE = 512
CHUNK = 64
SHORT_CHUNK = 8


def _resident(shape):
    nd = len(shape)
    return pl.BlockSpec(shape, lambda *_: (0,) * nd, pipeline_mode=pl.Buffered(1))


def _layer_norm(x, g, b):
    mu = jnp.mean(x, axis=-1, keepdims=True)
    d = x - mu
    var = jnp.mean(d * d, axis=-1, keepdims=True)
    return d * lax.rsqrt(var + LN_EPS) * g + b


def _swiglu(xb, wg_ref, wu_ref, wd_ref):
    g = jnp.dot(xb, wg_ref[...], preferred_element_type=F32)
    u = jnp.dot(xb, wu_ref[...], preferred_element_type=F32)
    h = (jax.nn.silu(g) * u).astype(BF16)
    return jnp.dot(h, wd_ref[...], preferred_element_type=F32)


def _ffn_ln_kernel(x_ref, wg_ref, wu_ref, wd_ref, g_ref, b_ref, o_ref, *, alpha):
    x = x_ref[...]
    y = alpha * x + 0.5 * _swiglu(x.astype(BF16), wg_ref, wu_ref, wd_ref)
    o_ref[...] = _layer_norm(y, g_ref[...], b_ref[...])


def _ffn_ln(x, wg, wu, wd, g, b, alpha):
    n, d = x.shape
    dff = wg.shape[1]
    tm = TOKEN_TILE
    assert n % tm == 0
    row = pl.BlockSpec((tm, d), lambda i: (i, 0))
    return pl.pallas_call(
        functools.partial(_ffn_ln_kernel, alpha=alpha),
        grid=(n // tm,),
        in_specs=[row, _resident((d, dff)), _resident((d, dff)), _resident((dff, d)),
                  _resident((1, d)), _resident((1, d))],
        out_specs=row,
        out_shape=jax.ShapeDtypeStruct((n, d), F32),
        compiler_params=pltpu.CompilerParams(
            dimension_semantics=("arbitrary",), vmem_limit_bytes=VMEM_LIMIT_BYTES),
        name="ffn_ln",
    )(x, wg, wu, wd, g, b)


HDR = 8


def _shift_rows(x, head, k):
    rolled = pltpu.roll(x, k, axis=0)
    row8 = lax.broadcasted_iota(jnp.int32, (HDR, 1), 0)
    top = jnp.where(row8 < k, pltpu.roll(head, k, axis=0), rolled[:HDR])
    return jnp.concatenate([top, rolled[HDR:]], axis=0)


class _PremixRefs:
    def __init__(self, refs, tiles_per_seq, full_out):
        refs = list(refs)
        (self.x, self.win, self.mu, self.w0, self.wup, self.a0, self.aup, self.gup, self.kk,
         self.ka, self.cw, self.seg) = refs[:12]
        refs = refs[12:]
        if tiles_per_seq == 0:
            self.sh0, self.cv1, self.cv2 = refs[:3]
            refs = refs[3:]
        (self.r, self.nkk, self.b, self.k, self.v, self.lw, self.g, self.yc) = refs[:8]
        refs = refs[8:]
        if full_out:
            self.ps_out, self.u_out = refs[:2]
        else:
            self.shift, self.tail = refs[:2]
        self.bufs = refs[2:]


def _premix_gates(rf, buf, first, *, seq_len, tiles_per_seq, d_rwkv, d_conv):
    p_shift = 3 * d_rwkv + LORA_W + LORA_A + LORA_G
    tm = rf.x.shape[0]
    ps = buf[HDR:, :p_shift]
    c_b = buf[HDR:, p_shift:p_shift + d_conv]
    u = buf[HDR:, p_shift + d_conv:p_shift + 2 * d_conv] * buf[HDR:, p_shift + 2 * d_conv:]

    if tiles_per_seq:
        head = jnp.where(first, 0.0, buf[0:HDR, :])
        head_u = head[:, p_shift + d_conv:p_shift + 2 * d_conv] * head[:, p_shift + 2 * d_conv:]
        prev = _shift_rows(ps, head[:, :p_shift], 1)
        um1 = _shift_rows(u, head_u, 1)
        um2 = _shift_rows(u, head_u, 2)
    else:
        t_in = lax.broadcasted_iota(jnp.int32, (tm, 1), 0) % seq_len
        prev = jnp.where(t_in == 0, rf.sh0[...], pltpu.roll(ps, 1, axis=0))
        um1 = jnp.where(t_in == 0, rf.cv1[...], pltpu.roll(u, 1, axis=0))
        um2 = jnp.where(t_in <= 1, rf.cv2[...], pltpu.roll(u, 2, axis=0))

    cw = rf.cw[...]
    z = cw[0:1, :] * um2
    z = z + cw[1:2, :] * um1
    z = z + cw[2:3, :] * u
    rf.yc[...] = (c_b * z).astype(BF16)
    if tiles_per_seq:
        rf.shift[0] = ps[tm - 1:tm, :]
        rf.tail[0] = u[tm - (CONV_W - 1):tm, :]
    else:
        rf.ps_out[...] = ps
        rf.u_out[...] = u

    mix = ps + rf.mu[...] * (prev - ps)
    r = mix[:, 0:d_rwkv]
    k = mix[:, d_rwkv:2 * d_rwkv]
    v = mix[:, 2 * d_rwkv:3 * d_rwkv]
    o = 3 * d_rwkv
    wd = mix[:, o:o + LORA_W]
    ad = mix[:, o + LORA_W:o + LORA_W + LORA_A]
    gd = mix[:, o + LORA_W + LORA_A:p_shift]

    rf.r[...] = r
    kk = k * rf.kk[...]
    dot_in = (jnp.tanh(wd).astype(BF16), ad.astype(BF16), jax.nn.sigmoid(gd).astype(BF16),
              (kk * kk).astype(BF16))
    return k, v, kk, dot_in


def _premix_dots(rf, dot_in):
    tanh_wd, ad, sig_gd, kk_sq = dot_in
    w_pre = jnp.dot(tanh_wd, rf.wup[...], preferred_element_type=F32)
    a_pre = jnp.dot(ad, rf.aup[...], preferred_element_type=F32)
    rf.g[...] = jnp.dot(sig_gd, rf.gup[...], preferred_element_type=F32)
    ss = jnp.dot(kk_sq, rf.seg[...], preferred_element_type=F32)
    return w_pre, a_pre, ss


def _premix_finish(rf, gates, dots):
    k, v, kk, _ = gates
    w_pre, a_pre, ss = dots
    log_decay = -math.exp(-0.5) * jax.nn.sigmoid(rf.w0[...] + w_pre)
    a = jax.nn.sigmoid(rf.a0[...] + a_pre)
    kk = kk * lax.rsqrt(jnp.maximum(ss, 1e-24))
    k = k * (1.0 + (a - 1.0) * rf.ka[...])
    nkk = -kk
    b = kk * a
    rf.nkk[...] = nkk
    rf.b[...] = b
    rf.k[...] = k
    rf.v[...] = v
    rf.lw[...] = log_decay


def _premix_kernel(*refs, seq_len, tiles_per_seq, d_rwkv, d_conv):
    rf = _PremixRefs(refs, tiles_per_seq, full_out=tiles_per_seq == 0)
    buf, = rf.bufs
    tm = rf.x.shape[0]
    i = pl.program_id(0)
    cfg = dict(seq_len=seq_len, tiles_per_seq=tiles_per_seq, d_rwkv=d_rwkv, d_conv=d_conv)

    if tiles_per_seq:
        @pl.when(i == 0)
        def _():
            buf[0:HDR, :] = jnp.zeros((HDR, buf.shape[1]), F32)

    buf[HDR:, :] = jnp.dot(rf.x[...].astype(BF16), rf.win[...], preferred_element_type=F32)
    first = (i % tiles_per_seq == 0) if tiles_per_seq else None
    gates = _premix_gates(rf, buf, first, **cfg)
    _premix_finish(rf, gates, _premix_dots(rf, gates[3]))
    if tiles_per_seq:
        buf[0:HDR, :] = buf[tm:tm + HDR, :]


def _premix(x1, p, seg, *, seq_len, init=None):
    n, d = x1.shape
    tm = TOKEN_TILE
    d_rwkv = p["w0"].shape[1]
    d_conv = p["conv_w"].shape[1]
    p_shift = 3 * d_rwkv + LORA_W + LORA_A + LORA_G
    p_total = p_shift + 3 * d_conv
    n_seq = n // seq_len
    if init is None:
        assert seq_len % tm == 0
        tiles_per_seq = seq_len // tm
    else:
        assert tm % seq_len == 0
        tiles_per_seq = 0
    full_out = init is not None

    def row(c):
        return pl.BlockSpec((tm, c), lambda i: (i, 0))

    in_specs = [row(d), _resident((d, p_total)), _resident((1, p_shift)),
                _resident((1, d_rwkv)), _resident((LORA_W, d_rwkv)),
                _resident((1, d_rwkv)), _resident((LORA_A, d_rwkv)),
                _resident((LORA_G, d_rwkv)), _resident((1, d_rwkv)), _resident((1, d_rwkv)),
                _resident((CONV_W, d_conv)), _resident((d_rwkv, d_rwkv))]
    args = [x1, p["w_in"], p["mu_shift"], p["w0"], p["w_lora_up"], p["a0"], p["a_lora_up"],
            p["g_lora_up"], p["k_k"], p["k_a"], p["conv_w"], seg]
    if init is not None:
        in_specs += [row(p_shift), row(d_conv), row(d_conv)]
        args += list(init)

    out_specs = [row(d_rwkv)] * 7 + [row(d_conv)]
    out_shape = [jax.ShapeDtypeStruct((n, d_rwkv), F32)] * 7 + [
        jax.ShapeDtypeStruct((n, d_conv), BF16)]
    if full_out:
        out_specs += [row(p_shift), row(d_conv)]
        out_shape += [jax.ShapeDtypeStruct((n, p_shift), F32),
                      jax.ShapeDtypeStruct((n, d_conv), F32)]
    else:
        out_specs += [
            pl.BlockSpec((1, 1, p_shift), lambda i: (i // tiles_per_seq, 0, 0)),
            pl.BlockSpec((1, CONV_W - 1, d_conv), lambda i: (i // tiles_per_seq, 0, 0))]
        out_shape += [jax.ShapeDtypeStruct((n_seq, 1, p_shift), F32),
                      jax.ShapeDtypeStruct((n_seq, CONV_W - 1, d_conv), F32)]

    return pl.pallas_call(
        functools.partial(_premix_kernel, seq_len=seq_len,
                          tiles_per_seq=tiles_per_seq, d_rwkv=d_rwkv, d_conv=d_conv),
        grid=(n // tm,),
        in_specs=in_specs,
        out_specs=out_specs,
        out_shape=out_shape,
        scratch_shapes=[pltpu.VMEM((HDR + tm, p_total), F32)],
        compiler_params=pltpu.CompilerParams(
            dimension_semantics=("arbitrary",), vmem_limit_bytes=VMEM_LIMIT_BYTES),
        name="premix",
    )(*args)


def _bdot(a, b):
    return jnp.dot(a.astype(BF16), b.astype(BF16), preferred_element_type=F32)


def _bdot_nt(a, b):
    return lax.dot_general(a.astype(BF16), b.astype(BF16), (((1,), (1,)), ((), ())),
                           preferred_element_type=F32)


def _bdot_tn(a, b):
    return lax.dot_general(a.astype(BF16), b.astype(BF16), (((0,), (0,)), ((), ())),
                           preferred_element_type=F32)


def _wkv_kernel(r_ref, nkk_ref, b_ref, k_ref, v_ref, lw_ref, g_ref, s0_ref,
                rk_ref, lng_ref, lnb_ref, y_ref, sout_ref, s_scr, *, seqs, chunk, n_heads):
    c = pl.program_id(1)
    n_pairs = n_heads // 2
    c2 = 2 * chunk
    lanes = 2 * HEAD

    @pl.when(c == 0)
    def _():
        for q in range(seqs):
            for p in range(n_pairs):
                s_scr[q, p] = jnp.concatenate([s0_ref[q, 2 * p], s0_ref[q, 2 * p + 1]], axis=1)

    ii = lax.broadcasted_iota(jnp.int32, (chunk, chunk), 0)
    jj = lax.broadcasted_iota(jnp.int32, (chunk, chunk), 1)
    tri = (ii >= jj).astype(BF16)
    n_doublings = chunk.bit_length() - 1

    ri = lax.broadcasted_iota(jnp.int32, (c2, c2), 0)
    ci = lax.broadcasted_iota(jnp.int32, (c2, c2), 1)
    same = (ri // chunk) == (ci // chunk)
    strict = same & (ri % chunk > ci % chunk)
    incl = same & (ri % chunk >= ci % chunk)
    row_head = lax.broadcasted_iota(jnp.int32, (c2, lanes), 0) // chunk
    lane_head = lax.broadcasted_iota(jnp.int32, (c2, lanes), 1) // HEAD
    own = row_head == lane_head
    lo = lane_head == 0
    g_own = (lax.broadcasted_iota(jnp.int32, (lanes, lanes), 0) // HEAD
             == lax.broadcasted_iota(jnp.int32, (lanes, lanes), 1) // HEAD)
    lo_c = lax.broadcasted_iota(jnp.int32, (chunk, lanes), 1) < HEAD

    def stack(x):
        return jnp.concatenate([x, x], axis=0)

    def bd(x):
        return jnp.where(own, stack(x), 0.0)

    def own_half(x, low):
        swapped = pltpu.roll(x, HEAD, axis=1)
        rows = jnp.concatenate([x, swapped] if low else [swapped, x], axis=0)
        return jnp.where(lo if low else ~lo, rows, 0.0)

    t_in = lw_ref.shape[1]

    def tokens(ref, q):
        x = ref[q]
        if t_in == chunk:
            return x
        return jnp.concatenate([x, jnp.zeros((chunk - t_in, x.shape[1]), x.dtype)], axis=0)

    ar_bd, bk_st, a_lo, r_lo, v_hi, bkh_bd, gam, unit = [], [], [], [], [], [], [], []
    for q in range(seqs):
        lw = tokens(lw_ref, q)
        lw_hi = lw.astype(BF16)
        lw_r1 = lw - lw_hi.astype(F32)
        lw_mid = lw_r1.astype(BF16)
        lw_lo = (lw_r1 - lw_mid.astype(F32)).astype(BF16)
        cum = (jnp.dot(tri, lw_hi, preferred_element_type=F32)
               + jnp.dot(tri, lw_mid, preferred_element_type=F32)
               + jnp.dot(tri, lw_lo, preferred_element_type=F32))
        last = cum[chunk - 1:chunk, :]
        e_in = jnp.exp(cum)
        e_out = jnp.exp(-cum)
        e_tail = jnp.exp(last - cum)
        b_all = tokens(b_ref, q)
        k_all = tokens(k_ref, q)
        a_t = tokens(nkk_ref, q) * jnp.exp(cum - lw)
        r_t = tokens(r_ref, q) * e_in
        b_t = b_all * e_out
        k_t = k_all * e_out
        b_h = b_all * e_tail
        k_h = k_all * e_tail
        gamma = jnp.exp(last)
        v_all = tokens(v_ref, q)
        for p in range(n_pairs):
            ps = slice(p * lanes, (p + 1) * lanes)
            ar_bd.append(jnp.concatenate([bd(a_t[:, ps]), bd(r_t[:, ps])], axis=0))
            bk_st.append(jnp.concatenate([stack(b_t[:, ps]), stack(k_t[:, ps])], axis=0))
            a_lo.append(own_half(a_t[:, ps], True))
            r_lo.append(own_half(r_t[:, ps], True))
            v_hi.append(own_half(v_all[:, ps], False))
            bkh_bd.append(jnp.concatenate([bd(b_h[:, ps]), bd(k_h[:, ps])], axis=0))
            gam.append(gamma[:, ps])
            unit.append((q, p, ps))

    units = range(seqs * n_pairs)
    m = [_bdot_nt(ar_bd[u], bk_st[u]) for u in units]
    l_ab = [jnp.where(strict, m[u][:c2, :c2], 0.0) for u in units]
    l_ak = [jnp.where(strict, m[u][:c2, c2:], 0.0) for u in units]
    l_rbk = [jnp.concatenate([jnp.where(incl, m[u][c2:, :c2], 0.0),
                              jnp.where(incl, m[u][c2:, c2:], 0.0)], axis=1) for u in units]
    y = [a_lo[u] + _bdot(l_ak[u], v_hi[u]) for u in units]
    pw = l_ab
    for step in range(n_doublings):
        if step < n_doublings - 1:
            py = [_bdot(pw[u], jnp.concatenate([y[u], pw[u]], axis=1)) for u in units]
            pw = [py[u][:, lanes:] for u in units]
            y = [y[u] + py[u][:, :lanes] for u in units]
        else:
            y = [y[u] + _bdot(pw[u], y[u]) for u in units]
    yv = [jnp.concatenate([y[u], v_hi[u]], axis=0) for u in units]
    gh = [_bdot_tn(yv[u], bkh_bd[u]) for u in units]
    lyv = [_bdot(l_rbk[u], yv[u]) for u in units]
    q_lo = [jnp.where(lo, r_lo[u] + lyv[u], 0.0) for u in units]
    q_bd = [jnp.where(own, q_lo[u] + pltpu.roll(q_lo[u], HEAD, axis=1), 0.0) for u in units]
    s = [s_scr[unit[u][0], unit[u][1]] for u in units]
    zero_s = jnp.zeros((HEAD, lanes), F32)
    o = [_bdot_nt(q_bd[u], jnp.concatenate([zero_s, s[u]], axis=0)) for u in units]
    g_bd = [jnp.where(g_own, stack(gh[u][:HEAD]), 0.0) for u in units]
    sg = [_bdot(s[u], g_bd[u]) for u in units]
    for u in units:
        s_scr[unit[u][0], unit[u][1]] = s[u] * gam[u] + sg[u] + gh[u][HEAD:]

    o = [jnp.where(lo, 0.0, o[u] + lyv[u]) for u in units]
    mu = [jnp.sum(o[u], axis=-1, keepdims=True) * (1.0 / HEAD) for u in units]
    d = [jnp.where(lo, 0.0, o[u] - mu[u]) for u in units]
    var = [jnp.sum(d[u] * d[u], axis=-1, keepdims=True) * (1.0 / HEAD) for u in units]
    rk_prod = [r_ref[unit[u][0]][:, unit[u][2]] * k_ref[unit[u][0]][:, unit[u][2]]
               * rk_ref[:, unit[u][2]] for u in units]
    lo_t = lo_c[:t_in]
    bonus_0 = [jnp.sum(jnp.where(lo_t, rk_prod[u], 0.0), axis=-1, keepdims=True) for u in units]
    bonus_1 = [jnp.sum(jnp.where(lo_t, 0.0, rk_prod[u]), axis=-1, keepdims=True) for u in units]
    for q in range(seqs):
        y_pairs = []
        for p in range(n_pairs):
            u = q * n_pairs + p
            ps = unit[u][2]
            gn = d[u] * lax.rsqrt(var[u] + GN_EPS)
            gn = jnp.where(lo_c, pltpu.roll(gn[:chunk], HEAD, axis=1), gn[chunk:])
            gn = gn[:t_in] * lng_ref[:, ps] + lnb_ref[:, ps]
            bonus = jnp.where(lo_t, bonus_0[u], bonus_1[u])
            y_pairs.append((gn + bonus * v_ref[q][:, ps]) * g_ref[q][:, ps])
        y_ref[q] = jnp.concatenate(y_pairs, axis=1).astype(y_ref.dtype)

    @pl.when(c == pl.num_programs(1) - 1)
    def _():
        for q in range(seqs):
            for p in range(n_pairs):
                sp = s_scr[q, p]
                sout_ref[q, 2 * p] = sp[:, :HEAD]
                sout_ref[q, 2 * p + 1] = sp[:, HEAD:]


def _wkv(ops, s0, rk, lng, lnb, *, seq_len, chunk, seqs):
    n, d_rwkv = ops[0].shape
    n_seq, n_heads = s0.shape[0], s0.shape[1]
    assert n_seq % seqs == 0
    if seq_len < chunk:
        n_chunks, rows, y_dtype = 1, seq_len, F32
    else:
        assert seq_len % chunk == 0
        n_chunks, rows, y_dtype = seq_len // chunk, chunk, BF16
    ops = [o.reshape(n_seq, seq_len, d_rwkv) for o in ops]
    tok = pl.BlockSpec((seqs, rows, d_rwkv), lambda q, c: (q, c, 0))
    state = pl.BlockSpec((seqs, n_heads, HEAD, HEAD), lambda q, c: (q, 0, 0, 0))
    y, s_new = pl.pallas_call(
        functools.partial(_wkv_kernel, seqs=seqs, chunk=chunk, n_heads=n_heads),
        grid=(n_seq // seqs, n_chunks),
        in_specs=[tok] * 7 + [state] + [pl.BlockSpec((1, d_rwkv), lambda q, c: (0, 0))] * 3,
        out_specs=[tok, state],
        out_shape=[jax.ShapeDtypeStruct((n_seq, seq_len, d_rwkv), y_dtype),
                   jax.ShapeDtypeStruct(s0.shape, F32)],
        scratch_shapes=[pltpu.VMEM((seqs, n_heads // 2, HEAD, 2 * HEAD), F32)],
        compiler_params=pltpu.CompilerParams(
            dimension_semantics=("arbitrary", "arbitrary"),
            vmem_limit_bytes=VMEM_LIMIT_BYTES),
        name="wkv",
    )(*ops, s0, rk, lng, lnb)
    return y.reshape(n, d_rwkv), s_new


def _out_ffn_kernel(x_ref, yr_ref, yc_ref, wo_r_ref, wo_c_ref, g2_ref, b2_ref,
                    wg_ref, wu_ref, wd_ref, g3_ref, b3_ref, o_ref, *, alpha):
    mix = jnp.dot(yr_ref[...].astype(BF16), wo_r_ref[...], preferred_element_type=F32)
    mix = mix + jnp.dot(yc_ref[...], wo_c_ref[...], preferred_element_type=F32)
    x = _layer_norm(alpha * x_ref[...] + mix, g2_ref[...], b2_ref[...])
    y = alpha * x + 0.5 * _swiglu(x.astype(BF16), wg_ref, wu_ref, wd_ref)
    o_ref[...] = _layer_norm(y, g3_ref[...], b3_ref[...])


def _out_ffn(x1, y_rwkv, y_conv, wo_r, wo_c, g2, b2, wg, wu, wd, g3, b3, alpha):
    n, d = x1.shape
    dff = wg.shape[1]
    d_rwkv, d_conv = y_rwkv.shape[1], y_conv.shape[1]
    tm = TOKEN_TILE
    assert n % tm == 0

    def row(c):
        return pl.BlockSpec((tm, c), lambda i: (i, 0))

    return pl.pallas_call(
        functools.partial(_out_ffn_kernel, alpha=alpha),
        grid=(n // tm,),
        in_specs=[row(d), row(d_rwkv), row(d_conv), _resident((d_rwkv, d)), _resident((d_conv, d)),
                  _resident((1, d)), _resident((1, d)),
                  _resident((d, dff)), _resident((d, dff)), _resident((dff, d)),
                  _resident((1, d)), _resident((1, d))],
        out_specs=row(d),
        out_shape=jax.ShapeDtypeStruct((n, d), F32),
        compiler_params=pltpu.CompilerParams(
            dimension_semantics=("arbitrary",), vmem_limit_bytes=VMEM_LIMIT_BYTES),
        name="out_ffn",
    )(x1, y_rwkv, y_conv, wo_r, wo_c, g2, b2, wg, wu, wd, g3, b3)


def _trunk_layer(x, p, seg, alpha, *, seq_len, chunk, seqs, wkv0, init):
    x1 = _ffn_ln(x, p["ffn1_wg"], p["ffn1_wu"], p["ffn1_wd"], p["ln1_g"], p["ln1_b"], alpha)
    pm = _premix(x1, p, seg, seq_len=seq_len, init=init)
    ops, y_conv, st_a, st_b = pm[:7], pm[7], pm[8], pm[9]
    y_rwkv, wkv_new = _wkv(ops, wkv0, p["r_k"], p["lnx_g"], p["lnx_b"],
                           seq_len=seq_len, chunk=chunk, seqs=seqs)
    d_rwkv = y_rwkv.shape[1]
    y = _out_ffn(x1, y_rwkv, y_conv, p["w_out"][:d_rwkv], p["w_out"][d_rwkv:],
                 p["ln2_g"], p["ln2_b"], p["ffn2_wg"], p["ffn2_wu"], p["ffn2_wd"],
                 p["ln3_g"], p["ln3_b"], alpha)
    return y, wkv_new, st_a, st_b


def _expand_rows(x, seq_len):
    n_seq, k, c = x.shape
    return jnp.pad(x, ((0, 0), (0, seq_len - k), (0, 0))).reshape(n_seq * seq_len, c)


def kernel(x_prompt, x_sample, state_wkv, state_shift, state_conv, ln1_g, ln1_b, ffn1_wg, ffn1_wu, ffn1_wd, w_in, mu_shift, w0, w_lora_up, a0, a_lora_up, g_lora_up, k_k, k_a, r_k, lnx_g, lnx_b, conv_w, w_out, ln2_g, ln2_b, ffn2_wg, ffn2_wu, ffn2_wd, ln3_g, ln3_b):
    depth = ln1_g.shape[0]
    bp, tp, d = x_prompt.shape
    bs, ts, _ = x_sample.shape
    n_heads = state_wkv.shape[2]
    d_rwkv = n_heads * HEAD
    alpha = (2.0 * depth) ** 0.25
    assert CONV_W - 1 <= ts <= SHORT_CHUNK

    head_id = jnp.arange(d_rwkv) // HEAD
    seg = (head_id[:, None] == head_id[None, :]).astype(BF16)

    xp = x_prompt.reshape(bp * tp, d)
    xs = x_sample.reshape(bs * ts, d)

    outs = {k: [] for k in ("wkv_p", "shift_p", "conv_p", "wkv_s", "shift_s", "conv_s")}
    for l in range(depth):
        vec = lambda a: a[l].reshape(1, -1).astype(F32)
        p = {
            "ln1_g": vec(ln1_g), "ln1_b": vec(ln1_b),
            "ffn1_wg": ffn1_wg[l].astype(BF16), "ffn1_wu": ffn1_wu[l].astype(BF16),
            "ffn1_wd": ffn1_wd[l].astype(BF16),
            "w_in": w_in[l].astype(BF16), "mu_shift": vec(mu_shift), "w0": vec(w0),
            "w_lora_up": w_lora_up[l].astype(BF16), "a0": vec(a0),
            "a_lora_up": a_lora_up[l].astype(BF16), "g_lora_up": g_lora_up[l].astype(BF16),
            "k_k": vec(k_k), "k_a": vec(k_a), "r_k": vec(r_k),
            "lnx_g": vec(lnx_g), "lnx_b": vec(lnx_b),
            "conv_w": conv_w[l].astype(F32), "w_out": w_out[l].astype(BF16),
            "ln2_g": vec(ln2_g), "ln2_b": vec(ln2_b),
            "ffn2_wg": ffn2_wg[l].astype(BF16), "ffn2_wu": ffn2_wu[l].astype(BF16),
            "ffn2_wd": ffn2_wd[l].astype(BF16),
            "ln3_g": vec(ln3_g), "ln3_b": vec(ln3_b),
        }
        wkv0_p = jnp.zeros((bp, n_heads, HEAD, HEAD), F32)
        xp, wp, sp, cp = _trunk_layer(xp, p, seg, alpha, seq_len=tp, chunk=CHUNK,
                                      seqs=bp, wkv0=wkv0_p, init=None)
        conv0 = state_conv[l].astype(F32)
        init = (_expand_rows(state_shift[l].astype(F32)[:, None, :], ts),
                _expand_rows(conv0[:, 1:2, :], ts),
                _expand_rows(conv0, ts))
        xs, wsm, ps_s, u_s = _trunk_layer(xs, p, seg, alpha, seq_len=ts, chunk=SHORT_CHUNK,
                                          seqs=8, wkv0=state_wkv[l].astype(F32), init=init)
        outs["wkv_p"].append(wp)
        outs["shift_p"].append(sp[:, 0, :])
        outs["conv_p"].append(cp)
        outs["wkv_s"].append(wsm)
        outs["shift_s"].append(ps_s.reshape(bs, ts, -1)[:, ts - 1, :])
        outs["conv_s"].append(u_s.reshape(bs, ts, -1)[:, ts - (CONV_W - 1):ts, :])

    y_prompt = xp.reshape(bp, tp, d)
    y_sample = xs.reshape(bs, ts, d)
    return (y_prompt, y_sample,
            jnp.stack(outs["wkv_p"]), jnp.stack(outs["shift_p"]), jnp.stack(outs["conv_p"]),
            jnp.stack(outs["wkv_s"]), jnp.stack(outs["shift_s"]), jnp.stack(outs["conv_s"]))
```

```python
import functools
import math

import jax
import jax.numpy as jnp
from jax import lax
from jax.experimental import pallas as pl
from jax.experimental.pallas import tpu as pltpu

F32 = jnp.float32
BF16 = jnp.bfloat16

HEAD = 64
LORA_W = 64
LORA_A = 64
LORA_G = 128
CONV_W = 3
LN_EPS = 1e-5
GN_EPS = 1e-5 * HEAD

VMEM_LIMIT_BYTES = 56 * 1024 * 1024
TOKEN_TILE = 512
CHUNK = 64
SHORT_CHUNK = 8


def _resident(shape):
    nd = len(shape)
    return pl.BlockSpec(shape, lambda *_: (0,) * nd, pipeline_mode=pl.Buffered(1))


def _layer_norm(x, g, b):
    mu = jnp.mean(x, axis=-1, keepdims=True)
    d = x - mu
    var = jnp.mean(d * d, axis=-1, keepdims=True)
    return d * lax.rsqrt(var + LN_EPS) * g + b


def _swiglu(xb, wg_ref, wu_ref, wd_ref):
    g = jnp.dot(xb, wg_ref[...], preferred_element_type=F32)
    u = jnp.dot(xb, wu_ref[...], preferred_element_type=F32)
    h = (jax.nn.silu(g) * u).astype(BF16)
    return jnp.dot(h, wd_ref[...], preferred_element_type=F32)


def _ffn_ln_kernel(x_ref, wg_ref, wu_ref, wd_ref, g_ref, b_ref, o_ref, *, alpha):
    x = x_ref[...]
    y = alpha * x + 0.5 * _swiglu(x.astype(BF16), wg_ref, wu_ref, wd_ref)
    o_ref[...] = _layer_norm(y, g_ref[...], b_ref[...])


def _ffn_ln(x, wg, wu, wd, g, b, alpha):
    n, d = x.shape
    dff = wg.shape[1]
    tm = TOKEN_TILE
    assert n % tm == 0
    row = pl.BlockSpec((tm, d), lambda i: (i, 0))
    return pl.pallas_call(
        functools.partial(_ffn_ln_kernel, alpha=alpha),
        grid=(n // tm,),
        in_specs=[row, _resident((d, dff)), _resident((d, dff)), _resident((dff, d)),
                  _resident((1, d)), _resident((1, d))],
        out_specs=row,
        out_shape=jax.ShapeDtypeStruct((n, d), F32),
        compiler_params=pltpu.CompilerParams(
            dimension_semantics=("arbitrary",), vmem_limit_bytes=VMEM_LIMIT_BYTES),
        name="ffn_ln",
    )(x, wg, wu, wd, g, b)


HDR = 8


def _shift_rows(x, head, k):
    rolled = pltpu.roll(x, k, axis=0)
    row8 = lax.broadcasted_iota(jnp.int32, (HDR, 1), 0)
    top = jnp.where(row8 < k, pltpu.roll(head, k, axis=0), rolled[:HDR])
    return jnp.concatenate([top, rolled[HDR:]], axis=0)


class _PremixRefs:
    def __init__(self, refs, tiles_per_seq, full_out):
        refs = list(refs)
        (self.x, self.win, self.mu, self.w0, self.wup, self.a0, self.aup, self.gup, self.kk,
         self.ka, self.cw, self.seg) = refs[:12]
        refs = refs[12:]
        if tiles_per_seq == 0:
            self.sh0, self.cv1, self.cv2 = refs[:3]
            refs = refs[3:]
        (self.r, self.nkk, self.b, self.k, self.v, self.lw, self.g, self.yc) = refs[:8]
        refs = refs[8:]
        if full_out:
            self.ps_out, self.u_out = refs[:2]
        else:
            self.shift, self.tail = refs[:2]
        self.bufs = refs[2:]


def _premix_gates(rf, buf, first, *, seq_len, tiles_per_seq, d_rwkv, d_conv):
    p_shift = 3 * d_rwkv + LORA_W + LORA_A + LORA_G
    tm = rf.x.shape[0]
    ps = buf[HDR:, :p_shift]
    c_b = buf[HDR:, p_shift:p_shift + d_conv]
    u = buf[HDR:, p_shift + d_conv:p_shift + 2 * d_conv] * buf[HDR:, p_shift + 2 * d_conv:]

    if tiles_per_seq:
        head = jnp.where(first, 0.0, buf[0:HDR, :])
        head_u = head[:, p_shift + d_conv:p_shift + 2 * d_conv] * head[:, p_shift + 2 * d_conv:]
        prev = _shift_rows(ps, head[:, :p_shift], 1)
        um1 = _shift_rows(u, head_u, 1)
        um2 = _shift_rows(u, head_u, 2)
    else:
        t_in = lax.broadcasted_iota(jnp.int32, (tm, 1), 0) % seq_len
        prev = jnp.where(t_in == 0, rf.sh0[...], pltpu.roll(ps, 1, axis=0))
        um1 = jnp.where(t_in == 0, rf.cv1[...], pltpu.roll(u, 1, axis=0))
        um2 = jnp.where(t_in <= 1, rf.cv2[...], pltpu.roll(u, 2, axis=0))

    cw = rf.cw[...]
    z = cw[0:1, :] * um2
    z = z + cw[1:2, :] * um1
    z = z + cw[2:3, :] * u
    rf.yc[...] = (c_b * z).astype(BF16)
    if tiles_per_seq:
        rf.shift[0] = ps[tm - 1:tm, :]
        rf.tail[0] = u[tm - (CONV_W - 1):tm, :]
    else:
        rf.ps_out[...] = ps
        rf.u_out[...] = u

    mix = ps + rf.mu[...] * (prev - ps)
    r = mix[:, 0:d_rwkv]
    k = mix[:, d_rwkv:2 * d_rwkv]
    v = mix[:, 2 * d_rwkv:3 * d_rwkv]
    o = 3 * d_rwkv
    wd = mix[:, o:o + LORA_W]
    ad = mix[:, o + LORA_W:o + LORA_W + LORA_A]
    gd = mix[:, o + LORA_W + LORA_A:p_shift]

    rf.r[...] = r
    kk = k * rf.kk[...]
    dot_in = (jnp.tanh(wd).astype(BF16), ad.astype(BF16), jax.nn.sigmoid(gd).astype(BF16),
              (kk * kk).astype(BF16))
    return k, v, kk, dot_in


def _premix_dots(rf, dot_in):
    tanh_wd, ad, sig_gd, kk_sq = dot_in
    w_pre = jnp.dot(tanh_wd, rf.wup[...], preferred_element_type=F32)
    a_pre = jnp.dot(ad, rf.aup[...], preferred_element_type=F32)
    rf.g[...] = jnp.dot(sig_gd, rf.gup[...], preferred_element_type=F32)
    ss = jnp.dot(kk_sq, rf.seg[...], preferred_element_type=F32)
    return w_pre, a_pre, ss


def _premix_finish(rf, gates, dots):
    k, v, kk, _ = gates
    w_pre, a_pre, ss = dots
    log_decay = -math.exp(-0.5) * jax.nn.sigmoid(rf.w0[...] + w_pre)
    a = jax.nn.sigmoid(rf.a0[...] + a_pre)
    kk = kk * lax.rsqrt(jnp.maximum(ss, 1e-24))
    k = k * (1.0 + (a - 1.0) * rf.ka[...])
    nkk = -kk
    b = kk * a
    rf.nkk[...] = nkk
    rf.b[...] = b
    rf.k[...] = k
    rf.v[...] = v
    rf.lw[...] = log_decay


def _premix_kernel(*refs, seq_len, tiles_per_seq, d_rwkv, d_conv):
    rf = _PremixRefs(refs, tiles_per_seq, full_out=tiles_per_seq == 0)
    buf, = rf.bufs
    tm = rf.x.shape[0]
    i = pl.program_id(0)
    cfg = dict(seq_len=seq_len, tiles_per_seq=tiles_per_seq, d_rwkv=d_rwkv, d_conv=d_conv)

    if tiles_per_seq:
        @pl.when(i == 0)
        def _():
            buf[0:HDR, :] = jnp.zeros((HDR, buf.shape[1]), F32)

    buf[HDR:, :] = jnp.dot(rf.x[...].astype(BF16), rf.win[...], preferred_element_type=F32)
    first = (i % tiles_per_seq == 0) if tiles_per_seq else None
    gates = _premix_gates(rf, buf, first, **cfg)
    _premix_finish(rf, gates, _premix_dots(rf, gates[3]))
    if tiles_per_seq:
        buf[0:HDR, :] = buf[tm:tm + HDR, :]


def _premix(x1, p, seg, *, seq_len, init=None):
    n, d = x1.shape
    tm = TOKEN_TILE
    d_rwkv = p["w0"].shape[1]
    d_conv = p["conv_w"].shape[1]
    p_shift = 3 * d_rwkv + LORA_W + LORA_A + LORA_G
    p_total = p_shift + 3 * d_conv
    n_seq = n // seq_len
    if init is None:
        assert seq_len % tm == 0
        tiles_per_seq = seq_len // tm
    else:
        assert tm % seq_len == 0
        tiles_per_seq = 0
    full_out = init is not None

    def row(c):
        return pl.BlockSpec((tm, c), lambda i: (i, 0))

    in_specs = [row(d), _resident((d, p_total)), _resident((1, p_shift)),
                _resident((1, d_rwkv)), _resident((LORA_W, d_rwkv)),
                _resident((1, d_rwkv)), _resident((LORA_A, d_rwkv)),
                _resident((LORA_G, d_rwkv)), _resident((1, d_rwkv)), _resident((1, d_rwkv)),
                _resident((CONV_W, d_conv)), _resident((d_rwkv, d_rwkv))]
    args = [x1, p["w_in"], p["mu_shift"], p["w0"], p["w_lora_up"], p["a0"], p["a_lora_up"],
            p["g_lora_up"], p["k_k"], p["k_a"], p["conv_w"], seg]
    if init is not None:
        in_specs += [row(p_shift), row(d_conv), row(d_conv)]
        args += list(init)

    out_specs = [row(d_rwkv)] * 7 + [row(d_conv)]
    out_shape = [jax.ShapeDtypeStruct((n, d_rwkv), F32)] * 7 + [
        jax.ShapeDtypeStruct((n, d_conv), BF16)]
    if full_out:
        out_specs += [row(p_shift), row(d_conv)]
        out_shape += [jax.ShapeDtypeStruct((n, p_shift), F32),
                      jax.ShapeDtypeStruct((n, d_conv), F32)]
    else:
        out_specs += [
            pl.BlockSpec((1, 1, p_shift), lambda i: (i // tiles_per_seq, 0, 0)),
            pl.BlockSpec((1, CONV_W - 1, d_conv), lambda i: (i // tiles_per_seq, 0, 0))]
        out_shape += [jax.ShapeDtypeStruct((n_seq, 1, p_shift), F32),
                      jax.ShapeDtypeStruct((n_seq, CONV_W - 1, d_conv), F32)]

    return pl.pallas_call(
        functools.partial(_premix_kernel, seq_len=seq_len,
                          tiles_per_seq=tiles_per_seq, d_rwkv=d_rwkv, d_conv=d_conv),
        grid=(n // tm,),
        in_specs=in_specs,
        out_specs=out_specs,
        out_shape=out_shape,
        scratch_shapes=[pltpu.VMEM((HDR + tm, p_total), F32)],
        compiler_params=pltpu.CompilerParams(
            dimension_semantics=("arbitrary",), vmem_limit_bytes=VMEM_LIMIT_BYTES),
        name="premix",
    )(*args)


def _bdot(a, b):
    return jnp.dot(a.astype(BF16), b.astype(BF16), preferred_element_type=F32)


def _bdot_nt(a, b):
    return lax.dot_general(a.astype(BF16), b.astype(BF16), (((1,), (1,)), ((), ())),
                           preferred_element_type=F32)


def _bdot_tn(a, b):
    return lax.dot_general(a.astype(BF16), b.astype(BF16), (((0,), (0,)), ((), ())),
                           preferred_element_type=F32)


def _wkv_kernel(r_ref, nkk_ref, b_ref, k_ref, v_ref, lw_ref, g_ref, s0_ref,
                rk_ref, lng_ref, lnb_ref, y_ref, sout_ref, s_scr, *, seqs, chunk, n_heads):
    c = pl.program_id(1)
    n_pairs = n_heads // 2
    c2 = 2 * chunk
    lanes = 2 * HEAD

    @pl.when(c == 0)
    def _():
        for q in range(seqs):
            for p in range(n_pairs):
                s_scr[q, p] = jnp.concatenate([s0_ref[q, 2 * p], s0_ref[q, 2 * p + 1]], axis=1)

    ii = lax.broadcasted_iota(jnp.int32, (chunk, chunk), 0)
    jj = lax.broadcasted_iota(jnp.int32, (chunk, chunk), 1)
    tri = (ii >= jj).astype(BF16)
    n_doublings = chunk.bit_length() - 1

    ri = lax.broadcasted_iota(jnp.int32, (c2, c2), 0)
    ci = lax.broadcasted_iota(jnp.int32, (c2, c2), 1)
    same = (ri // chunk) == (ci // chunk)
    strict = same & (ri % chunk > ci % chunk)
    incl = same & (ri % chunk >= ci % chunk)
    row_head = lax.broadcasted_iota(jnp.int32, (c2, lanes), 0) // chunk
    lane_head = lax.broadcasted_iota(jnp.int32, (c2, lanes), 1) // HEAD
    own = row_head == lane_head
    lo = lane_head == 0
    g_own = (lax.broadcasted_iota(jnp.int32, (lanes, lanes), 0) // HEAD
             == lax.broadcasted_iota(jnp.int32, (lanes, lanes), 1) // HEAD)
    lo_c = lax.broadcasted_iota(jnp.int32, (chunk, lanes), 1) < HEAD

    def stack(x):
        return jnp.concatenate([x, x], axis=0)

    def bd(x):
        return jnp.where(own, stack(x), 0.0)

    def own_half(x, low):
        swapped = pltpu.roll(x, HEAD, axis=1)
        rows = jnp.concatenate([x, swapped] if low else [swapped, x], axis=0)
        return jnp.where(lo if low else ~lo, rows, 0.0)

    t_in = chunk if len(lw_ref.shape) == 3 else lw_ref.shape[0] // seqs

    def seq_rows(ref, q):
        return ref[q] if t_in == chunk else ref[q * t_in:(q + 1) * t_in, :]

    def tokens(ref, q):
        x = seq_rows(ref, q)
        if t_in == chunk:
            return x
        return jnp.concatenate([x, jnp.zeros((chunk - t_in, x.shape[1]), x.dtype)], axis=0)

    ar_bd, bk_st, a_lo, r_lo, v_hi, bkh_bd, gam, unit = [], [], [], [], [], [], [], []
    for q in range(seqs):
        lw = tokens(lw_ref, q)
        lw_hi = lw.astype(BF16)
        lw_r1 = lw - lw_hi.astype(F32)
        lw_mid = lw_r1.astype(BF16)
        lw_lo = (lw_r1 - lw_mid.astype(F32)).astype(BF16)
        cum = (jnp.dot(tri, lw_hi, preferred_element_type=F32)
               + jnp.dot(tri, lw_mid, preferred_element_type=F32)
               + jnp.dot(tri, lw_lo, preferred_element_type=F32))
        last = cum[chunk - 1:chunk, :]
        e_in = jnp.exp(cum)
        e_out = jnp.exp(-cum)
        e_tail = jnp.exp(last - cum)
        b_all = tokens(b_ref, q)
        k_all = tokens(k_ref, q)
        a_t = tokens(nkk_ref, q) * jnp.exp(cum - lw)
        r_t = tokens(r_ref, q) * e_in
        b_t = b_all * e_out
        k_t = k_all * e_out
        b_h = b_all * e_tail
        k_h = k_all * e_tail
        gamma = jnp.exp(last)
        v_all = tokens(v_ref, q)
        for p in range(n_pairs):
            ps = slice(p * lanes, (p + 1) * lanes)
            ar_bd.append(jnp.concatenate([bd(a_t[:, ps]), bd(r_t[:, ps])], axis=0))
            bk_st.append(jnp.concatenate([stack(b_t[:, ps]), stack(k_t[:, ps])], axis=0))
            a_lo.append(own_half(a_t[:, ps], True))
            r_lo.append(own_half(r_t[:, ps], True))
            v_hi.append(own_half(v_all[:, ps], False))
            bkh_bd.append(jnp.concatenate([bd(b_h[:, ps]), bd(k_h[:, ps])], axis=0))
            gam.append(gamma[:, ps])
            unit.append((q, p, ps))

    units = range(seqs * n_pairs)
    m = [_bdot_nt(ar_bd[u], bk_st[u]) for u in units]
    l_ab = [jnp.where(strict, m[u][:c2, :c2], 0.0) for u in units]
    l_ak = [jnp.where(strict, m[u][:c2, c2:], 0.0) for u in units]
    l_rbk = [jnp.concatenate([jnp.where(incl, m[u][c2:, :c2], 0.0),
                              jnp.where(incl, m[u][c2:, c2:], 0.0)], axis=1) for u in units]
    y = [a_lo[u] + _bdot(l_ak[u], v_hi[u]) for u in units]
    pw = l_ab
    for step in range(n_doublings):
        if step < n_doublings - 1:
            py = [_bdot(pw[u], jnp.concatenate([y[u], pw[u]], axis=1)) for u in units]
            pw = [py[u][:, lanes:] for u in units]
            y = [y[u] + py[u][:, :lanes] for u in units]
        else:
            y = [y[u] + _bdot(pw[u], y[u]) for u in units]
    yv = [jnp.concatenate([y[u], v_hi[u]], axis=0) for u in units]
    gh = [_bdot_tn(yv[u], bkh_bd[u]) for u in units]
    lyv = [_bdot(l_rbk[u], yv[u]) for u in units]
    q_lo = [jnp.where(lo, r_lo[u] + lyv[u], 0.0) for u in units]
    q_bd = [jnp.where(own, q_lo[u] + pltpu.roll(q_lo[u], HEAD, axis=1), 0.0) for u in units]
    s = [s_scr[unit[u][0], unit[u][1]] for u in units]
    zero_s = jnp.zeros((HEAD, lanes), F32)
    o = [_bdot_nt(q_bd[u], jnp.concatenate([zero_s, s[u]], axis=0)) for u in units]
    g_bd = [jnp.where(g_own, stack(gh[u][:HEAD]), 0.0) for u in units]
    sg = [_bdot(s[u], g_bd[u]) for u in units]
    for u in units:
        s_scr[unit[u][0], unit[u][1]] = s[u] * gam[u] + sg[u] + gh[u][HEAD:]

    head_sum = g_own.astype(BF16)
    o = [o[u] + lyv[u] for u in units]
    o = [jnp.where(lo_c, pltpu.roll(o[u][:chunk], HEAD, axis=1), o[u][chunk:]) for u in units]
    mu = [_bdot(o[u], head_sum) * (1.0 / HEAD) for u in units]
    d = [o[u] - mu[u] for u in units]
    var = [_bdot(d[u] * d[u], head_sum) * (1.0 / HEAD) for u in units]
    bonus = [_bdot(tokens(r_ref, unit[u][0])[:, unit[u][2]]
                   * tokens(k_ref, unit[u][0])[:, unit[u][2]] * rk_ref[:, unit[u][2]], head_sum)
             for u in units]
    for q in range(seqs):
        y_pairs = []
        for p in range(n_pairs):
            u = q * n_pairs + p
            ps = unit[u][2]
            gn = (d[u] * lax.rsqrt(var[u] + GN_EPS))[:t_in] * lng_ref[:, ps] + lnb_ref[:, ps]
            y_pairs.append((gn + bonus[u][:t_in] * seq_rows(v_ref, q)[:, ps])
                           * seq_rows(g_ref, q)[:, ps])
        y_q = jnp.concatenate(y_pairs, axis=1).astype(y_ref.dtype)
        if t_in == chunk:
            y_ref[q] = y_q
        else:
            y_ref[q * t_in:(q + 1) * t_in, :] = y_q

    @pl.when(c == pl.num_programs(1) - 1)
    def _():
        for q in range(seqs):
            for p in range(n_pairs):
                sp = s_scr[q, p]
                sout_ref[q, 2 * p] = sp[:, :HEAD]
                sout_ref[q, 2 * p + 1] = sp[:, HEAD:]


def _wkv(ops, s0, rk, lng, lnb, *, seq_len, chunk, seqs):
    n, d_rwkv = ops[0].shape
    n_seq, n_heads = s0.shape[0], s0.shape[1]
    assert n_seq % seqs == 0
    if seq_len < chunk:
        n_chunks, y_shape, y_dtype = 1, (n, d_rwkv), F32
        tok = pl.BlockSpec((seqs * seq_len, d_rwkv), lambda q, c: (q, 0))
    else:
        assert seq_len % chunk == 0
        n_chunks, y_shape, y_dtype = seq_len // chunk, (n_seq, seq_len, d_rwkv), BF16
        ops = [o.reshape(y_shape) for o in ops]
        tok = pl.BlockSpec((seqs, chunk, d_rwkv), lambda q, c: (q, c, 0))
    state = pl.BlockSpec((seqs, n_heads, HEAD, HEAD), lambda q, c: (q, 0, 0, 0))
    y, s_new = pl.pallas_call(
        functools.partial(_wkv_kernel, seqs=seqs, chunk=chunk, n_heads=n_heads),
        grid=(n_seq // seqs, n_chunks),
        in_specs=[tok] * 7 + [state] + [pl.BlockSpec((1, d_rwkv), lambda q, c: (0, 0))] * 3,
        out_specs=[tok, state],
        out_shape=[jax.ShapeDtypeStruct(y_shape, y_dtype),
                   jax.ShapeDtypeStruct(s0.shape, F32)],
        scratch_shapes=[pltpu.VMEM((seqs, n_heads // 2, HEAD, 2 * HEAD), F32)],
        compiler_params=pltpu.CompilerParams(
            dimension_semantics=("arbitrary", "arbitrary"),
            vmem_limit_bytes=VMEM_LIMIT_BYTES),
        name="wkv",
    )(*ops, s0, rk, lng, lnb)
    return y.reshape(n, d_rwkv), s_new


def _out_ffn_kernel(x_ref, yr_ref, yc_ref, wo_r_ref, wo_c_ref, g2_ref, b2_ref,
                    wg_ref, wu_ref, wd_ref, g3_ref, b3_ref, o_ref, *, alpha):
    mix = jnp.dot(yr_ref[...].astype(BF16), wo_r_ref[...], preferred_element_type=F32)
    mix = mix + jnp.dot(yc_ref[...], wo_c_ref[...], preferred_element_type=F32)
    x = _layer_norm(alpha * x_ref[...] + mix, g2_ref[...], b2_ref[...])
    y = alpha * x + 0.5 * _swiglu(x.astype(BF16), wg_ref, wu_ref, wd_ref)
    o_ref[...] = _layer_norm(y, g3_ref[...], b3_ref[...])


def _out_ffn(x1, y_rwkv, y_conv, wo_r, wo_c, g2, b2, wg, wu, wd, g3, b3, alpha):
    n, d = x1.shape
    dff = wg.shape[1]
    d_rwkv, d_conv = y_rwkv.shape[1], y_conv.shape[1]
    tm = TOKEN_TILE
    assert n % tm == 0

    def row(c):
        return pl.BlockSpec((tm, c), lambda i: (i, 0))

    return pl.pallas_call(
        functools.partial(_out_ffn_kernel, alpha=alpha),
        grid=(n // tm,),
        in_specs=[row(d), row(d_rwkv), row(d_conv), _resident((d_rwkv, d)), _resident((d_conv, d)),
                  _resident((1, d)), _resident((1, d)),
                  _resident((d, dff)), _resident((d, dff)), _resident((dff, d)),
                  _resident((1, d)), _resident((1, d))],
        out_specs=row(d),
        out_shape=jax.ShapeDtypeStruct((n, d), F32),
        compiler_params=pltpu.CompilerParams(
            dimension_semantics=("arbitrary",), vmem_limit_bytes=VMEM_LIMIT_BYTES),
        name="out_ffn",
    )(x1, y_rwkv, y_conv, wo_r, wo_c, g2, b2, wg, wu, wd, g3, b3)


def _trunk_layer(x, p, seg, alpha, *, seq_len, chunk, seqs, wkv0, init):
    x1 = _ffn_ln(x, p["ffn1_wg"], p["ffn1_wu"], p["ffn1_wd"], p["ln1_g"], p["ln1_b"], alpha)
    pm = _premix(x1, p, seg, seq_len=seq_len, init=init)
    ops, y_conv, st_a, st_b = pm[:7], pm[7], pm[8], pm[9]
    y_rwkv, wkv_new = _wkv(ops, wkv0, p["r_k"], p["lnx_g"], p["lnx_b"],
                           seq_len=seq_len, chunk=chunk, seqs=seqs)
    d_rwkv = y_rwkv.shape[1]
    y = _out_ffn(x1, y_rwkv, y_conv, p["w_out"][:d_rwkv], p["w_out"][d_rwkv:],
                 p["ln2_g"], p["ln2_b"], p["ffn2_wg"], p["ffn2_wu"], p["ffn2_wd"],
                 p["ln3_g"], p["ln3_b"], alpha)
    return y, wkv_new, st_a, st_b


def _expand_rows(x, seq_len):
    n_seq, k, c = x.shape
    return jnp.pad(x, ((0, 0), (0, seq_len - k), (0, 0))).reshape(n_seq * seq_len, c)


def kernel(x_prompt, x_sample, state_wkv, state_shift, state_conv, ln1_g, ln1_b, ffn1_wg, ffn1_wu, ffn1_wd, w_in, mu_shift, w0, w_lora_up, a0, a_lora_up, g_lora_up, k_k, k_a, r_k, lnx_g, lnx_b, conv_w, w_out, ln2_g, ln2_b, ffn2_wg, ffn2_wu, ffn2_wd, ln3_g, ln3_b):
    depth = ln1_g.shape[0]
    bp, tp, d = x_prompt.shape
    bs, ts, _ = x_sample.shape
    n_heads = state_wkv.shape[2]
    d_rwkv = n_heads * HEAD
    alpha = (2.0 * depth) ** 0.25
    assert CONV_W - 1 <= ts <= SHORT_CHUNK

    head_id = jnp.arange(d_rwkv) // HEAD
    seg = (head_id[:, None] == head_id[None, :]).astype(BF16)

    xp = x_prompt.reshape(bp * tp, d)
    xs = x_sample.reshape(bs * ts, d)

    outs = {k: [] for k in ("wkv_p", "shift_p", "conv_p", "wkv_s", "shift_s", "conv_s")}
    for l in range(depth):
        vec = lambda a: a[l].reshape(1, -1).astype(F32)
        p = {
            "ln1_g": vec(ln1_g), "ln1_b": vec(ln1_b),
            "ffn1_wg": ffn1_wg[l].astype(BF16), "ffn1_wu": ffn1_wu[l].astype(BF16),
            "ffn1_wd": ffn1_wd[l].astype(BF16),
            "w_in": w_in[l].astype(BF16), "mu_shift": vec(mu_shift), "w0": vec(w0),
            "w_lora_up": w_lora_up[l].astype(BF16), "a0": vec(a0),
            "a_lora_up": a_lora_up[l].astype(BF16), "g_lora_up": g_lora_up[l].astype(BF16),
            "k_k": vec(k_k), "k_a": vec(k_a), "r_k": vec(r_k),
            "lnx_g": vec(lnx_g), "lnx_b": vec(lnx_b),
            "conv_w": conv_w[l].astype(F32), "w_out": w_out[l].astype(BF16),
            "ln2_g": vec(ln2_g), "ln2_b": vec(ln2_b),
            "ffn2_wg": ffn2_wg[l].astype(BF16), "ffn2_wu": ffn2_wu[l].astype(BF16),
            "ffn2_wd": ffn2_wd[l].astype(BF16),
            "ln3_g": vec(ln3_g), "ln3_b": vec(ln3_b),
        }
        wkv0_p = jnp.zeros((bp, n_heads, HEAD, HEAD), F32)
        xp, wp, sp, cp = _trunk_layer(xp, p, seg, alpha, seq_len=tp, chunk=CHUNK,
                                      seqs=bp, wkv0=wkv0_p, init=None)
        conv0 = state_conv[l].astype(F32)
        init = (_expand_rows(state_shift[l].astype(F32)[:, None, :], ts),
                _expand_rows(conv0[:, 1:2, :], ts),
                _expand_rows(conv0, ts))
        xs, wsm, ps_s, u_s = _trunk_layer(xs, p, seg, alpha, seq_len=ts, chunk=SHORT_CHUNK,
                                          seqs=8, wkv0=state_wkv[l].astype(F32), init=init)
        outs["wkv_p"].append(wp)
        outs["shift_p"].append(sp[:, 0, :])
        outs["conv_p"].append(cp)
        outs["wkv_s"].append(wsm)
        outs["shift_s"].append(ps_s[ts - 1::ts])
        outs["conv_s"].append(jnp.stack(
            [u_s[t::ts] for t in range(ts - (CONV_W - 1), ts)], axis=1))

    y_prompt = xp.reshape(bp, tp, d)
    y_sample = xs.reshape(bs, ts, d)
    return (y_prompt, y_sample,
            jnp.stack(outs["wkv_p"]), jnp.stack(outs["shift_p"]), jnp.stack(outs["conv_p"]),
            jnp.stack(outs["wkv_s"]), jnp.stack(outs["shift_s"]), jnp.stack(outs["conv_s"]))
```

```python
import functools
import math

import jax
import jax.numpy as jnp
from jax import lax
from jax.experimental import pallas as pl
from jax.experimental.pallas import tpu as pltpu

F32 = jnp.float32
BF16 = jnp.bfloat16

HEAD = 64
LORA_W = 64
LORA_A = 64
LORA_G = 128
CONV_W = 3
LN_EPS = 1e-5
GN_EPS = 1e-5 * HEAD

VMEM_LIMIT_BYTES = 56 * 1024 * 1024
TOKEN_TILE = 512
FFN_ROW_GROUPS = 2
CHUNK = 64


def _resident(shape):
    nd = len(shape)
    return pl.BlockSpec(shape, lambda *_: (0,) * nd, pipeline_mode=pl.Buffered(1))


def _layer_norm(x, g, b):
    mu = jnp.mean(x, axis=-1, keepdims=True)
    d = x - mu
    var = jnp.mean(d * d, axis=-1, keepdims=True)
    return d * lax.rsqrt(var + LN_EPS) * g + b


def _ffn_post_ln(x, wg_ref, wu_ref, wd_ref, g_ref, b_ref, o_ref, alpha):
    xb = x.astype(BF16)
    g = jnp.dot(xb, wg_ref[...], preferred_element_type=F32)
    u = jnp.dot(xb, wu_ref[...], preferred_element_type=F32)
    h = (jax.nn.silu(g) * u).astype(BF16)
    rows = x.shape[0] // FFN_ROW_GROUPS
    for s in range(FFN_ROW_GROUPS):
        sl = slice(s * rows, (s + 1) * rows)
        y = alpha * x[sl] + 0.5 * jnp.dot(h[sl], wd_ref[...], preferred_element_type=F32)
        o_ref[sl, :] = _layer_norm(y, g_ref[...], b_ref[...])


def _ffn_ln_kernel(x_ref, wg_ref, wu_ref, wd_ref, g_ref, b_ref, o_ref, *, alpha):
    _ffn_post_ln(x_ref[...], wg_ref, wu_ref, wd_ref, g_ref, b_ref, o_ref, alpha)


def _ffn_ln(x, wg, wu, wd, g, b, alpha):
    n, d = x.shape
    dff = wg.shape[1]
    tm = TOKEN_TILE
    assert n % tm == 0
    row = pl.BlockSpec((tm, d), lambda i: (i, 0))
    return pl.pallas_call(
        functools.partial(_ffn_ln_kernel, alpha=alpha),
        grid=(n // tm,),
        in_specs=[row, _resident((d, dff)), _resident((d, dff)), _resident((dff, d)),
                  _resident((1, d)), _resident((1, d))],
        out_specs=row,
        out_shape=jax.ShapeDtypeStruct((n, d), F32),
        compiler_params=pltpu.CompilerParams(
            dimension_semantics=("arbitrary",), vmem_limit_bytes=VMEM_LIMIT_BYTES),
        name="ffn_ln",
    )(x, wg, wu, wd, g, b)


HDR = 8


def _shift_rows(x, head, k):
    rolled = pltpu.roll(x, k, axis=0)
    row8 = lax.broadcasted_iota(jnp.int32, (HDR, 1), 0)
    top = jnp.where(row8 < k, pltpu.roll(head, k, axis=0), rolled[:HDR])
    return jnp.concatenate([top, rolled[HDR:]], axis=0)


class _PremixRefs:
    def __init__(self, refs, tiles_per_seq, full_out):
        refs = list(refs)
        (self.x, self.win, self.mu, self.w0, self.wup, self.a0, self.aup, self.gup, self.kk,
         self.ka, self.cw, self.seg) = refs[:12]
        refs = refs[12:]
        if tiles_per_seq == 0:
            self.sh0, self.cv0 = refs[:2]
            refs = refs[2:]
        (self.r, self.nkk, self.b, self.k, self.v, self.lw, self.g, self.yc) = refs[:8]
        refs = refs[8:]
        if full_out:
            self.ps_out, self.u_out = refs[:2]
        else:
            self.shift, self.tail = refs[:2]
        self.bufs = refs[2:]


def _premix_gates(rf, buf, first, *, tiles_per_seq, d_rwkv, d_conv):
    p_shift = 3 * d_rwkv + LORA_W + LORA_A + LORA_G
    tm = rf.x.shape[0]
    ps = buf[HDR:, :p_shift]
    c_b = buf[HDR:, p_shift:p_shift + d_conv]
    u = buf[HDR:, p_shift + d_conv:p_shift + 2 * d_conv] * buf[HDR:, p_shift + 2 * d_conv:]

    if tiles_per_seq:
        head = jnp.where(first, 0.0, buf[0:HDR, :])
        head_u = head[:, p_shift + d_conv:p_shift + 2 * d_conv] * head[:, p_shift + 2 * d_conv:]
        prev = _shift_rows(ps, head[:, :p_shift], 1)
        um1 = _shift_rows(u, head_u, 1)
        um2 = _shift_rows(u, head_u, 2)
    else:
        n_seq = rf.sh0.shape[0]
        prev = jnp.concatenate([rf.sh0[...], ps[:tm - n_seq]], axis=0)
        hist = jnp.concatenate([rf.cv0[t] for t in range(CONV_W - 1)] + [u], axis=0)
        um2 = hist[:tm]
        um1 = hist[n_seq:n_seq + tm]

    cw = rf.cw[...]
    z = cw[0:1, :] * um2
    z = z + cw[1:2, :] * um1
    z = z + cw[2:3, :] * u
    rf.yc[...] = (c_b * z).astype(BF16)
    if tiles_per_seq:
        rf.shift[0] = ps[tm - 1:tm, :]
        rf.tail[0] = u[tm - (CONV_W - 1):tm, :]
    else:
        rf.ps_out[...] = ps
        rf.u_out[...] = u

    mix = ps + rf.mu[...] * (prev - ps)
    r = mix[:, 0:d_rwkv]
    k = mix[:, d_rwkv:2 * d_rwkv]
    v = mix[:, 2 * d_rwkv:3 * d_rwkv]
    o = 3 * d_rwkv
    wd = mix[:, o:o + LORA_W]
    ad = mix[:, o + LORA_W:o + LORA_W + LORA_A]
    gd = mix[:, o + LORA_W + LORA_A:p_shift]

    rf.r[...] = r
    kk = k * rf.kk[...]
    dot_in = (jnp.tanh(wd).astype(BF16), ad.astype(BF16), jax.nn.sigmoid(gd).astype(BF16),
              (kk * kk).astype(BF16))
    return k, v, kk, dot_in


def _premix_dots(rf, dot_in):
    tanh_wd, ad, sig_gd, kk_sq = dot_in
    w_pre = jnp.dot(tanh_wd, rf.wup[...], preferred_element_type=F32)
    a_pre = jnp.dot(ad, rf.aup[...], preferred_element_type=F32)
    rf.g[...] = jnp.dot(sig_gd, rf.gup[...], preferred_element_type=F32)
    ss = jnp.dot(kk_sq, rf.seg[...], preferred_element_type=F32)
    return w_pre, a_pre, ss


def _premix_finish(rf, gates, dots):
    k, v, kk, _ = gates
    w_pre, a_pre, ss = dots
    log_decay = -math.exp(-0.5) * jax.nn.sigmoid(rf.w0[...] + w_pre)
    a = jax.nn.sigmoid(rf.a0[...] + a_pre)
    kk = kk * lax.rsqrt(jnp.maximum(ss, 1e-24))
    k = k * (1.0 + (a - 1.0) * rf.ka[...])
    nkk = -kk
    b = kk * a
    rf.nkk[...] = nkk
    rf.b[...] = b
    rf.k[...] = k
    rf.v[...] = v
    rf.lw[...] = log_decay


def _premix_kernel(*refs, tiles_per_seq, d_rwkv, d_conv):
    rf = _PremixRefs(refs, tiles_per_seq, full_out=tiles_per_seq == 0)
    buf, = rf.bufs
    tm = rf.x.shape[0]
    i = pl.program_id(0)
    cfg = dict(tiles_per_seq=tiles_per_seq, d_rwkv=d_rwkv, d_conv=d_conv)

    if tiles_per_seq:
        @pl.when(i == 0)
        def _():
            buf[0:HDR, :] = jnp.zeros((HDR, buf.shape[1]), F32)

    buf[HDR:, :] = jnp.dot(rf.x[...].astype(BF16), rf.win[...], preferred_element_type=F32)
    first = (i % tiles_per_seq == 0) if tiles_per_seq else None
    gates = _premix_gates(rf, buf, first, **cfg)
    _premix_finish(rf, gates, _premix_dots(rf, gates[3]))
    if tiles_per_seq:
        buf[0:HDR, :] = buf[tm:tm + HDR, :]


def _premix(x1, p, seg, *, seq_len, init=None):
    n, d = x1.shape
    tm = TOKEN_TILE
    d_rwkv = p["w0"].shape[1]
    d_conv = p["conv_w"].shape[1]
    p_shift = 3 * d_rwkv + LORA_W + LORA_A + LORA_G
    p_total = p_shift + 3 * d_conv
    n_seq = n // seq_len
    if init is None:
        assert seq_len % tm == 0
        tiles_per_seq = seq_len // tm
    else:
        assert n == tm and n_seq % HDR == 0
        tiles_per_seq = 0
    full_out = init is not None

    def row(c):
        return pl.BlockSpec((tm, c), lambda i: (i, 0))

    in_specs = [row(d), _resident((d, p_total)), _resident((1, p_shift)),
                _resident((1, d_rwkv)), _resident((LORA_W, d_rwkv)),
                _resident((1, d_rwkv)), _resident((LORA_A, d_rwkv)),
                _resident((LORA_G, d_rwkv)), _resident((1, d_rwkv)), _resident((1, d_rwkv)),
                _resident((CONV_W, d_conv)), _resident((d_rwkv, d_rwkv))]
    args = [x1, p["w_in"], p["mu_shift"], p["w0"], p["w_lora_up"], p["a0"], p["a_lora_up"],
            p["g_lora_up"], p["k_k"], p["k_a"], p["conv_w"], seg]
    if init is not None:
        in_specs += [_resident((n_seq, p_shift)), _resident((CONV_W - 1, n_seq, d_conv))]
        args += list(init)

    out_specs = [row(d_rwkv)] * 7 + [row(d_conv)]
    out_shape = [jax.ShapeDtypeStruct((n, d_rwkv), F32)] * 7 + [
        jax.ShapeDtypeStruct((n, d_conv), BF16)]
    if full_out:
        out_specs += [row(p_shift), row(d_conv)]
        out_shape += [jax.ShapeDtypeStruct((n, p_shift), F32),
                      jax.ShapeDtypeStruct((n, d_conv), F32)]
    else:
        out_specs += [
            pl.BlockSpec((1, 1, p_shift), lambda i: (i // tiles_per_seq, 0, 0)),
            pl.BlockSpec((1, CONV_W - 1, d_conv), lambda i: (i // tiles_per_seq, 0, 0))]
        out_shape += [jax.ShapeDtypeStruct((n_seq, 1, p_shift), F32),
                      jax.ShapeDtypeStruct((n_seq, CONV_W - 1, d_conv), F32)]

    return pl.pallas_call(
        functools.partial(_premix_kernel, tiles_per_seq=tiles_per_seq, d_rwkv=d_rwkv,
                          d_conv=d_conv),
        grid=(n // tm,),
        in_specs=in_specs,
        out_specs=out_specs,
        out_shape=out_shape,
        scratch_shapes=[pltpu.VMEM((HDR + tm, p_total), F32)],
        compiler_params=pltpu.CompilerParams(
            dimension_semantics=("arbitrary",), vmem_limit_bytes=VMEM_LIMIT_BYTES),
        name="premix",
    )(*args)


def _bdot(a, b):
    return jnp.dot(a.astype(BF16), b.astype(BF16), preferred_element_type=F32)


def _bdot_nt(a, b):
    return lax.dot_general(a.astype(BF16), b.astype(BF16), (((1,), (1,)), ((), ())),
                           preferred_element_type=F32)


def _bdot_tn(a, b):
    return lax.dot_general(a.astype(BF16), b.astype(BF16), (((0,), (0,)), ((), ())),
                           preferred_element_type=F32)


def _wkv_kernel(r_ref, nkk_ref, b_ref, k_ref, v_ref, lw_ref, g_ref, s0_ref,
                rk_ref, lng_ref, lnb_ref, y_ref, sout_ref, s_scr, *, seqs, chunk, n_heads):
    c = pl.program_id(1)
    n_pairs = n_heads // 2
    c2 = 2 * chunk
    lanes = 2 * HEAD

    @pl.when(c == 0)
    def _():
        for q in range(seqs):
            for p in range(n_pairs):
                s_scr[q, p] = jnp.concatenate([s0_ref[q, 2 * p], s0_ref[q, 2 * p + 1]], axis=1)

    ii = lax.broadcasted_iota(jnp.int32, (chunk, chunk), 0)
    jj = lax.broadcasted_iota(jnp.int32, (chunk, chunk), 1)
    tri = (ii >= jj).astype(BF16)
    n_doublings = chunk.bit_length() - 1

    ri = lax.broadcasted_iota(jnp.int32, (c2, c2), 0)
    ci = lax.broadcasted_iota(jnp.int32, (c2, c2), 1)
    same = (ri // chunk) == (ci // chunk)
    strict = same & (ri % chunk > ci % chunk)
    incl = same & (ri % chunk >= ci % chunk)
    row_head = lax.broadcasted_iota(jnp.int32, (c2, lanes), 0) // chunk
    lane_head = lax.broadcasted_iota(jnp.int32, (c2, lanes), 1) // HEAD
    own = row_head == lane_head
    lo = lane_head == 0
    g_own = (lax.broadcasted_iota(jnp.int32, (lanes, lanes), 0) // HEAD
             == lax.broadcasted_iota(jnp.int32, (lanes, lanes), 1) // HEAD)
    lo_c = lax.broadcasted_iota(jnp.int32, (chunk, lanes), 1) < HEAD

    def stack(x):
        return jnp.concatenate([x, x], axis=0)

    def bd(x):
        return jnp.where(own, stack(x), 0.0)

    def own_half(x, low):
        swapped = pltpu.roll(x, HEAD, axis=1)
        rows = jnp.concatenate([x, swapped] if low else [swapped, x], axis=0)
        return jnp.where(lo if low else ~lo, rows, 0.0)

    ar_bd, bk_st, a_lo, r_lo, v_hi, bkh_bd, gam, unit = [], [], [], [], [], [], [], []
    for q in range(seqs):
        lw = lw_ref[q]
        lw_hi = lw.astype(BF16)
        lw_r1 = lw - lw_hi.astype(F32)
        lw_mid = lw_r1.astype(BF16)
        lw_lo = (lw_r1 - lw_mid.astype(F32)).astype(BF16)
        cum = (jnp.dot(tri, lw_hi, preferred_element_type=F32)
               + jnp.dot(tri, lw_mid, preferred_element_type=F32)
               + jnp.dot(tri, lw_lo, preferred_element_type=F32))
        last = cum[chunk - 1:chunk, :]
        e_in = jnp.exp(cum)
        e_out = jnp.exp(-cum)
        e_tail = jnp.exp(last - cum)
        a_t = nkk_ref[q] * jnp.exp(cum - lw)
        r_t = r_ref[q] * e_in
        b_t = b_ref[q] * e_out
        k_t = k_ref[q] * e_out
        b_h = b_ref[q] * e_tail
        k_h = k_ref[q] * e_tail
        gamma = jnp.exp(last)
        v_all = v_ref[q]
        for p in range(n_pairs):
            ps = slice(p * lanes, (p + 1) * lanes)
            ar_bd.append(jnp.concatenate([bd(a_t[:, ps]), bd(r_t[:, ps])], axis=0))
            bk_st.append(jnp.concatenate([stack(b_t[:, ps]), stack(k_t[:, ps])], axis=0))
            a_lo.append(own_half(a_t[:, ps], True))
            r_lo.append(own_half(r_t[:, ps], True))
            v_hi.append(own_half(v_all[:, ps], False))
            bkh_bd.append(jnp.concatenate([bd(b_h[:, ps]), bd(k_h[:, ps])], axis=0))
            gam.append(gamma[:, ps])
            unit.append((q, p, ps))

    units = range(seqs * n_pairs)
    m = [_bdot_nt(ar_bd[u], bk_st[u]) for u in units]
    l_ab = [jnp.where(strict, m[u][:c2, :c2], 0.0) for u in units]
    l_ak = [jnp.where(strict, m[u][:c2, c2:], 0.0) for u in units]
    l_rbk = [jnp.concatenate([jnp.where(incl, m[u][c2:, :c2], 0.0),
                              jnp.where(incl, m[u][c2:, c2:], 0.0)], axis=1) for u in units]
    y = [a_lo[u] + _bdot(l_ak[u], v_hi[u]) for u in units]
    pw = l_ab
    for step in range(n_doublings):
        if step < n_doublings - 1:
            py = [_bdot(pw[u], jnp.concatenate([y[u], pw[u]], axis=1)) for u in units]
            pw = [py[u][:, lanes:] for u in units]
            y = [y[u] + py[u][:, :lanes] for u in units]
        else:
            y = [y[u] + _bdot(pw[u], y[u]) for u in units]
    yv = [jnp.concatenate([y[u], v_hi[u]], axis=0) for u in units]
    gh = [_bdot_tn(yv[u], bkh_bd[u]) for u in units]
    lyv = [_bdot(l_rbk[u], yv[u]) for u in units]
    q_lo = [jnp.where(lo, r_lo[u] + lyv[u], 0.0) for u in units]
    q_bd = [jnp.where(own, q_lo[u] + pltpu.roll(q_lo[u], HEAD, axis=1), 0.0) for u in units]
    s = [s_scr[unit[u][0], unit[u][1]] for u in units]
    zero_s = jnp.zeros((HEAD, lanes), F32)
    o = [_bdot_nt(q_bd[u], jnp.concatenate([zero_s, s[u]], axis=0)) for u in units]
    g_bd = [jnp.where(g_own, stack(gh[u][:HEAD]), 0.0) for u in units]
    sg = [_bdot(s[u], g_bd[u]) for u in units]
    for u in units:
        s_scr[unit[u][0], unit[u][1]] = s[u] * gam[u] + sg[u] + gh[u][HEAD:]

    head_sum = g_own.astype(BF16)
    o = [o[u] + lyv[u] for u in units]
    o = [jnp.where(lo_c, pltpu.roll(o[u][:chunk], HEAD, axis=1), o[u][chunk:]) for u in units]
    mu = [_bdot(o[u], head_sum) * (1.0 / HEAD) for u in units]
    d = [o[u] - mu[u] for u in units]
    var = [_bdot(d[u] * d[u], head_sum) * (1.0 / HEAD) for u in units]
    bonus = [_bdot(r_ref[unit[u][0]][:, unit[u][2]] * k_ref[unit[u][0]][:, unit[u][2]]
                   * rk_ref[:, unit[u][2]], head_sum) for u in units]
    for q in range(seqs):
        y_pairs = []
        for p in range(n_pairs):
            u = q * n_pairs + p
            ps = unit[u][2]
            gn = d[u] * lax.rsqrt(var[u] + GN_EPS) * lng_ref[:, ps] + lnb_ref[:, ps]
            y_pairs.append((gn + bonus[u] * v_ref[q][:, ps]) * g_ref[q][:, ps])
        y_ref[q] = jnp.concatenate(y_pairs, axis=1).astype(BF16)

    @pl.when(c == pl.num_programs(1) - 1)
    def _():
        for q in range(seqs):
            for p in range(n_pairs):
                sp = s_scr[q, p]
                sout_ref[q, 2 * p] = sp[:, :HEAD]
                sout_ref[q, 2 * p + 1] = sp[:, HEAD:]


def _wkv(ops, s0, rk, lng, lnb, *, seq_len, chunk, seqs):
    n, d_rwkv = ops[0].shape
    n_seq, n_heads = s0.shape[0], s0.shape[1]
    assert n_seq % seqs == 0 and seq_len % chunk == 0
    ops = [o.reshape(n_seq, seq_len, d_rwkv) for o in ops]
    tok = pl.BlockSpec((seqs, chunk, d_rwkv), lambda q, c: (q, c, 0))
    state = pl.BlockSpec((seqs, n_heads, HEAD, HEAD), lambda q, c: (q, 0, 0, 0))
    y, s_new = pl.pallas_call(
        functools.partial(_wkv_kernel, seqs=seqs, chunk=chunk, n_heads=n_heads),
        grid=(n_seq // seqs, seq_len // chunk),
        in_specs=[tok] * 7 + [state] + [pl.BlockSpec((1, d_rwkv), lambda q, c: (0, 0))] * 3,
        out_specs=[tok, state],
        out_shape=[jax.ShapeDtypeStruct((n_seq, seq_len, d_rwkv), BF16),
                   jax.ShapeDtypeStruct(s0.shape, F32)],
        scratch_shapes=[pltpu.VMEM((seqs, n_heads // 2, HEAD, 2 * HEAD), F32)],
        compiler_params=pltpu.CompilerParams(
            dimension_semantics=("arbitrary", "arbitrary"),
            vmem_limit_bytes=VMEM_LIMIT_BYTES),
        name="wkv",
    )(*ops, s0, rk, lng, lnb)
    return y.reshape(n, d_rwkv), s_new


def _wkv_steps_kernel(r_ref, nkk_ref, b_ref, k_ref, v_ref, lw_ref, g_ref, s0_ref,
                      rk_ref, lng_ref, lnb_ref, y_ref, sout_ref, vt_scr, o_scr, *, n_t):
    n_seq = s0_ref.shape[-1]

    def feature_major(ref, t):
        return ref[t * n_seq:(t + 1) * n_seq, :].T

    for t in range(n_t):
        src = s0_ref if t == 0 else sout_ref
        nkk = feature_major(nkk_ref, t)
        beta = feature_major(b_ref, t)
        key = feature_major(k_ref, t)
        rec = feature_major(r_ref, t)
        decay = jnp.exp(feature_major(lw_ref, t))
        vt_scr[...] = feature_major(v_ref, t)
        for h in range(2):
            hs = slice(h * HEAD, (h + 1) * HEAD)
            nkk_h, beta_h, key_h, rec_h, decay_h = nkk[hs], beta[hs], key[hs], rec[hs], decay[hs]

            def one_row(i, carry):
                s = src[h, i]
                sa = jnp.sum(s * nkk_h, axis=0, keepdims=True)
                v_i = vt_scr[pl.ds(h * HEAD + i, 1), :]
                s = s * decay_h + sa * beta_h + v_i * key_h
                sout_ref[h, i] = s
                o_scr[pl.ds(h * HEAD + i, 1), :] = jnp.sum(s * rec_h, axis=0, keepdims=True)
                return carry

            lax.fori_loop(0, HEAD, one_row, 0, unroll=4)

        v_t = vt_scr[...]
        bonus_prod = rec * key * rk_ref[...]
        ys = []
        for h in range(2):
            hs = slice(h * HEAD, (h + 1) * HEAD)
            o = o_scr[hs, :]
            mu = jnp.mean(o, axis=0, keepdims=True)
            d = o - mu
            var = jnp.mean(d * d, axis=0, keepdims=True)
            gn = d * lax.rsqrt(var + GN_EPS) * lng_ref[hs, :] + lnb_ref[hs, :]
            bonus = jnp.sum(bonus_prod[hs], axis=0, keepdims=True)
            ys.append(gn + bonus * v_t[hs])
        y_t = jnp.concatenate(ys, axis=0) * feature_major(g_ref, t)
        y_ref[t * n_seq:(t + 1) * n_seq, :] = y_t.T.astype(y_ref.dtype)


def _wkv_steps(ops, s0, rk, lng, lnb, *, n_t):
    n, d_rwkv = ops[0].shape
    n_heads, n_seq = s0.shape[0], s0.shape[-1]
    lanes = 2 * HEAD
    assert n == n_t * n_seq and n_seq == lanes

    def bcast(a):
        return jnp.broadcast_to(a.reshape(d_rwkv, 1), (d_rwkv, n_seq))

    tok = pl.BlockSpec((n, lanes), lambda p: (0, p))
    state = pl.BlockSpec((2, HEAD, HEAD, n_seq), lambda p: (p, 0, 0, 0))
    par = pl.BlockSpec((lanes, n_seq), lambda p: (p, 0))
    return pl.pallas_call(
        functools.partial(_wkv_steps_kernel, n_t=n_t),
        grid=(n_heads // 2,),
        in_specs=[tok] * 7 + [state] + [par] * 3,
        out_specs=[tok, state],
        out_shape=[jax.ShapeDtypeStruct((n, d_rwkv), BF16),
                   jax.ShapeDtypeStruct(s0.shape, F32)],
        scratch_shapes=[pltpu.VMEM((lanes, n_seq), F32), pltpu.VMEM((lanes, n_seq), F32)],
        compiler_params=pltpu.CompilerParams(
            dimension_semantics=("arbitrary",), vmem_limit_bytes=VMEM_LIMIT_BYTES),
        name="wkv_steps",
    )(*ops, s0, bcast(rk), bcast(lng), bcast(lnb))


def _out_ffn_kernel(x_ref, yr_ref, yc_ref, wo_r_ref, wo_c_ref, g2_ref, b2_ref,
                    wg_ref, wu_ref, wd_ref, g3_ref, b3_ref, o_ref, *, alpha):
    mix = jnp.dot(yr_ref[...].astype(BF16), wo_r_ref[...], preferred_element_type=F32)
    mix = mix + jnp.dot(yc_ref[...], wo_c_ref[...], preferred_element_type=F32)
    x = _layer_norm(alpha * x_ref[...] + mix, g2_ref[...], b2_ref[...])
    _ffn_post_ln(x, wg_ref, wu_ref, wd_ref, g3_ref, b3_ref, o_ref, alpha)


def _out_ffn(x1, y_rwkv, y_conv, wo_r, wo_c, g2, b2, wg, wu, wd, g3, b3, alpha):
    n, d = x1.shape
    dff = wg.shape[1]
    d_rwkv, d_conv = y_rwkv.shape[1], y_conv.shape[1]
    tm = TOKEN_TILE
    assert n % tm == 0

    def row(c):
        return pl.BlockSpec((tm, c), lambda i: (i, 0))

    return pl.pallas_call(
        functools.partial(_out_ffn_kernel, alpha=alpha),
        grid=(n // tm,),
        in_specs=[row(d), row(d_rwkv), row(d_conv), _resident((d_rwkv, d)), _resident((d_conv, d)),
                  _resident((1, d)), _resident((1, d)),
                  _resident((d, dff)), _resident((d, dff)), _resident((dff, d)),
                  _resident((1, d)), _resident((1, d))],
        out_specs=row(d),
        out_shape=jax.ShapeDtypeStruct((n, d), F32),
        compiler_params=pltpu.CompilerParams(
            dimension_semantics=("arbitrary",), vmem_limit_bytes=VMEM_LIMIT_BYTES),
        name="out_ffn",
    )(x1, y_rwkv, y_conv, wo_r, wo_c, g2, b2, wg, wu, wd, g3, b3)


def _trunk_layer(x, p, seg, alpha, *, seq_len, wkv0, init):
    x1 = _ffn_ln(x, p["ffn1_wg"], p["ffn1_wu"], p["ffn1_wd"], p["ln1_g"], p["ln1_b"], alpha)
    pm = _premix(x1, p, seg, seq_len=seq_len, init=init)
    ops, y_conv, st_a, st_b = pm[:7], pm[7], pm[8], pm[9]
    if init is None:
        y_rwkv, wkv_new = _wkv(ops, wkv0, p["r_k"], p["lnx_g"], p["lnx_b"],
                               seq_len=seq_len, chunk=CHUNK, seqs=wkv0.shape[0])
    else:
        y_rwkv, wkv_new = _wkv_steps(ops, wkv0, p["r_k"], p["lnx_g"], p["lnx_b"], n_t=seq_len)
    d_rwkv = y_rwkv.shape[1]
    y = _out_ffn(x1, y_rwkv, y_conv, p["w_out"][:d_rwkv], p["w_out"][d_rwkv:],
                 p["ln2_g"], p["ln2_b"], p["ffn2_wg"], p["ffn2_wu"], p["ffn2_wd"],
                 p["ln3_g"], p["ln3_b"], alpha)
    return y, wkv_new, st_a, st_b


def kernel(x_prompt, x_sample, state_wkv, state_shift, state_conv, ln1_g, ln1_b, ffn1_wg, ffn1_wu, ffn1_wd, w_in, mu_shift, w0, w_lora_up, a0, a_lora_up, g_lora_up, k_k, k_a, r_k, lnx_g, lnx_b, conv_w, w_out, ln2_g, ln2_b, ffn2_wg, ffn2_wu, ffn2_wd, ln3_g, ln3_b):
    depth = ln1_g.shape[0]
    bp, tp, d = x_prompt.shape
    bs, ts, _ = x_sample.shape
    n_heads = state_wkv.shape[2]
    d_rwkv = n_heads * HEAD
    alpha = (2.0 * depth) ** 0.25
    assert ts >= CONV_W - 1

    head_id = jnp.arange(d_rwkv) // HEAD
    seg = (head_id[:, None] == head_id[None, :]).astype(BF16)

    xp = x_prompt.reshape(bp * tp, d)
    xs = x_sample.transpose(1, 0, 2).reshape(ts * bs, d)

    outs = {k: [] for k in ("wkv_p", "shift_p", "conv_p", "wkv_s", "shift_s", "conv_s")}
    for l in range(depth):
        vec = lambda a: a[l].reshape(1, -1).astype(F32)
        p = {
            "ln1_g": vec(ln1_g), "ln1_b": vec(ln1_b),
            "ffn1_wg": ffn1_wg[l].astype(BF16), "ffn1_wu": ffn1_wu[l].astype(BF16),
            "ffn1_wd": ffn1_wd[l].astype(BF16),
            "w_in": w_in[l].astype(BF16), "mu_shift": vec(mu_shift), "w0": vec(w0),
            "w_lora_up": w_lora_up[l].astype(BF16), "a0": vec(a0),
            "a_lora_up": a_lora_up[l].astype(BF16), "g_lora_up": g_lora_up[l].astype(BF16),
            "k_k": vec(k_k), "k_a": vec(k_a), "r_k": vec(r_k),
            "lnx_g": vec(lnx_g), "lnx_b": vec(lnx_b),
            "conv_w": conv_w[l].astype(F32), "w_out": w_out[l].astype(BF16),
            "ln2_g": vec(ln2_g), "ln2_b": vec(ln2_b),
            "ffn2_wg": ffn2_wg[l].astype(BF16), "ffn2_wu": ffn2_wu[l].astype(BF16),
            "ffn2_wd": ffn2_wd[l].astype(BF16),
            "ln3_g": vec(ln3_g), "ln3_b": vec(ln3_b),
        }
        wkv0_p = jnp.zeros((bp, n_heads, HEAD, HEAD), F32)
        xp, wp, sp, cp = _trunk_layer(xp, p, seg, alpha, seq_len=tp, wkv0=wkv0_p, init=None)
        init = (state_shift[l].astype(F32), state_conv[l].astype(F32).transpose(1, 0, 2))
        wkv0_s = state_wkv[l].astype(F32).transpose(1, 2, 3, 0)
        xs, wsm, ps_s, u_s = _trunk_layer(xs, p, seg, alpha, seq_len=ts, wkv0=wkv0_s, init=init)
        outs["wkv_p"].append(wp)
        outs["shift_p"].append(sp[:, 0, :])
        outs["conv_p"].append(cp)
        outs["wkv_s"].append(wsm.transpose(3, 0, 1, 2))
        outs["shift_s"].append(ps_s[(ts - 1) * bs:])
        outs["conv_s"].append(
            u_s[(ts - (CONV_W - 1)) * bs:].reshape(CONV_W - 1, bs, -1).transpose(1, 0, 2))

    y_prompt = xp.reshape(bp, tp, d)
    y_sample = xs.reshape(ts, bs, d).transpose(1, 0, 2)
    return (y_prompt, y_sample,
            jnp.stack(outs["wkv_p"]), jnp.stack(outs["shift_p"]), jnp.stack(outs["conv_p"]),
            jnp.stack(outs["wkv_s"]), jnp.stack(outs["shift_s"]), jnp.stack(outs["conv_s"]))
```

```python
import functools
import math

import jax
import jax.numpy as jnp
from jax import lax
from jax.experimental import pallas as pl
from jax.experimental.pallas import tpu as pltpu

F32 = jnp.float32
BF16 = jnp.bfloat16

HEAD = 64
LORA_W = 64
LORA_A = 64
LORA_G = 128
CONV_W = 3
LN_EPS = 1e-5
GN_EPS = 1e-5 * HEAD

VMEM_LIMIT_BYTES = 56 * 1024 * 1024
TOKEN_TILE = 512
FFN_TILE = 1024
FFN_GROUP = 256
CHUNK = 64


def _resident(shape):
    nd = len(shape)
    return pl.BlockSpec(shape, lambda *_: (0,) * nd, pipeline_mode=pl.Buffered(1))


def _layer_norm(x, g, b):
    mu = jnp.mean(x, axis=-1, keepdims=True)
    d = x - mu
    var = jnp.mean(d * d, axis=-1, keepdims=True)
    return d * lax.rsqrt(var + LN_EPS) * g + b


def _ffn_post_ln(rows_of, wg_ref, wu_ref, wd_ref, g_ref, b_ref, o_ref, alpha):
    for s in range(o_ref.shape[0] // FFN_GROUP):
        sl = slice(s * FFN_GROUP, (s + 1) * FFN_GROUP)
        xs = rows_of(sl)
        xb = xs.astype(BF16)
        g = jnp.dot(xb, wg_ref[...], preferred_element_type=F32)
        u = jnp.dot(xb, wu_ref[...], preferred_element_type=F32)
        h = (jax.nn.silu(g) * u).astype(BF16)
        y = alpha * xs + 0.5 * jnp.dot(h, wd_ref[...], preferred_element_type=F32)
        o_ref[sl, :] = _layer_norm(y, g_ref[...], b_ref[...])


def _ffn_ln_kernel(x_ref, wg_ref, wu_ref, wd_ref, g_ref, b_ref, o_ref, *, alpha):
    _ffn_post_ln(lambda sl: x_ref[sl, :], wg_ref, wu_ref, wd_ref, g_ref, b_ref, o_ref, alpha)


def _ffn_ln(x, wg, wu, wd, g, b, alpha):
    n, d = x.shape
    dff = wg.shape[1]
    tm = min(FFN_TILE, n)
    assert n % tm == 0 and tm % FFN_GROUP == 0
    row = pl.BlockSpec((tm, d), lambda i: (i, 0))
    return pl.pallas_call(
        functools.partial(_ffn_ln_kernel, alpha=alpha),
        grid=(n // tm,),
        in_specs=[row, _resident((d, dff)), _resident((d, dff)), _resident((dff, d)),
                  _resident((1, d)), _resident((1, d))],
        out_specs=row,
        out_shape=jax.ShapeDtypeStruct((n, d), F32),
        compiler_params=pltpu.CompilerParams(
            dimension_semantics=("arbitrary",), vmem_limit_bytes=VMEM_LIMIT_BYTES),
        name="ffn_ln",
    )(x, wg, wu, wd, g, b)


HDR = 8


def _shift_rows(x, head, k):
    rolled = pltpu.roll(x, k, axis=0)
    row8 = lax.broadcasted_iota(jnp.int32, (HDR, 1), 0)
    top = jnp.where(row8 < k, pltpu.roll(head, k, axis=0), rolled[:HDR])
    return jnp.concatenate([top, rolled[HDR:]], axis=0)


class _PremixRefs:
    def __init__(self, refs, tiles_per_seq, full_out):
        refs = list(refs)
        (self.x, self.win, self.mu, self.w0, self.wup, self.a0, self.aup, self.gup, self.kk,
         self.ka, self.cw, self.seg) = refs[:12]
        refs = refs[12:]
        if tiles_per_seq == 0:
            self.sh0, self.cv0 = refs[:2]
            refs = refs[2:]
        (self.r, self.nkk, self.b, self.k, self.v, self.lw, self.g, self.yc) = refs[:8]
        refs = refs[8:]
        if full_out:
            self.ps_out, self.u_out = refs[:2]
        else:
            self.shift, self.tail = refs[:2]
        self.bufs = refs[2:]


def _premix_gates(rf, buf, first, *, tiles_per_seq, d_rwkv, d_conv):
    p_shift = 3 * d_rwkv + LORA_W + LORA_A + LORA_G
    tm = rf.x.shape[0]
    ps = buf[HDR:, :p_shift]
    c_b = buf[HDR:, p_shift:p_shift + d_conv]
    u = buf[HDR:, p_shift + d_conv:p_shift + 2 * d_conv] * buf[HDR:, p_shift + 2 * d_conv:]

    if tiles_per_seq:
        head = jnp.where(first, 0.0, buf[0:HDR, :])
        head_u = head[:, p_shift + d_conv:p_shift + 2 * d_conv] * head[:, p_shift + 2 * d_conv:]
        prev = _shift_rows(ps, head[:, :p_shift], 1)
        um1 = _shift_rows(u, head_u, 1)
        um2 = _shift_rows(u, head_u, 2)
    else:
        n_seq = rf.sh0.shape[0]
        prev = jnp.concatenate([rf.sh0[...], ps[:tm - n_seq]], axis=0)
        hist = jnp.concatenate([rf.cv0[t] for t in range(CONV_W - 1)] + [u], axis=0)
        um2 = hist[:tm]
        um1 = hist[n_seq:n_seq + tm]

    cw = rf.cw[...]
    z = cw[0:1, :] * um2
    z = z + cw[1:2, :] * um1
    z = z + cw[2:3, :] * u
    rf.yc[...] = (c_b * z).astype(BF16)
    if tiles_per_seq:
        rf.shift[0] = ps[tm - 1:tm, :]
        rf.tail[0] = u[tm - (CONV_W - 1):tm, :]
    else:
        rf.ps_out[...] = ps
        rf.u_out[...] = u

    mix = ps + rf.mu[...] * (prev - ps)
    r = mix[:, 0:d_rwkv]
    k = mix[:, d_rwkv:2 * d_rwkv]
    v = mix[:, 2 * d_rwkv:3 * d_rwkv]
    o = 3 * d_rwkv
    wd = mix[:, o:o + LORA_W]
    ad = mix[:, o + LORA_W:o + LORA_W + LORA_A]
    gd = mix[:, o + LORA_W + LORA_A:p_shift]

    rf.r[...] = r
    kk = k * rf.kk[...]
    dot_in = (jnp.tanh(wd).astype(BF16), ad.astype(BF16), jax.nn.sigmoid(gd).astype(BF16),
              (kk * kk).astype(BF16))
    return k, v, kk, dot_in


def _premix_dots(rf, dot_in):
    tanh_wd, ad, sig_gd, kk_sq = dot_in
    w_pre = jnp.dot(tanh_wd, rf.wup[...], preferred_element_type=F32)
    a_pre = jnp.dot(ad, rf.aup[...], preferred_element_type=F32)
    rf.g[...] = jnp.dot(sig_gd, rf.gup[...], preferred_element_type=F32)
    ss = jnp.dot(kk_sq, rf.seg[...], preferred_element_type=F32)
    return w_pre, a_pre, ss


def _premix_finish(rf, gates, dots):
    k, v, kk, _ = gates
    w_pre, a_pre, ss = dots
    log_decay = -math.exp(-0.5) * jax.nn.sigmoid(rf.w0[...] + w_pre)
    a = jax.nn.sigmoid(rf.a0[...] + a_pre)
    kk = kk * lax.rsqrt(jnp.maximum(ss, 1e-24))
    k = k * (1.0 + (a - 1.0) * rf.ka[...])
    nkk = -kk
    b = kk * a
    rf.nkk[...] = nkk
    rf.b[...] = b
    rf.k[...] = k
    rf.v[...] = v
    rf.lw[...] = log_decay


def _premix_kernel(*refs, tiles_per_seq, d_rwkv, d_conv):
    rf = _PremixRefs(refs, tiles_per_seq, full_out=tiles_per_seq == 0)
    buf, = rf.bufs
    tm = rf.x.shape[0]
    i = pl.program_id(0)
    cfg = dict(tiles_per_seq=tiles_per_seq, d_rwkv=d_rwkv, d_conv=d_conv)

    if tiles_per_seq:
        @pl.when(i == 0)
        def _():
            buf[0:HDR, :] = jnp.zeros((HDR, buf.shape[1]), F32)

    buf[HDR:, :] = jnp.dot(rf.x[...].astype(BF16), rf.win[...], preferred_element_type=F32)
    first = (i % tiles_per_seq == 0) if tiles_per_seq else None
    gates = _premix_gates(rf, buf, first, **cfg)
    _premix_finish(rf, gates, _premix_dots(rf, gates[3]))
    if tiles_per_seq:
        buf[0:HDR, :] = buf[tm:tm + HDR, :]


def _premix(x1, p, seg, *, seq_len, init=None):
    n, d = x1.shape
    tm = TOKEN_TILE
    d_rwkv = p["w0"].shape[1]
    d_conv = p["conv_w"].shape[1]
    p_shift = 3 * d_rwkv + LORA_W + LORA_A + LORA_G
    p_total = p_shift + 3 * d_conv
    n_seq = n // seq_len
    if init is None:
        assert seq_len % tm == 0
        tiles_per_seq = seq_len // tm
    else:
        assert n == tm and n_seq % HDR == 0
        tiles_per_seq = 0
    full_out = init is not None

    def row(c):
        return pl.BlockSpec((tm, c), lambda i: (i, 0))

    in_specs = [row(d), _resident((d, p_total)), _resident((1, p_shift)),
                _resident((1, d_rwkv)), _resident((LORA_W, d_rwkv)),
                _resident((1, d_rwkv)), _resident((LORA_A, d_rwkv)),
                _resident((LORA_G, d_rwkv)), _resident((1, d_rwkv)), _resident((1, d_rwkv)),
                _resident((CONV_W, d_conv)), _resident((d_rwkv, d_rwkv))]
    args = [x1, p["w_in"], p["mu_shift"], p["w0"], p["w_lora_up"], p["a0"], p["a_lora_up"],
            p["g_lora_up"], p["k_k"], p["k_a"], p["conv_w"], seg]
    if init is not None:
        in_specs += [_resident((n_seq, p_shift)), _resident((CONV_W - 1, n_seq, d_conv))]
        args += list(init)

    out_specs = [row(d_rwkv)] * 7 + [row(d_conv)]
    out_shape = [jax.ShapeDtypeStruct((n, d_rwkv), F32)] * 7 + [
        jax.ShapeDtypeStruct((n, d_conv), BF16)]
    if full_out:
        out_specs += [row(p_shift), row(d_conv)]
        out_shape += [jax.ShapeDtypeStruct((n, p_shift), F32),
                      jax.ShapeDtypeStruct((n, d_conv), F32)]
    else:
        out_specs += [
            pl.BlockSpec((1, 1, p_shift), lambda i: (i // tiles_per_seq, 0, 0)),
            pl.BlockSpec((1, CONV_W - 1, d_conv), lambda i: (i // tiles_per_seq, 0, 0))]
        out_shape += [jax.ShapeDtypeStruct((n_seq, 1, p_shift), F32),
                      jax.ShapeDtypeStruct((n_seq, CONV_W - 1, d_conv), F32)]

    return pl.pallas_call(
        functools.partial(_premix_kernel, tiles_per_seq=tiles_per_seq, d_rwkv=d_rwkv,
                          d_conv=d_conv),
        grid=(n // tm,),
        in_specs=in_specs,
        out_specs=out_specs,
        out_shape=out_shape,
        scratch_shapes=[pltpu.VMEM((HDR + tm, p_total), F32)],
        compiler_params=pltpu.CompilerParams(
            dimension_semantics=("arbitrary",), vmem_limit_bytes=VMEM_LIMIT_BYTES),
        name="premix",
    )(*args)


def _bdot(a, b):
    return jnp.dot(a.astype(BF16), b.astype(BF16), preferred_element_type=F32)


def _bdot_nt(a, b):
    return lax.dot_general(a.astype(BF16), b.astype(BF16), (((1,), (1,)), ((), ())),
                           preferred_element_type=F32)


def _bdot_tn(a, b):
    return lax.dot_general(a.astype(BF16), b.astype(BF16), (((0,), (0,)), ((), ())),
                           preferred_element_type=F32)


def _wkv_kernel(r_ref, nkk_ref, b_ref, k_ref, v_ref, lw_ref, g_ref, s0_ref,
                rk_ref, lng_ref, lnb_ref, y_ref, sout_ref, s_scr, *, seqs, chunk, n_heads):
    c = pl.program_id(1)
    n_pairs = n_heads // 2
    c2 = 2 * chunk
    lanes = 2 * HEAD

    @pl.when(c == 0)
    def _():
        for q in range(seqs):
            for p in range(n_pairs):
                s_scr[q, p] = jnp.concatenate([s0_ref[q, 2 * p], s0_ref[q, 2 * p + 1]], axis=1)

    ii = lax.broadcasted_iota(jnp.int32, (chunk, chunk), 0)
    jj = lax.broadcasted_iota(jnp.int32, (chunk, chunk), 1)
    tri = (ii >= jj).astype(BF16)
    n_doublings = chunk.bit_length() - 1

    ri = lax.broadcasted_iota(jnp.int32, (c2, c2), 0)
    ci = lax.broadcasted_iota(jnp.int32, (c2, c2), 1)
    same = (ri // chunk) == (ci // chunk)
    strict = same & (ri % chunk > ci % chunk)
    incl = same & (ri % chunk >= ci % chunk)
    row_head = lax.broadcasted_iota(jnp.int32, (c2, lanes), 0) // chunk
    lane_head = lax.broadcasted_iota(jnp.int32, (c2, lanes), 1) // HEAD
    own = row_head == lane_head
    lo = lane_head == 0
    g_own = (lax.broadcasted_iota(jnp.int32, (lanes, lanes), 0) // HEAD
             == lax.broadcasted_iota(jnp.int32, (lanes, lanes), 1) // HEAD)
    lo_c = lax.broadcasted_iota(jnp.int32, (chunk, lanes), 1) < HEAD

    def stack(x):
        return jnp.concatenate([x, x], axis=0)

    def bd(x):
        return jnp.where(own, stack(x), 0.0)

    def own_half(x, low):
        swapped = pltpu.roll(x, HEAD, axis=1)
        rows = jnp.concatenate([x, swapped] if low else [swapped, x], axis=0)
        return jnp.where(lo if low else ~lo, rows, 0.0)

    ar_bd, bk_st, a_lo, r_lo, v_hi, bkh_bd, gam, unit = [], [], [], [], [], [], [], []
    for q in range(seqs):
        lw = lw_ref[q]
        lw_hi = lw.astype(BF16)
        lw_r1 = lw - lw_hi.astype(F32)
        lw_mid = lw_r1.astype(BF16)
        lw_lo = (lw_r1 - lw_mid.astype(F32)).astype(BF16)
        cum = (jnp.dot(tri, lw_hi, preferred_element_type=F32)
               + jnp.dot(tri, lw_mid, preferred_element_type=F32)
               + jnp.dot(tri, lw_lo, preferred_element_type=F32))
        last = cum[chunk - 1:chunk, :]
        e_in = jnp.exp(cum)
        e_out = jnp.exp(-cum)
        e_tail = jnp.exp(last - cum)
        a_t = nkk_ref[q] * jnp.exp(cum - lw)
        r_t = r_ref[q] * e_in
        b_t = b_ref[q] * e_out
        k_t = k_ref[q] * e_out
        b_h = b_ref[q] * e_tail
        k_h = k_ref[q] * e_tail
        gamma = jnp.exp(last)
        v_all = v_ref[q]
        for p in range(n_pairs):
            ps = slice(p * lanes, (p + 1) * lanes)
            ar_bd.append(jnp.concatenate([bd(a_t[:, ps]), bd(r_t[:, ps])], axis=0))
            bk_st.append(jnp.concatenate([stack(b_t[:, ps]), stack(k_t[:, ps])], axis=0))
            a_lo.append(own_half(a_t[:, ps], True))
            r_lo.append(own_half(r_t[:, ps], True))
            v_hi.append(own_half(v_all[:, ps], False))
            bkh_bd.append(jnp.concatenate([bd(b_h[:, ps]), bd(k_h[:, ps])], axis=0))
            gam.append(gamma[:, ps])
            unit.append((q, p, ps))

    units = range(seqs * n_pairs)
    m = [_bdot_nt(ar_bd[u], bk_st[u]) for u in units]
    l_ab = [jnp.where(strict, m[u][:c2, :c2], 0.0) for u in units]
    l_ak = [jnp.where(strict, m[u][:c2, c2:], 0.0) for u in units]
    l_rbk = [jnp.concatenate([jnp.where(incl, m[u][c2:, :c2], 0.0),
                              jnp.where(incl, m[u][c2:, c2:], 0.0)], axis=1) for u in units]
    y = [a_lo[u] + _bdot(l_ak[u], v_hi[u]) for u in units]
    pw = l_ab
    for step in range(n_doublings):
        if step < n_doublings - 1:
            py = [_bdot(pw[u], jnp.concatenate([y[u], pw[u]], axis=1)) for u in units]
            pw = [py[u][:, lanes:] for u in units]
            y = [y[u] + py[u][:, :lanes] for u in units]
        else:
            y = [y[u] + _bdot(pw[u], y[u]) for u in units]
    yv = [jnp.concatenate([y[u], v_hi[u]], axis=0) for u in units]
    gh = [_bdot_tn(yv[u], bkh_bd[u]) for u in units]
    lyv = [_bdot(l_rbk[u], yv[u]) for u in units]
    q_lo = [jnp.where(lo, r_lo[u] + lyv[u], 0.0) for u in units]
    q_bd = [jnp.where(own, q_lo[u] + pltpu.roll(q_lo[u], HEAD, axis=1), 0.0) for u in units]
    s = [s_scr[unit[u][0], unit[u][1]] for u in units]
    zero_s = jnp.zeros((HEAD, lanes), F32)
    o = [_bdot_nt(q_bd[u], jnp.concatenate([zero_s, s[u]], axis=0)) for u in units]
    g_bd = [jnp.where(g_own, stack(gh[u][:HEAD]), 0.0) for u in units]
    sg = [_bdot(s[u], g_bd[u]) for u in units]
    for u in units:
        s_scr[unit[u][0], unit[u][1]] = s[u] * gam[u] + sg[u] + gh[u][HEAD:]

    head_sum = g_own.astype(BF16)
    o = [o[u] + lyv[u] for u in units]
    o = [jnp.where(lo_c, pltpu.roll(o[u][:chunk], HEAD, axis=1), o[u][chunk:]) for u in units]
    mu = [_bdot(o[u], head_sum) * (1.0 / HEAD) for u in units]
    d = [o[u] - mu[u] for u in units]
    var = [_bdot(d[u] * d[u], head_sum) * (1.0 / HEAD) for u in units]
    bonus = [_bdot(r_ref[unit[u][0]][:, unit[u][2]] * k_ref[unit[u][0]][:, unit[u][2]]
                   * rk_ref[:, unit[u][2]], head_sum) for u in units]
    for q in range(seqs):
        y_pairs = []
        for p in range(n_pairs):
            u = q * n_pairs + p
            ps = unit[u][2]
            gn = d[u] * lax.rsqrt(var[u] + GN_EPS) * lng_ref[:, ps] + lnb_ref[:, ps]
            y_pairs.append((gn + bonus[u] * v_ref[q][:, ps]) * g_ref[q][:, ps])
        y_ref[q] = jnp.concatenate(y_pairs, axis=1).astype(BF16)

    @pl.when(c == pl.num_programs(1) - 1)
    def _():
        for q in range(seqs):
            for p in range(n_pairs):
                sp = s_scr[q, p]
                sout_ref[q, 2 * p] = sp[:, :HEAD]
                sout_ref[q, 2 * p + 1] = sp[:, HEAD:]


def _wkv(ops, s0, rk, lng, lnb, *, seq_len, chunk, seqs):
    n, d_rwkv = ops[0].shape
    n_seq, n_heads = s0.shape[0], s0.shape[1]
    assert n_seq % seqs == 0 and seq_len % chunk == 0
    ops = [o.reshape(n_seq, seq_len, d_rwkv) for o in ops]
    tok = pl.BlockSpec((seqs, chunk, d_rwkv), lambda q, c: (q, c, 0))
    state = pl.BlockSpec((seqs, n_heads, HEAD, HEAD), lambda q, c: (q, 0, 0, 0))
    y, s_new = pl.pallas_call(
        functools.partial(_wkv_kernel, seqs=seqs, chunk=chunk, n_heads=n_heads),
        grid=(n_seq // seqs, seq_len // chunk),
        in_specs=[tok] * 7 + [state] + [pl.BlockSpec((1, d_rwkv), lambda q, c: (0, 0))] * 3,
        out_specs=[tok, state],
        out_shape=[jax.ShapeDtypeStruct((n_seq, seq_len, d_rwkv), BF16),
                   jax.ShapeDtypeStruct(s0.shape, F32)],
        scratch_shapes=[pltpu.VMEM((seqs, n_heads // 2, HEAD, 2 * HEAD), F32)],
        compiler_params=pltpu.CompilerParams(
            dimension_semantics=("arbitrary", "arbitrary"),
            vmem_limit_bytes=VMEM_LIMIT_BYTES),
        name="wkv",
    )(*ops, s0, rk, lng, lnb)
    return y.reshape(n, d_rwkv), s_new


def _wkv_steps_kernel(r_ref, nkk_ref, b_ref, k_ref, v_ref, lw_ref, g_ref, s0_ref,
                      rk_ref, lng_ref, lnb_ref, y_ref, sout_ref, vt_scr, o_scr, *, n_t):
    n_seq = s0_ref.shape[-1]

    def feature_major(ref, t):
        return ref[t * n_seq:(t + 1) * n_seq, :].T

    for t in range(n_t):
        src = s0_ref if t == 0 else sout_ref
        nkk = feature_major(nkk_ref, t)
        beta = feature_major(b_ref, t)
        key = feature_major(k_ref, t)
        rec = feature_major(r_ref, t)
        decay = jnp.exp(feature_major(lw_ref, t))
        vt_scr[...] = feature_major(v_ref, t)
        for h in range(2):
            hs = slice(h * HEAD, (h + 1) * HEAD)
            nkk_h, beta_h, key_h, rec_h, decay_h = nkk[hs], beta[hs], key[hs], rec[hs], decay[hs]

            def one_row(i, carry):
                s = src[h, i]
                sa = jnp.sum(s * nkk_h, axis=0, keepdims=True)
                v_i = vt_scr[pl.ds(h * HEAD + i, 1), :]
                s = s * decay_h + sa * beta_h + v_i * key_h
                sout_ref[h, i] = s
                o_scr[pl.ds(h * HEAD + i, 1), :] = jnp.sum(s * rec_h, axis=0, keepdims=True)
                return carry

            lax.fori_loop(0, HEAD, one_row, 0, unroll=4)

        v_t = vt_scr[...]
        bonus_prod = rec * key * rk_ref[...]
        ys = []
        for h in range(2):
            hs = slice(h * HEAD, (h + 1) * HEAD)
            o = o_scr[hs, :]
            mu = jnp.mean(o, axis=0, keepdims=True)
            d = o - mu
            var = jnp.mean(d * d, axis=0, keepdims=True)
            gn = d * lax.rsqrt(var + GN_EPS) * lng_ref[hs, :] + lnb_ref[hs, :]
            bonus = jnp.sum(bonus_prod[hs], axis=0, keepdims=True)
            ys.append(gn + bonus * v_t[hs])
        y_t = jnp.concatenate(ys, axis=0) * feature_major(g_ref, t)
        y_ref[t * n_seq:(t + 1) * n_seq, :] = y_t.T.astype(y_ref.dtype)


def _wkv_steps(ops, s0, rk, lng, lnb, *, n_t):
    n, d_rwkv = ops[0].shape
    n_heads, n_seq = s0.shape[0], s0.shape[-1]
    lanes = 2 * HEAD
    assert n == n_t * n_seq and n_seq == lanes

    def bcast(a):
        return jnp.broadcast_to(a.reshape(d_rwkv, 1), (d_rwkv, n_seq))

    tok = pl.BlockSpec((n, lanes), lambda p: (0, p))
    state = pl.BlockSpec((2, HEAD, HEAD, n_seq), lambda p: (p, 0, 0, 0))
    par = pl.BlockSpec((lanes, n_seq), lambda p: (p, 0))
    return pl.pallas_call(
        functools.partial(_wkv_steps_kernel, n_t=n_t),
        grid=(n_heads // 2,),
        in_specs=[tok] * 7 + [state] + [par] * 3,
        out_specs=[tok, state],
        out_shape=[jax.ShapeDtypeStruct((n, d_rwkv), BF16),
                   jax.ShapeDtypeStruct(s0.shape, F32)],
        scratch_shapes=[pltpu.VMEM((lanes, n_seq), F32), pltpu.VMEM((lanes, n_seq), F32)],
        compiler_params=pltpu.CompilerParams(
            dimension_semantics=("arbitrary",), vmem_limit_bytes=VMEM_LIMIT_BYTES),
        name="wkv_steps",
    )(*ops, s0, bcast(rk), bcast(lng), bcast(lnb))


def _out_ffn_kernel(x_ref, yr_ref, yc_ref, wo_r_ref, wo_c_ref, g2_ref, b2_ref,
                    wg_ref, wu_ref, wd_ref, g3_ref, b3_ref, o_ref, *, alpha):
    mix = jnp.dot(yr_ref[...].astype(BF16), wo_r_ref[...], preferred_element_type=F32)
    mix = mix + jnp.dot(yc_ref[...], wo_c_ref[...], preferred_element_type=F32)

    def mixed_rows(sl):
        return _layer_norm(alpha * x_ref[sl, :] + mix[sl], g2_ref[...], b2_ref[...])

    _ffn_post_ln(mixed_rows, wg_ref, wu_ref, wd_ref, g3_ref, b3_ref, o_ref, alpha)


def _out_ffn(x1, y_rwkv, y_conv, wo_r, wo_c, g2, b2, wg, wu, wd, g3, b3, alpha):
    n, d = x1.shape
    dff = wg.shape[1]
    d_rwkv, d_conv = y_rwkv.shape[1], y_conv.shape[1]
    tm = min(FFN_TILE, n)
    assert n % tm == 0 and tm % FFN_GROUP == 0

    def row(c):
        return pl.BlockSpec((tm, c), lambda i: (i, 0))

    return pl.pallas_call(
        functools.partial(_out_ffn_kernel, alpha=alpha),
        grid=(n // tm,),
        in_specs=[row(d), row(d_rwkv), row(d_conv), _resident((d_rwkv, d)), _resident((d_conv, d)),
                  _resident((1, d)), _resident((1, d)),
                  _resident((d, dff)), _resident((d, dff)), _resident((dff, d)),
                  _resident((1, d)), _resident((1, d))],
        out_specs=row(d),
        out_shape=jax.ShapeDtypeStruct((n, d), F32),
        compiler_params=pltpu.CompilerParams(
            dimension_semantics=("arbitrary",), vmem_limit_bytes=VMEM_LIMIT_BYTES),
        name="out_ffn",
    )(x1, y_rwkv, y_conv, wo_r, wo_c, g2, b2, wg, wu, wd, g3, b3)


def _trunk_layer(x, p, seg, alpha, *, seq_len, wkv0, init):
    x1 = _ffn_ln(x, p["ffn1_wg"], p["ffn1_wu"], p["ffn1_wd"], p["ln1_g"], p["ln1_b"], alpha)
    pm = _premix(x1, p, seg, seq_len=seq_len, init=init)
    ops, y_conv, st_a, st_b = pm[:7], pm[7], pm[8], pm[9]
    if init is None:
        y_rwkv, wkv_new = _wkv(ops, wkv0, p["r_k"], p["lnx_g"], p["lnx_b"],
                               seq_len=seq_len, chunk=CHUNK, seqs=wkv0.shape[0])
    else:
        y_rwkv, wkv_new = _wkv_steps(ops, wkv0, p["r_k"], p["lnx_g"], p["lnx_b"], n_t=seq_len)
    d_rwkv = y_rwkv.shape[1]
    y = _out_ffn(x1, y_rwkv, y_conv, p["w_out"][:d_rwkv], p["w_out"][d_rwkv:],
                 p["ln2_g"], p["ln2_b"], p["ffn2_wg"], p["ffn2_wu"], p["ffn2_wd"],
                 p["ln3_g"], p["ln3_b"], alpha)
    return y, wkv_new, st_a, st_b


def kernel(x_prompt, x_sample, state_wkv, state_shift, state_conv, ln1_g, ln1_b, ffn1_wg, ffn1_wu, ffn1_wd, w_in, mu_shift, w0, w_lora_up, a0, a_lora_up, g_lora_up, k_k, k_a, r_k, lnx_g, lnx_b, conv_w, w_out, ln2_g, ln2_b, ffn2_wg, ffn2_wu, ffn2_wd, ln3_g, ln3_b):
    depth = ln1_g.shape[0]
    bp, tp, d = x_prompt.shape
    bs, ts, _ = x_sample.shape
    n_heads = state_wkv.shape[2]
    d_rwkv = n_heads * HEAD
    alpha = (2.0 * depth) ** 0.25
    assert ts >= CONV_W - 1

    head_id = jnp.arange(d_rwkv) // HEAD
    seg = (head_id[:, None] == head_id[None, :]).astype(BF16)

    xp = x_prompt.reshape(bp * tp, d)
    xs = x_sample.transpose(1, 0, 2).reshape(ts * bs, d)

    outs = {k: [] for k in ("wkv_p", "shift_p", "conv_p", "wkv_s", "shift_s", "conv_s")}
    for l in range(depth):
        vec = lambda a: a[l].reshape(1, -1).astype(F32)
        p = {
            "ln1_g": vec(ln1_g), "ln1_b": vec(ln1_b),
            "ffn1_wg": ffn1_wg[l].astype(BF16), "ffn1_wu": ffn1_wu[l].astype(BF16),
            "ffn1_wd": ffn1_wd[l].astype(BF16),
            "w_in": w_in[l].astype(BF16), "mu_shift": vec(mu_shift), "w0": vec(w0),
            "w_lora_up": w_lora_up[l].astype(BF16), "a0": vec(a0),
            "a_lora_up": a_lora_up[l].astype(BF16), "g_lora_up": g_lora_up[l].astype(BF16),
            "k_k": vec(k_k), "k_a": vec(k_a), "r_k": vec(r_k),
            "lnx_g": vec(lnx_g), "lnx_b": vec(lnx_b),
            "conv_w": conv_w[l].astype(F32), "w_out": w_out[l].astype(BF16),
            "ln2_g": vec(ln2_g), "ln2_b": vec(ln2_b),
            "ffn2_wg": ffn2_wg[l].astype(BF16), "ffn2_wu": ffn2_wu[l].astype(BF16),
            "ffn2_wd": ffn2_wd[l].astype(BF16),
            "ln3_g": vec(ln3_g), "ln3_b": vec(ln3_b),
        }
        wkv0_p = jnp.zeros((bp, n_heads, HEAD, HEAD), F32)
        xp, wp, sp, cp = _trunk_layer(xp, p, seg, alpha, seq_len=tp, wkv0=wkv0_p, init=None)
        init = (state_shift[l].astype(F32), state_conv[l].astype(F32).transpose(1, 0, 2))
        wkv0_s = state_wkv[l].astype(F32).transpose(1, 2, 3, 0)
        xs, wsm, ps_s, u_s = _trunk_layer(xs, p, seg, alpha, seq_len=ts, wkv0=wkv0_s, init=init)
        outs["wkv_p"].append(wp)
        outs["shift_p"].append(sp[:, 0, :])
        outs["conv_p"].append(cp)
        outs["wkv_s"].append(wsm.transpose(3, 0, 1, 2))
        outs["shift_s"].append(ps_s[(ts - 1) * bs:])
        outs["conv_s"].append(
            u_s[(ts - (CONV_W - 1)) * bs:].reshape(CONV_W - 1, bs, -1).transpose(1, 0, 2))

    y_prompt = xp.reshape(bp, tp, d)
    y_sample = xs.reshape(ts, bs, d).transpose(1, 0, 2)
    return (y_prompt, y_sample,
            jnp.stack(outs["wkv_p"]), jnp.stack(outs["shift_p"]), jnp.stack(outs["conv_p"]),
            jnp.stack(outs["wkv_s"]), jnp.stack(outs["shift_s"]), jnp.stack(outs["conv_s"]))
```

```python
import functools
import math

import jax
import jax.numpy as jnp
from jax import lax
from jax.experimental import pallas as pl
from jax.experimental.pallas import tpu as pltpu

F32 = jnp.float32
BF16 = jnp.bfloat16

HEAD = 64
LORA_W = 64
LORA_A = 64
LORA_G = 128
CONV_W = 3
LN_EPS = 1e-5
GN_EPS = 1e-5 * HEAD

VMEM_LIMIT_BYTES = 56 * 1024 * 1024
TOKEN_TILE = 512
FFN_TILE = 1024
FFN_GROUP = 256
CHUNK = 64


def _resident(shape):
    nd = len(shape)
    return pl.BlockSpec(shape, lambda *_: (0,) * nd, pipeline_mode=pl.Buffered(1))


def _layer_norm(x, g, b):
    mu = jnp.mean(x, axis=-1, keepdims=True)
    d = x - mu
    var = jnp.mean(d * d, axis=-1, keepdims=True)
    return d * lax.rsqrt(var + LN_EPS) * g + b


def _ffn_post_ln(rows_of, wg_ref, wu_ref, wd_ref, g_ref, b_ref, o_ref, alpha):
    for s in range(o_ref.shape[0] // FFN_GROUP):
        sl = slice(s * FFN_GROUP, (s + 1) * FFN_GROUP)
        xs = rows_of(sl)
        xb = xs.astype(BF16)
        g = jnp.dot(xb, wg_ref[...], preferred_element_type=F32)
        u = jnp.dot(xb, wu_ref[...], preferred_element_type=F32)
        h = (jax.nn.silu(g) * u).astype(BF16)
        y = alpha * xs + 0.5 * jnp.dot(h, wd_ref[...], preferred_element_type=F32)
        o_ref[sl, :] = _layer_norm(y, g_ref[...], b_ref[...])


def _two_group_grid(n_long, n_short):
    tm = FFN_TILE
    assert n_long % tm == 0 and n_short % FFN_GROUP == 0
    n_tiles = n_long // tm

    def long_rows(c):
        return pl.BlockSpec((tm, c), lambda i: (jnp.minimum(i, n_tiles - 1), 0))

    def short_rows(c, single_buffer=True):
        mode = dict(pipeline_mode=pl.Buffered(1)) if single_buffer else {}
        return pl.BlockSpec((n_short, c), lambda i: (0, 0), **mode)

    return (n_tiles + 1,), long_rows, short_rows


def _ffn_ln_kernel(xa_ref, xb_ref, wg_ref, wu_ref, wd_ref, g_ref, b_ref, oa_ref, ob_ref, *, alpha):
    last = pl.num_programs(0) - 1
    for x_ref, o_ref, cond in ((xa_ref, oa_ref, pl.program_id(0) < last),
                               (xb_ref, ob_ref, pl.program_id(0) == last)):
        @pl.when(cond)
        def _(x_ref=x_ref, o_ref=o_ref):
            _ffn_post_ln(lambda sl: x_ref[sl, :], wg_ref, wu_ref, wd_ref, g_ref, b_ref,
                         o_ref, alpha)


def _ffn_ln(xa, xb, wg, wu, wd, g, b, alpha):
    d = xa.shape[1]
    dff = wg.shape[1]
    grid, long_rows, short_rows = _two_group_grid(xa.shape[0], xb.shape[0])
    return pl.pallas_call(
        functools.partial(_ffn_ln_kernel, alpha=alpha),
        grid=grid,
        in_specs=[long_rows(d), short_rows(d), _resident((d, dff)), _resident((d, dff)),
                  _resident((dff, d)), _resident((1, d)), _resident((1, d))],
        out_specs=[long_rows(d), short_rows(d, single_buffer=False)],
        out_shape=[jax.ShapeDtypeStruct(xa.shape, F32), jax.ShapeDtypeStruct(xb.shape, F32)],
        compiler_params=pltpu.CompilerParams(
            dimension_semantics=("arbitrary",), vmem_limit_bytes=VMEM_LIMIT_BYTES),
        name="ffn_ln",
    )(xa, xb, wg, wu, wd, g, b)


HDR = 8


def _shift_rows(x, head, k):
    rolled = pltpu.roll(x, k, axis=0)
    row8 = lax.broadcasted_iota(jnp.int32, (HDR, 1), 0)
    top = jnp.where(row8 < k, pltpu.roll(head, k, axis=0), rolled[:HDR])
    return jnp.concatenate([top, rolled[HDR:]], axis=0)


class _PremixRefs:
    def __init__(self, refs, tiles_per_seq, full_out):
        refs = list(refs)
        (self.x, self.win, self.mu, self.w0, self.wup, self.a0, self.aup, self.gup, self.kk,
         self.ka, self.cw, self.seg) = refs[:12]
        refs = refs[12:]
        if tiles_per_seq == 0:
            self.sh0, self.cv0 = refs[:2]
            refs = refs[2:]
        (self.r, self.nkk, self.b, self.k, self.v, self.lw, self.g, self.yc) = refs[:8]
        refs = refs[8:]
        if full_out:
            self.ps_out, self.u_out = refs[:2]
        else:
            self.shift, self.tail = refs[:2]
        self.bufs = refs[2:]


def _premix_gates(rf, buf, first, *, tiles_per_seq, d_rwkv, d_conv):
    p_shift = 3 * d_rwkv + LORA_W + LORA_A + LORA_G
    tm = rf.x.shape[0]
    ps = buf[HDR:, :p_shift]
    c_b = buf[HDR:, p_shift:p_shift + d_conv]
    u = buf[HDR:, p_shift + d_conv:p_shift + 2 * d_conv] * buf[HDR:, p_shift + 2 * d_conv:]

    if tiles_per_seq:
        head = jnp.where(first, 0.0, buf[0:HDR, :])
        head_u = head[:, p_shift + d_conv:p_shift + 2 * d_conv] * head[:, p_shift + 2 * d_conv:]
        prev = _shift_rows(ps, head[:, :p_shift], 1)
        um1 = _shift_rows(u, head_u, 1)
        um2 = _shift_rows(u, head_u, 2)
    else:
        n_seq = rf.sh0.shape[0]
        prev = jnp.concatenate([rf.sh0[...], ps[:tm - n_seq]], axis=0)
        hist = jnp.concatenate([rf.cv0[t] for t in range(CONV_W - 1)] + [u], axis=0)
        um2 = hist[:tm]
        um1 = hist[n_seq:n_seq + tm]

    cw = rf.cw[...]
    z = cw[0:1, :] * um2
    z = z + cw[1:2, :] * um1
    z = z + cw[2:3, :] * u
    rf.yc[...] = (c_b * z).astype(BF16)
    if tiles_per_seq:
        rf.shift[0] = ps[tm - 1:tm, :]
        rf.tail[0] = u[tm - (CONV_W - 1):tm, :]
    else:
        rf.ps_out[...] = ps
        rf.u_out[...] = u

    mix = ps + rf.mu[...] * (prev - ps)
    r = mix[:, 0:d_rwkv]
    k = mix[:, d_rwkv:2 * d_rwkv]
    v = mix[:, 2 * d_rwkv:3 * d_rwkv]
    o = 3 * d_rwkv
    wd = mix[:, o:o + LORA_W]
    ad = mix[:, o + LORA_W:o + LORA_W + LORA_A]
    gd = mix[:, o + LORA_W + LORA_A:p_shift]

    rf.r[...] = r
    kk = k * rf.kk[...]
    dot_in = (jnp.tanh(wd).astype(BF16), ad.astype(BF16), jax.nn.sigmoid(gd).astype(BF16),
              (kk * kk).astype(BF16))
    return k, v, kk, dot_in


def _premix_dots(rf, dot_in):
    tanh_wd, ad, sig_gd, kk_sq = dot_in
    w_pre = jnp.dot(tanh_wd, rf.wup[...], preferred_element_type=F32)
    a_pre = jnp.dot(ad, rf.aup[...], preferred_element_type=F32)
    rf.g[...] = jnp.dot(sig_gd, rf.gup[...], preferred_element_type=F32)
    ss = jnp.dot(kk_sq, rf.seg[...], preferred_element_type=F32)
    return w_pre, a_pre, ss


def _premix_finish(rf, gates, dots):
    k, v, kk, _ = gates
    w_pre, a_pre, ss = dots
    log_decay = -math.exp(-0.5) * jax.nn.sigmoid(rf.w0[...] + w_pre)
    a = jax.nn.sigmoid(rf.a0[...] + a_pre)
    kk = kk * lax.rsqrt(jnp.maximum(ss, 1e-24))
    k = k * (1.0 + (a - 1.0) * rf.ka[...])
    nkk = -kk
    b = kk * a
    rf.nkk[...] = nkk
    rf.b[...] = b
    rf.k[...] = k
    rf.v[...] = v
    rf.lw[...] = log_decay


def _premix_kernel(*refs, tiles_per_seq, d_rwkv, d_conv):
    rf = _PremixRefs(refs, tiles_per_seq, full_out=tiles_per_seq == 0)
    buf, = rf.bufs
    tm = rf.x.shape[0]
    i = pl.program_id(0)
    cfg = dict(tiles_per_seq=tiles_per_seq, d_rwkv=d_rwkv, d_conv=d_conv)

    if tiles_per_seq:
        @pl.when(i == 0)
        def _():
            buf[0:HDR, :] = jnp.zeros((HDR, buf.shape[1]), F32)

    buf[HDR:, :] = jnp.dot(rf.x[...].astype(BF16), rf.win[...], preferred_element_type=F32)
    first = (i % tiles_per_seq == 0) if tiles_per_seq else None
    gates = _premix_gates(rf, buf, first, **cfg)
    _premix_finish(rf, gates, _premix_dots(rf, gates[3]))
    if tiles_per_seq:
        buf[0:HDR, :] = buf[tm:tm + HDR, :]


def _premix(x1, p, seg, *, seq_len, init=None):
    n, d = x1.shape
    tm = TOKEN_TILE
    d_rwkv = p["w0"].shape[1]
    d_conv = p["conv_w"].shape[1]
    p_shift = 3 * d_rwkv + LORA_W + LORA_A + LORA_G
    p_total = p_shift + 3 * d_conv
    n_seq = n // seq_len
    if init is None:
        assert seq_len % tm == 0
        tiles_per_seq = seq_len // tm
    else:
        assert n == tm and n_seq % HDR == 0
        tiles_per_seq = 0
    full_out = init is not None

    def row(c):
        return pl.BlockSpec((tm, c), lambda i: (i, 0))

    in_specs = [row(d), _resident((d, p_total)), _resident((1, p_shift)),
                _resident((1, d_rwkv)), _resident((LORA_W, d_rwkv)),
                _resident((1, d_rwkv)), _resident((LORA_A, d_rwkv)),
                _resident((LORA_G, d_rwkv)), _resident((1, d_rwkv)), _resident((1, d_rwkv)),
                _resident((CONV_W, d_conv)), _resident((d_rwkv, d_rwkv))]
    args = [x1, p["w_in"], p["mu_shift"], p["w0"], p["w_lora_up"], p["a0"], p["a_lora_up"],
            p["g_lora_up"], p["k_k"], p["k_a"], p["conv_w"], seg]
    if init is not None:
        in_specs += [_resident((n_seq, p_shift)), _resident((CONV_W - 1, n_seq, d_conv))]
        args += list(init)

    out_specs = [row(d_rwkv)] * 7 + [row(d_conv)]
    out_shape = [jax.ShapeDtypeStruct((n, d_rwkv), F32)] * 7 + [
        jax.ShapeDtypeStruct((n, d_conv), BF16)]
    if full_out:
        out_specs += [row(p_shift), row(d_conv)]
        out_shape += [jax.ShapeDtypeStruct((n, p_shift), F32),
                      jax.ShapeDtypeStruct((n, d_conv), F32)]
    else:
        out_specs += [
            pl.BlockSpec((1, 1, p_shift), lambda i: (i // tiles_per_seq, 0, 0)),
            pl.BlockSpec((1, CONV_W - 1, d_conv), lambda i: (i // tiles_per_seq, 0, 0))]
        out_shape += [jax.ShapeDtypeStruct((n_seq, 1, p_shift), F32),
                      jax.ShapeDtypeStruct((n_seq, CONV_W - 1, d_conv), F32)]

    return pl.pallas_call(
        functools.partial(_premix_kernel, tiles_per_seq=tiles_per_seq, d_rwkv=d_rwkv,
                          d_conv=d_conv),
        grid=(n // tm,),
        in_specs=in_specs,
        out_specs=out_specs,
        out_shape=out_shape,
        scratch_shapes=[pltpu.VMEM((HDR + tm, p_total), F32)],
        compiler_params=pltpu.CompilerParams(
            dimension_semantics=("arbitrary",), vmem_limit_bytes=VMEM_LIMIT_BYTES),
        name="premix",
    )(*args)


def _bdot(a, b):
    return jnp.dot(a.astype(BF16), b.astype(BF16), preferred_element_type=F32)


def _bdot_nt(a, b):
    return lax.dot_general(a.astype(BF16), b.astype(BF16), (((1,), (1,)), ((), ())),
                           preferred_element_type=F32)


def _bdot_tn(a, b):
    return lax.dot_general(a.astype(BF16), b.astype(BF16), (((0,), (0,)), ((), ())),
                           preferred_element_type=F32)


def _wkv_kernel(r_ref, nkk_ref, b_ref, k_ref, v_ref, lw_ref, g_ref, s0_ref,
                rk_ref, lng_ref, lnb_ref, y_ref, sout_ref, s_scr, *, seqs, chunk, n_heads):
    c = pl.program_id(1)
    n_pairs = n_heads // 2
    c2 = 2 * chunk
    lanes = 2 * HEAD

    @pl.when(c == 0)
    def _():
        for q in range(seqs):
            for p in range(n_pairs):
                s_scr[q, p] = jnp.concatenate([s0_ref[q, 2 * p], s0_ref[q, 2 * p + 1]], axis=1)

    ii = lax.broadcasted_iota(jnp.int32, (chunk, chunk), 0)
    jj = lax.broadcasted_iota(jnp.int32, (chunk, chunk), 1)
    tri = (ii >= jj).astype(BF16)
    n_doublings = chunk.bit_length() - 1

    ri = lax.broadcasted_iota(jnp.int32, (c2, c2), 0)
    ci = lax.broadcasted_iota(jnp.int32, (c2, c2), 1)
    same = (ri // chunk) == (ci // chunk)
    strict = same & (ri % chunk > ci % chunk)
    incl = same & (ri % chunk >= ci % chunk)
    row_head = lax.broadcasted_iota(jnp.int32, (c2, lanes), 0) // chunk
    lane_head = lax.broadcasted_iota(jnp.int32, (c2, lanes), 1) // HEAD
    own = row_head == lane_head
    lo = lane_head == 0
    g_own = (lax.broadcasted_iota(jnp.int32, (lanes, lanes), 0) // HEAD
             == lax.broadcasted_iota(jnp.int32, (lanes, lanes), 1) // HEAD)
    lo_c = lax.broadcasted_iota(jnp.int32, (chunk, lanes), 1) < HEAD

    def stack(x):
        return jnp.concatenate([x, x], axis=0)

    def bd(x):
        return jnp.where(own, stack(x), 0.0)

    def own_half(x, low):
        swapped = pltpu.roll(x, HEAD, axis=1)
        rows = jnp.concatenate([x, swapped] if low else [swapped, x], axis=0)
        return jnp.where(lo if low else ~lo, rows, 0.0)

    ar_bd, bk_st, a_lo, r_lo, v_hi, bkh_bd, gam, unit = [], [], [], [], [], [], [], []
    for q in range(seqs):
        lw = lw_ref[q]
        lw_hi = lw.astype(BF16)
        lw_r1 = lw - lw_hi.astype(F32)
        lw_mid = lw_r1.astype(BF16)
        lw_lo = (lw_r1 - lw_mid.astype(F32)).astype(BF16)
        cum = (jnp.dot(tri, lw_hi, preferred_element_type=F32)
               + jnp.dot(tri, lw_mid, preferred_element_type=F32)
               + jnp.dot(tri, lw_lo, preferred_element_type=F32))
        last = cum[chunk - 1:chunk, :]
        e_in = jnp.exp(cum)
        e_out = jnp.exp(-cum)
        e_tail = jnp.exp(last - cum)
        a_t = nkk_ref[q] * jnp.exp(cum - lw)
        r_t = r_ref[q] * e_in
        b_t = b_ref[q] * e_out
        k_t = k_ref[q] * e_out
        b_h = b_ref[q] * e_tail
        k_h = k_ref[q] * e_tail
        gamma = jnp.exp(last)
        v_all = v_ref[q]
        for p in range(n_pairs):
            ps = slice(p * lanes, (p + 1) * lanes)
            ar_bd.append(jnp.concatenate([bd(a_t[:, ps]), bd(r_t[:, ps])], axis=0))
            bk_st.append(jnp.concatenate([stack(b_t[:, ps]), stack(k_t[:, ps])], axis=0))
            a_lo.append(own_half(a_t[:, ps], True))
            r_lo.append(own_half(r_t[:, ps], True))
            v_hi.append(own_half(v_all[:, ps], False))
            bkh_bd.append(jnp.concatenate([bd(b_h[:, ps]), bd(k_h[:, ps])], axis=0))
            gam.append(gamma[:, ps])
            unit.append((q, p, ps))

    units = range(seqs * n_pairs)
    m = [_bdot_nt(ar_bd[u], bk_st[u]) for u in units]
    l_ab = [jnp.where(strict, m[u][:c2, :c2], 0.0) for u in units]
    l_ak = [jnp.where(strict, m[u][:c2, c2:], 0.0) for u in units]
    l_rbk = [jnp.concatenate([jnp.where(incl, m[u][c2:, :c2], 0.0),
                              jnp.where(incl, m[u][c2:, c2:], 0.0)], axis=1) for u in units]
    y = [a_lo[u] + _bdot(l_ak[u], v_hi[u]) for u in units]
    pw = l_ab
    for step in range(n_doublings):
        if step < n_doublings - 1:
            py = [_bdot(pw[u], jnp.concatenate([y[u], pw[u]], axis=1)) for u in units]
            pw = [py[u][:, lanes:] for u in units]
            y = [y[u] + py[u][:, :lanes] for u in units]
        else:
            y = [y[u] + _bdot(pw[u], y[u]) for u in units]
    yv = [jnp.concatenate([y[u], v_hi[u]], axis=0) for u in units]
    gh = [_bdot_tn(yv[u], bkh_bd[u]) for u in units]
    lyv = [_bdot(l_rbk[u], yv[u]) for u in units]
    q_lo = [jnp.where(lo, r_lo[u] + lyv[u], 0.0) for u in units]
    q_bd = [jnp.where(own, q_lo[u] + pltpu.roll(q_lo[u], HEAD, axis=1), 0.0) for u in units]
    s = [s_scr[unit[u][0], unit[u][1]] for u in units]
    zero_s = jnp.zeros((HEAD, lanes), F32)
    o = [_bdot_nt(q_bd[u], jnp.concatenate([zero_s, s[u]], axis=0)) for u in units]
    g_bd = [jnp.where(g_own, stack(gh[u][:HEAD]), 0.0) for u in units]
    sg = [_bdot(s[u], g_bd[u]) for u in units]
    for u in units:
        s_scr[unit[u][0], unit[u][1]] = s[u] * gam[u] + sg[u] + gh[u][HEAD:]

    head_sum = g_own.astype(BF16)
    o = [o[u] + lyv[u] for u in units]
    o = [jnp.where(lo_c, pltpu.roll(o[u][:chunk], HEAD, axis=1), o[u][chunk:]) for u in units]
    mu = [_bdot(o[u], head_sum) * (1.0 / HEAD) for u in units]
    d = [o[u] - mu[u] for u in units]
    var = [_bdot(d[u] * d[u], head_sum) * (1.0 / HEAD) for u in units]
    bonus = [_bdot(r_ref[unit[u][0]][:, unit[u][2]] * k_ref[unit[u][0]][:, unit[u][2]]
                   * rk_ref[:, unit[u][2]], head_sum) for u in units]
    for q in range(seqs):
        y_pairs = []
        for p in range(n_pairs):
            u = q * n_pairs + p
            ps = unit[u][2]
            gn = d[u] * lax.rsqrt(var[u] + GN_EPS) * lng_ref[:, ps] + lnb_ref[:, ps]
            y_pairs.append((gn + bonus[u] * v_ref[q][:, ps]) * g_ref[q][:, ps])
        y_ref[q] = jnp.concatenate(y_pairs, axis=1).astype(BF16)

    @pl.when(c == pl.num_programs(1) - 1)
    def _():
        for q in range(seqs):
            for p in range(n_pairs):
                sp = s_scr[q, p]
                sout_ref[q, 2 * p] = sp[:, :HEAD]
                sout_ref[q, 2 * p + 1] = sp[:, HEAD:]


def _wkv(ops, s0, rk, lng, lnb, *, seq_len, chunk, seqs):
    n, d_rwkv = ops[0].shape
    n_seq, n_heads = s0.shape[0], s0.shape[1]
    assert n_seq % seqs == 0 and seq_len % chunk == 0
    ops = [o.reshape(n_seq, seq_len, d_rwkv) for o in ops]
    tok = pl.BlockSpec((seqs, chunk, d_rwkv), lambda q, c: (q, c, 0))
    state = pl.BlockSpec((seqs, n_heads, HEAD, HEAD), lambda q, c: (q, 0, 0, 0))
    y, s_new = pl.pallas_call(
        functools.partial(_wkv_kernel, seqs=seqs, chunk=chunk, n_heads=n_heads),
        grid=(n_seq // seqs, seq_len // chunk),
        in_specs=[tok] * 7 + [state] + [pl.BlockSpec((1, d_rwkv), lambda q, c: (0, 0))] * 3,
        out_specs=[tok, state],
        out_shape=[jax.ShapeDtypeStruct((n_seq, seq_len, d_rwkv), BF16),
                   jax.ShapeDtypeStruct(s0.shape, F32)],
        scratch_shapes=[pltpu.VMEM((seqs, n_heads // 2, HEAD, 2 * HEAD), F32)],
        compiler_params=pltpu.CompilerParams(
            dimension_semantics=("arbitrary", "arbitrary"),
            vmem_limit_bytes=VMEM_LIMIT_BYTES),
        name="wkv",
    )(*ops, s0, rk, lng, lnb)
    return y.reshape(n, d_rwkv), s_new


def _wkv_steps_kernel(r_ref, nkk_ref, b_ref, k_ref, v_ref, lw_ref, g_ref, s0_ref,
                      rk_ref, lng_ref, lnb_ref, y_ref, sout_ref, vt_scr, o_scr, *, n_t):
    n_seq = s0_ref.shape[-1]

    def feature_major(ref, t):
        return ref[t * n_seq:(t + 1) * n_seq, :].T

    for t in range(n_t):
        src = s0_ref if t == 0 else sout_ref
        nkk = feature_major(nkk_ref, t)
        beta = feature_major(b_ref, t)
        key = feature_major(k_ref, t)
        rec = feature_major(r_ref, t)
        decay = jnp.exp(feature_major(lw_ref, t))
        vt_scr[...] = feature_major(v_ref, t)
        for h in range(2):
            hs = slice(h * HEAD, (h + 1) * HEAD)
            nkk_h, beta_h, key_h, rec_h, decay_h = nkk[hs], beta[hs], key[hs], rec[hs], decay[hs]

            def one_row(i, carry):
                s = src[h, i]
                sa = jnp.sum(s * nkk_h, axis=0, keepdims=True)
                v_i = vt_scr[pl.ds(h * HEAD + i, 1), :]
                s = s * decay_h + sa * beta_h + v_i * key_h
                sout_ref[h, i] = s
                o_scr[pl.ds(h * HEAD + i, 1), :] = jnp.sum(s * rec_h, axis=0, keepdims=True)
                return carry

            lax.fori_loop(0, HEAD, one_row, 0, unroll=4)

        v_t = vt_scr[...]
        bonus_prod = rec * key * rk_ref[...]
        ys = []
        for h in range(2):
            hs = slice(h * HEAD, (h + 1) * HEAD)
            o = o_scr[hs, :]
            mu = jnp.mean(o, axis=0, keepdims=True)
            d = o - mu
            var = jnp.mean(d * d, axis=0, keepdims=True)
            gn = d * lax.rsqrt(var + GN_EPS) * lng_ref[hs, :] + lnb_ref[hs, :]
            bonus = jnp.sum(bonus_prod[hs], axis=0, keepdims=True)
            ys.append(gn + bonus * v_t[hs])
        y_t = jnp.concatenate(ys, axis=0) * feature_major(g_ref, t)
        y_ref[t * n_seq:(t + 1) * n_seq, :] = y_t.T.astype(y_ref.dtype)


def _wkv_steps(ops, s0, rk, lng, lnb, *, n_t):
    n, d_rwkv = ops[0].shape
    n_heads, n_seq = s0.shape[0], s0.shape[-1]
    lanes = 2 * HEAD
    assert n == n_t * n_seq and n_seq == lanes

    def bcast(a):
        return jnp.broadcast_to(a.reshape(d_rwkv, 1), (d_rwkv, n_seq))

    tok = pl.BlockSpec((n, lanes), lambda p: (0, p))
    state = pl.BlockSpec((2, HEAD, HEAD, n_seq), lambda p: (p, 0, 0, 0))
    par = pl.BlockSpec((lanes, n_seq), lambda p: (p, 0))
    return pl.pallas_call(
        functools.partial(_wkv_steps_kernel, n_t=n_t),
        grid=(n_heads // 2,),
        in_specs=[tok] * 7 + [state] + [par] * 3,
        out_specs=[tok, state],
        out_shape=[jax.ShapeDtypeStruct((n, d_rwkv), BF16),
                   jax.ShapeDtypeStruct(s0.shape, F32)],
        scratch_shapes=[pltpu.VMEM((lanes, n_seq), F32), pltpu.VMEM((lanes, n_seq), F32)],
        compiler_params=pltpu.CompilerParams(
            dimension_semantics=("arbitrary",), vmem_limit_bytes=VMEM_LIMIT_BYTES),
        name="wkv_steps",
    )(*ops, s0, bcast(rk), bcast(lng), bcast(lnb))


def _out_ffn_kernel(xa_ref, yra_ref, yca_ref, xb_ref, yrb_ref, ycb_ref, wo_r_ref, wo_c_ref,
                    g2_ref, b2_ref, wg_ref, wu_ref, wd_ref, g3_ref, b3_ref, oa_ref, ob_ref, *, alpha):
    last = pl.num_programs(0) - 1
    for x_ref, yr_ref, yc_ref, o_ref, cond in (
            (xa_ref, yra_ref, yca_ref, oa_ref, pl.program_id(0) < last),
            (xb_ref, yrb_ref, ycb_ref, ob_ref, pl.program_id(0) == last)):
        @pl.when(cond)
        def _(x_ref=x_ref, yr_ref=yr_ref, yc_ref=yc_ref, o_ref=o_ref):
            mix = jnp.dot(yr_ref[...], wo_r_ref[...], preferred_element_type=F32)
            mix = mix + jnp.dot(yc_ref[...], wo_c_ref[...], preferred_element_type=F32)

            def mixed_rows(sl):
                return _layer_norm(alpha * x_ref[sl, :] + mix[sl], g2_ref[...], b2_ref[...])

            _ffn_post_ln(mixed_rows, wg_ref, wu_ref, wd_ref, g3_ref, b3_ref, o_ref, alpha)


def _out_ffn(group_a, group_b, wo_r, wo_c, g2, b2, wg, wu, wd, g3, b3, alpha):
    d = group_a[0].shape[1]
    dff = wg.shape[1]
    d_rwkv, d_conv = group_a[1].shape[1], group_a[2].shape[1]
    grid, long_rows, short_rows = _two_group_grid(group_a[0].shape[0], group_b[0].shape[0])
    return pl.pallas_call(
        functools.partial(_out_ffn_kernel, alpha=alpha),
        grid=grid,
        in_specs=[long_rows(d), long_rows(d_rwkv), long_rows(d_conv),
                  short_rows(d), short_rows(d_rwkv), short_rows(d_conv),
                  _resident((d_rwkv, d)), _resident((d_conv, d)),
                  _resident((1, d)), _resident((1, d)),
                  _resident((d, dff)), _resident((d, dff)), _resident((dff, d)),
                  _resident((1, d)), _resident((1, d))],
        out_specs=[long_rows(d), short_rows(d, single_buffer=False)],
        out_shape=[jax.ShapeDtypeStruct(group_a[0].shape, F32),
                   jax.ShapeDtypeStruct(group_b[0].shape, F32)],
        compiler_params=pltpu.CompilerParams(
            dimension_semantics=("arbitrary",), vmem_limit_bytes=VMEM_LIMIT_BYTES),
        name="out_ffn",
    )(*group_a, *group_b, wo_r, wo_c, g2, b2, wg, wu, wd, g3, b3)


def _trunk_layer(xp, xs, p, seg, alpha, *, tp, ts, wkv0_p, wkv0_s, init_s):
    x1p, x1s = _ffn_ln(xp, xs, p["ffn1_wg"], p["ffn1_wu"], p["ffn1_wd"], p["ln1_g"], p["ln1_b"],
                       alpha)
    pm_p = _premix(x1p, p, seg, seq_len=tp, init=None)
    pm_s = _premix(x1s, p, seg, seq_len=ts, init=init_s)
    yr_p, wkv_p = _wkv(pm_p[:7], wkv0_p, p["r_k"], p["lnx_g"], p["lnx_b"],
                       seq_len=tp, chunk=CHUNK, seqs=wkv0_p.shape[0])
    yr_s, wkv_s = _wkv_steps(pm_s[:7], wkv0_s, p["r_k"], p["lnx_g"], p["lnx_b"], n_t=ts)
    d_rwkv = yr_p.shape[1]
    yp, ys = _out_ffn((x1p, yr_p, pm_p[7]), (x1s, yr_s, pm_s[7]),
                      p["w_out"][:d_rwkv], p["w_out"][d_rwkv:],
                      p["ln2_g"], p["ln2_b"], p["ffn2_wg"], p["ffn2_wu"], p["ffn2_wd"],
                      p["ln3_g"], p["ln3_b"], alpha)
    return (yp, wkv_p, pm_p[8], pm_p[9]), (ys, wkv_s, pm_s[8], pm_s[9])


def kernel(x_prompt, x_sample, state_wkv, state_shift, state_conv, ln1_g, ln1_b, ffn1_wg, ffn1_wu, ffn1_wd, w_in, mu_shift, w0, w_lora_up, a0, a_lora_up, g_lora_up, k_k, k_a, r_k, lnx_g, lnx_b, conv_w, w_out, ln2_g, ln2_b, ffn2_wg, ffn2_wu, ffn2_wd, ln3_g, ln3_b):
    depth = ln1_g.shape[0]
    bp, tp, d = x_prompt.shape
    bs, ts, _ = x_sample.shape
    n_heads = state_wkv.shape[2]
    d_rwkv = n_heads * HEAD
    alpha = (2.0 * depth) ** 0.25
    assert ts >= CONV_W - 1

    head_id = jnp.arange(d_rwkv) // HEAD
    seg = (head_id[:, None] == head_id[None, :]).astype(BF16)

    xp = x_prompt.reshape(bp * tp, d)
    xs = x_sample.transpose(1, 0, 2).reshape(ts * bs, d)

    outs = {k: [] for k in ("wkv_p", "shift_p", "conv_p", "wkv_s", "shift_s", "conv_s")}
    for l in range(depth):
        vec = lambda a: a[l].reshape(1, -1).astype(F32)
        p = {
            "ln1_g": vec(ln1_g), "ln1_b": vec(ln1_b),
            "ffn1_wg": ffn1_wg[l].astype(BF16), "ffn1_wu": ffn1_wu[l].astype(BF16),
            "ffn1_wd": ffn1_wd[l].astype(BF16),
            "w_in": w_in[l].astype(BF16), "mu_shift": vec(mu_shift), "w0": vec(w0),
            "w_lora_up": w_lora_up[l].astype(BF16), "a0": vec(a0),
            "a_lora_up": a_lora_up[l].astype(BF16), "g_lora_up": g_lora_up[l].astype(BF16),
            "k_k": vec(k_k), "k_a": vec(k_a), "r_k": vec(r_k),
            "lnx_g": vec(lnx_g), "lnx_b": vec(lnx_b),
            "conv_w": conv_w[l].astype(F32), "w_out": w_out[l].astype(BF16),
            "ln2_g": vec(ln2_g), "ln2_b": vec(ln2_b),
            "ffn2_wg": ffn2_wg[l].astype(BF16), "ffn2_wu": ffn2_wu[l].astype(BF16),
            "ffn2_wd": ffn2_wd[l].astype(BF16),
            "ln3_g": vec(ln3_g), "ln3_b": vec(ln3_b),
        }
        wkv0_p = jnp.zeros((bp, n_heads, HEAD, HEAD), F32)
        init = (state_shift[l].astype(F32), state_conv[l].astype(F32).transpose(1, 0, 2))
        wkv0_s = state_wkv[l].astype(F32).transpose(1, 2, 3, 0)
        (xp, wp, sp, cp), (xs, wsm, ps_s, u_s) = _trunk_layer(
            xp, xs, p, seg, alpha, tp=tp, ts=ts, wkv0_p=wkv0_p, wkv0_s=wkv0_s, init_s=init)
        outs["wkv_p"].append(wp)
        outs["shift_p"].append(sp[:, 0, :])
        outs["conv_p"].append(cp)
        outs["wkv_s"].append(wsm.transpose(3, 0, 1, 2))
        outs["shift_s"].append(ps_s[(ts - 1) * bs:])
        outs["conv_s"].append(
            u_s[(ts - (CONV_W - 1)) * bs:].reshape(CONV_W - 1, bs, -1).transpose(1, 0, 2))

    y_prompt = xp.reshape(bp, tp, d)
    y_sample = xs.reshape(ts, bs, d).transpose(1, 0, 2)
    return (y_prompt, y_sample,
            jnp.stack(outs["wkv_p"]), jnp.stack(outs["shift_p"]), jnp.stack(outs["conv_p"]),
            jnp.stack(outs["wkv_s"]), jnp.stack(outs["shift_s"]), jnp.stack(outs["conv_s"]))
```

```python
import functools
import math

import jax
import jax.numpy as jnp
from jax import lax
from jax.experimental import pallas as pl
from jax.experimental.pallas import tpu as pltpu

F32 = jnp.float32
BF16 = jnp.bfloat16

HEAD = 64
LORA_W = 64
LORA_A = 64
LORA_G = 128
CONV_W = 3
LN_EPS = 1e-5
GN_EPS = 1e-5 * HEAD

VMEM_LIMIT_BYTES = 56 * 1024 * 1024
TOKEN_TILE = 512
FFN_TILE = 1024
FFN_GROUP = 256
CHUNK = 64


def _resident(shape):
    nd = len(shape)
    return pl.BlockSpec(shape, lambda *_: (0,) * nd, pipeline_mode=pl.Buffered(1))


def _layer_norm(x, g, b):
    mu = jnp.mean(x, axis=-1, keepdims=True)
    d = x - mu
    var = jnp.mean(d * d, axis=-1, keepdims=True)
    return d * lax.rsqrt(var + LN_EPS) * g + b


def _ffn_post_ln(rows_of, wg_ref, wu_ref, wd_ref, g_ref, b_ref, o_ref, alpha):
    for s in range(o_ref.shape[0] // FFN_GROUP):
        sl = slice(s * FFN_GROUP, (s + 1) * FFN_GROUP)
        xs = rows_of(sl)
        xb = xs.astype(BF16)
        g = jnp.dot(xb, wg_ref[...], preferred_element_type=F32)
        u = jnp.dot(xb, wu_ref[...], preferred_element_type=F32)
        h = (jax.nn.silu(g) * u).astype(BF16)
        y = alpha * xs + 0.5 * jnp.dot(h, wd_ref[...], preferred_element_type=F32)
        o_ref[sl, :] = _layer_norm(y, g_ref[...], b_ref[...])


def _two_group_grid(n_long, n_short):
    tm = FFN_TILE
    assert n_long % tm == 0 and n_short % FFN_GROUP == 0
    n_tiles = n_long // tm

    def long_rows(c):
        return pl.BlockSpec((tm, c), lambda i: (jnp.minimum(i, n_tiles - 1), 0))

    def short_rows(c, single_buffer=True):
        mode = dict(pipeline_mode=pl.Buffered(1)) if single_buffer else {}
        return pl.BlockSpec((n_short, c), lambda i: (0, 0), **mode)

    return (n_tiles + 1,), long_rows, short_rows


def _ffn_ln_kernel(xa_ref, xb_ref, wg_ref, wu_ref, wd_ref, g_ref, b_ref, oa_ref, ob_ref, *, alpha):
    last = pl.num_programs(0) - 1
    for x_ref, o_ref, cond in ((xa_ref, oa_ref, pl.program_id(0) < last),
                               (xb_ref, ob_ref, pl.program_id(0) == last)):
        @pl.when(cond)
        def _(x_ref=x_ref, o_ref=o_ref):
            _ffn_post_ln(lambda sl: x_ref[sl, :], wg_ref, wu_ref, wd_ref, g_ref, b_ref,
                         o_ref, alpha)


def _ffn_ln(xa, xb, wg, wu, wd, g, b, alpha):
    d = xa.shape[1]
    dff = wg.shape[1]
    grid, long_rows, short_rows = _two_group_grid(xa.shape[0], xb.shape[0])
    return pl.pallas_call(
        functools.partial(_ffn_ln_kernel, alpha=alpha),
        grid=grid,
        in_specs=[long_rows(d), short_rows(d), _resident((d, dff)), _resident((d, dff)),
                  _resident((dff, d)), _resident((1, d)), _resident((1, d))],
        out_specs=[long_rows(d), short_rows(d, single_buffer=False)],
        out_shape=[jax.ShapeDtypeStruct(xa.shape, F32), jax.ShapeDtypeStruct(xb.shape, F32)],
        compiler_params=pltpu.CompilerParams(
            dimension_semantics=("arbitrary",), vmem_limit_bytes=VMEM_LIMIT_BYTES),
        name="ffn_ln",
    )(xa, xb, wg, wu, wd, g, b)


HDR = 8


def _shift_rows(x, head, k):
    rolled = pltpu.roll(x, k, axis=0)
    row8 = lax.broadcasted_iota(jnp.int32, (HDR, 1), 0)
    top = jnp.where(row8 < k, pltpu.roll(head, k, axis=0), rolled[:HDR])
    return jnp.concatenate([top, rolled[HDR:]], axis=0)


class _PremixRefs:
    def __init__(self, refs, tiles_per_seq, full_out):
        refs = list(refs)
        (self.x, self.win, self.mu, self.w0, self.wup, self.a0, self.aup, self.gup, self.kk,
         self.ka, self.cw, self.seg) = refs[:12]
        refs = refs[12:]
        if tiles_per_seq == 0:
            self.sh0, self.cv0 = refs[:2]
            refs = refs[2:]
        (self.r, self.nkk, self.b, self.k, self.v, self.lw, self.g, self.yc) = refs[:8]
        refs = refs[8:]
        if full_out:
            self.ps_out, self.u_out = refs[:2]
        else:
            self.shift, self.tail = refs[:2]
        self.bufs = refs[2:]


def _premix_gates(rf, buf, first, *, tiles_per_seq, d_rwkv, d_conv):
    p_shift = 3 * d_rwkv + LORA_W + LORA_A + LORA_G
    tm = rf.x.shape[0]
    ps = buf[HDR:, :p_shift]
    c_b = buf[HDR:, p_shift:p_shift + d_conv]
    u = buf[HDR:, p_shift + d_conv:p_shift + 2 * d_conv] * buf[HDR:, p_shift + 2 * d_conv:]

    if tiles_per_seq:
        head = jnp.where(first, 0.0, buf[0:HDR, :])
        head_u = head[:, p_shift + d_conv:p_shift + 2 * d_conv] * head[:, p_shift + 2 * d_conv:]
        prev = _shift_rows(ps, head[:, :p_shift], 1)
        um1 = _shift_rows(u, head_u, 1)
        um2 = _shift_rows(u, head_u, 2)
    else:
        n_seq = rf.sh0.shape[0]
        prev = jnp.concatenate([rf.sh0[...], ps[:tm - n_seq]], axis=0)
        hist = jnp.concatenate([rf.cv0[t] for t in range(CONV_W - 1)] + [u], axis=0)
        um2 = hist[:tm]
        um1 = hist[n_seq:n_seq + tm]

    cw = rf.cw[...]
    z = cw[0:1, :] * um2
    z = z + cw[1:2, :] * um1
    z = z + cw[2:3, :] * u
    rf.yc[...] = (c_b * z).astype(BF16)
    if tiles_per_seq:
        rf.shift[0] = ps[tm - 1:tm, :]
        rf.tail[0] = u[tm - (CONV_W - 1):tm, :]
    else:
        rf.ps_out[...] = ps
        rf.u_out[...] = u

    mix = ps + rf.mu[...] * (prev - ps)
    r = mix[:, 0:d_rwkv]
    k = mix[:, d_rwkv:2 * d_rwkv]
    v = mix[:, 2 * d_rwkv:3 * d_rwkv]
    o = 3 * d_rwkv
    wd = mix[:, o:o + LORA_W]
    ad = mix[:, o + LORA_W:o + LORA_W + LORA_A]
    gd = mix[:, o + LORA_W + LORA_A:p_shift]

    rf.r[...] = r
    kk = k * rf.kk[...]
    dot_in = (jnp.tanh(wd).astype(BF16), ad.astype(BF16), jax.nn.sigmoid(gd).astype(BF16),
              (kk * kk).astype(BF16))
    return k, v, kk, dot_in


def _premix_dots(rf, dot_in):
    tanh_wd, ad, sig_gd, kk_sq = dot_in
    w_pre = jnp.dot(tanh_wd, rf.wup[...], preferred_element_type=F32)
    a_pre = jnp.dot(ad, rf.aup[...], preferred_element_type=F32)
    rf.g[...] = jnp.dot(sig_gd, rf.gup[...], preferred_element_type=F32)
    ss = jnp.dot(kk_sq, rf.seg[...], preferred_element_type=F32)
    return w_pre, a_pre, ss


def _premix_finish(rf, gates, dots):
    k, v, kk, _ = gates
    w_pre, a_pre, ss = dots
    log_decay = -math.exp(-0.5) * jax.nn.sigmoid(rf.w0[...] + w_pre)
    a = jax.nn.sigmoid(rf.a0[...] + a_pre)
    kk = kk * lax.rsqrt(jnp.maximum(ss, 1e-24))
    k = k * (1.0 + (a - 1.0) * rf.ka[...])
    nkk = -kk
    b = kk * a
    rf.nkk[...] = nkk
    rf.b[...] = b
    rf.k[...] = k
    rf.v[...] = v
    rf.lw[...] = log_decay


def _premix_kernel(*refs, tiles_per_seq, d_rwkv, d_conv):
    rf = _PremixRefs(refs, tiles_per_seq, full_out=tiles_per_seq == 0)
    buf, = rf.bufs
    tm = rf.x.shape[0]
    i = pl.program_id(0)
    cfg = dict(tiles_per_seq=tiles_per_seq, d_rwkv=d_rwkv, d_conv=d_conv)

    if tiles_per_seq:
        @pl.when(i == 0)
        def _():
            buf[0:HDR, :] = jnp.zeros((HDR, buf.shape[1]), F32)

    buf[HDR:, :] = jnp.dot(rf.x[...].astype(BF16), rf.win[...], preferred_element_type=F32)
    first = (i % tiles_per_seq == 0) if tiles_per_seq else None
    gates = _premix_gates(rf, buf, first, **cfg)
    _premix_finish(rf, gates, _premix_dots(rf, gates[3]))
    if tiles_per_seq:
        buf[0:HDR, :] = buf[tm:tm + HDR, :]


def _premix(x1, p, seg, *, seq_len, init=None):
    n, d = x1.shape
    tm = TOKEN_TILE
    d_rwkv = p["w0"].shape[1]
    d_conv = p["conv_w"].shape[1]
    p_shift = 3 * d_rwkv + LORA_W + LORA_A + LORA_G
    p_total = p_shift + 3 * d_conv
    n_seq = n // seq_len
    if init is None:
        assert seq_len % tm == 0
        tiles_per_seq = seq_len // tm
    else:
        assert n == tm and n_seq % HDR == 0
        tiles_per_seq = 0
    full_out = init is not None

    def row(c):
        return pl.BlockSpec((tm, c), lambda i: (i, 0))

    in_specs = [row(d), _resident((d, p_total)), _resident((1, p_shift)),
                _resident((1, d_rwkv)), _resident((LORA_W, d_rwkv)),
                _resident((1, d_rwkv)), _resident((LORA_A, d_rwkv)),
                _resident((LORA_G, d_rwkv)), _resident((1, d_rwkv)), _resident((1, d_rwkv)),
                _resident((CONV_W, d_conv)), _resident((d_rwkv, d_rwkv))]
    args = [x1, p["w_in"], p["mu_shift"], p["w0"], p["w_lora_up"], p["a0"], p["a_lora_up"],
            p["g_lora_up"], p["k_k"], p["k_a"], p["conv_w"], seg]
    if init is not None:
        in_specs += [_resident((n_seq, p_shift)), _resident((CONV_W - 1, n_seq, d_conv))]
        args += list(init)

    out_specs = [row(d_rwkv)] * 7 + [row(d_conv)]
    out_shape = [jax.ShapeDtypeStruct((n, d_rwkv), F32)] * 7 + [
        jax.ShapeDtypeStruct((n, d_conv), BF16)]
    if full_out:
        out_specs += [row(p_shift), row(d_conv)]
        out_shape += [jax.ShapeDtypeStruct((n, p_shift), F32),
                      jax.ShapeDtypeStruct((n, d_conv), F32)]
    else:
        out_specs += [
            pl.BlockSpec((1, 1, p_shift), lambda i: (i // tiles_per_seq, 0, 0)),
            pl.BlockSpec((1, CONV_W - 1, d_conv), lambda i: (i // tiles_per_seq, 0, 0))]
        out_shape += [jax.ShapeDtypeStruct((n_seq, 1, p_shift), F32),
                      jax.ShapeDtypeStruct((n_seq, CONV_W - 1, d_conv), F32)]

    return pl.pallas_call(
        functools.partial(_premix_kernel, tiles_per_seq=tiles_per_seq, d_rwkv=d_rwkv,
                          d_conv=d_conv),
        grid=(n // tm,),
        in_specs=in_specs,
        out_specs=out_specs,
        out_shape=out_shape,
        scratch_shapes=[pltpu.VMEM((HDR + tm, p_total), F32)],
        compiler_params=pltpu.CompilerParams(
            dimension_semantics=("arbitrary",), vmem_limit_bytes=VMEM_LIMIT_BYTES),
        name="premix",
    )(*args)


def _bdot(a, b):
    return jnp.dot(a.astype(BF16), b.astype(BF16), preferred_element_type=F32)


def _bdot_nt(a, b):
    return lax.dot_general(a.astype(BF16), b.astype(BF16), (((1,), (1,)), ((), ())),
                           preferred_element_type=F32)


def _bdot_tn(a, b):
    return lax.dot_general(a.astype(BF16), b.astype(BF16), (((0,), (0,)), ((), ())),
                           preferred_element_type=F32)


def _wkv_kernel(r_ref, nkk_ref, b_ref, k_ref, v_ref, lw_ref, g_ref, s0_ref,
                rk_ref, lng_ref, lnb_ref, y_ref, sout_ref, s_scr, *, seqs, chunk, n_heads):
    c = pl.program_id(1)
    n_pairs = n_heads // 2
    c2 = 2 * chunk
    lanes = 2 * HEAD

    @pl.when(c == 0)
    def _():
        for q in range(seqs):
            for p in range(n_pairs):
                s_scr[q, p] = jnp.concatenate([s0_ref[q, 2 * p], s0_ref[q, 2 * p + 1]], axis=1)

    t_row = lax.broadcasted_iota(jnp.int32, (chunk, 1), 0)
    n_doublings = chunk.bit_length() - 1

    ri = lax.broadcasted_iota(jnp.int32, (c2, c2), 0)
    ci = lax.broadcasted_iota(jnp.int32, (c2, c2), 1)
    same = (ri // chunk) == (ci // chunk)
    strict = same & (ri % chunk > ci % chunk)
    incl = same & (ri % chunk >= ci % chunk)
    row_head = lax.broadcasted_iota(jnp.int32, (c2, lanes), 0) // chunk
    lane_head = lax.broadcasted_iota(jnp.int32, (c2, lanes), 1) // HEAD
    own = row_head == lane_head
    lo = lane_head == 0
    g_own = (lax.broadcasted_iota(jnp.int32, (lanes, lanes), 0) // HEAD
             == lax.broadcasted_iota(jnp.int32, (lanes, lanes), 1) // HEAD)
    lo_c = lax.broadcasted_iota(jnp.int32, (chunk, lanes), 1) < HEAD

    def stack(x):
        return jnp.concatenate([x, x], axis=0)

    def bd(x):
        return jnp.where(own, stack(x), 0.0)

    def own_half(x, low):
        swapped = pltpu.roll(x, HEAD, axis=1)
        rows = jnp.concatenate([x, swapped] if low else [swapped, x], axis=0)
        return jnp.where(lo if low else ~lo, rows, 0.0)

    ar_bd, bk_st, a_lo, r_lo, v_hi, bkh_bd, gam, unit = [], [], [], [], [], [], [], []
    for q in range(seqs):
        lw = lw_ref[q]
        cum = lw
        for step in range(n_doublings):
            shift = 1 << step
            cum = cum + jnp.where(t_row >= shift, pltpu.roll(cum, shift, axis=0), 0.0)
        last = cum[chunk - 1:chunk, :]
        e_in = jnp.exp(cum)
        e_out = jnp.exp(-cum)
        e_tail = jnp.exp(last - cum)
        a_t = nkk_ref[q] * jnp.exp(cum - lw)
        r_t = r_ref[q] * e_in
        b_t = b_ref[q] * e_out
        k_t = k_ref[q] * e_out
        b_h = b_ref[q] * e_tail
        k_h = k_ref[q] * e_tail
        gamma = jnp.exp(last)
        v_all = v_ref[q]
        for p in range(n_pairs):
            ps = slice(p * lanes, (p + 1) * lanes)
            ar_bd.append(jnp.concatenate([bd(a_t[:, ps]), bd(r_t[:, ps])], axis=0))
            bk_st.append(jnp.concatenate([stack(b_t[:, ps]), stack(k_t[:, ps])], axis=0))
            a_lo.append(own_half(a_t[:, ps], True))
            r_lo.append(own_half(r_t[:, ps], True))
            v_hi.append(own_half(v_all[:, ps], False))
            bkh_bd.append(jnp.concatenate([bd(b_h[:, ps]), bd(k_h[:, ps])], axis=0))
            gam.append(gamma[:, ps])
            unit.append((q, p, ps))

    units = range(seqs * n_pairs)
    m = [_bdot_nt(ar_bd[u], bk_st[u]) for u in units]
    l_ab = [jnp.where(strict, m[u][:c2, :c2], 0.0) for u in units]
    l_ak = [jnp.where(strict, m[u][:c2, c2:], 0.0) for u in units]
    l_rbk = [jnp.concatenate([jnp.where(incl, m[u][c2:, :c2], 0.0),
                              jnp.where(incl, m[u][c2:, c2:], 0.0)], axis=1) for u in units]
    y = [a_lo[u] + _bdot(l_ak[u], v_hi[u]) for u in units]
    pw = l_ab
    for step in range(n_doublings):
        skip = (1 << step) if (1 << step) % 8 == 0 else 0

        def live_rows(x):
            return x if skip == 0 else jnp.concatenate([x[skip:chunk], x[chunk + skip:]], axis=0)

        def all_rows(x):
            if skip == 0:
                return x
            zero = jnp.zeros((skip, x.shape[1]), x.dtype)
            return jnp.concatenate([zero, x[:chunk - skip], zero, x[chunk - skip:]], axis=0)

        if step < n_doublings - 1:
            py = [all_rows(_bdot(live_rows(pw[u]), jnp.concatenate([y[u], pw[u]], axis=1)))
                  for u in units]
            pw = [py[u][:, lanes:] for u in units]
            y = [y[u] + py[u][:, :lanes] for u in units]
        else:
            y = [y[u] + all_rows(_bdot(live_rows(pw[u]), y[u])) for u in units]
    yv = [jnp.concatenate([y[u], v_hi[u]], axis=0) for u in units]
    gh = [_bdot_tn(yv[u], bkh_bd[u]) for u in units]
    lyv = [_bdot(l_rbk[u], yv[u]) for u in units]
    q_lo = [jnp.where(lo, r_lo[u] + lyv[u], 0.0) for u in units]
    q_bd = [jnp.where(own, q_lo[u] + pltpu.roll(q_lo[u], HEAD, axis=1), 0.0) for u in units]
    s = [s_scr[unit[u][0], unit[u][1]] for u in units]
    zero_s = jnp.zeros((HEAD, lanes), F32)
    o = [_bdot_nt(q_bd[u], jnp.concatenate([zero_s, s[u]], axis=0)) for u in units]
    g_bd = [jnp.where(g_own, stack(gh[u][:HEAD]), 0.0) for u in units]
    sg = [_bdot(s[u], g_bd[u]) for u in units]
    for u in units:
        s_scr[unit[u][0], unit[u][1]] = s[u] * gam[u] + sg[u] + gh[u][HEAD:]

    head_sum = g_own.astype(BF16)
    o = [o[u] + lyv[u] for u in units]
    o = [jnp.where(lo_c, pltpu.roll(o[u][:chunk], HEAD, axis=1), o[u][chunk:]) for u in units]
    mu = [_bdot(o[u], head_sum) * (1.0 / HEAD) for u in units]
    d = [o[u] - mu[u] for u in units]
    var = [_bdot(d[u] * d[u], head_sum) * (1.0 / HEAD) for u in units]
    bonus = [_bdot(r_ref[unit[u][0]][:, unit[u][2]] * k_ref[unit[u][0]][:, unit[u][2]]
                   * rk_ref[:, unit[u][2]], head_sum) for u in units]
    for q in range(seqs):
        y_pairs = []
        for p in range(n_pairs):
            u = q * n_pairs + p
            ps = unit[u][2]
            gn = d[u] * lax.rsqrt(var[u] + GN_EPS) * lng_ref[:, ps] + lnb_ref[:, ps]
            y_pairs.append((gn + bonus[u] * v_ref[q][:, ps]) * g_ref[q][:, ps])
        y_ref[q] = jnp.concatenate(y_pairs, axis=1).astype(BF16)

    @pl.when(c == pl.num_programs(1) - 1)
    def _():
        for q in range(seqs):
            for p in range(n_pairs):
                sp = s_scr[q, p]
                sout_ref[q, 2 * p] = sp[:, :HEAD]
                sout_ref[q, 2 * p + 1] = sp[:, HEAD:]


def _wkv(ops, s0, rk, lng, lnb, *, seq_len, chunk, seqs):
    n, d_rwkv = ops[0].shape
    n_seq, n_heads = s0.shape[0], s0.shape[1]
    assert n_seq % seqs == 0 and seq_len % chunk == 0
    ops = [o.reshape(n_seq, seq_len, d_rwkv) for o in ops]
    tok = pl.BlockSpec((seqs, chunk, d_rwkv), lambda q, c: (q, c, 0))
    state = pl.BlockSpec((seqs, n_heads, HEAD, HEAD), lambda q, c: (q, 0, 0, 0))
    y, s_new = pl.pallas_call(
        functools.partial(_wkv_kernel, seqs=seqs, chunk=chunk, n_heads=n_heads),
        grid=(n_seq // seqs, seq_len // chunk),
        in_specs=[tok] * 7 + [state] + [pl.BlockSpec((1, d_rwkv), lambda q, c: (0, 0))] * 3,
        out_specs=[tok, state],
        out_shape=[jax.ShapeDtypeStruct((n_seq, seq_len, d_rwkv), BF16),
                   jax.ShapeDtypeStruct(s0.shape, F32)],
        scratch_shapes=[pltpu.VMEM((seqs, n_heads // 2, HEAD, 2 * HEAD), F32)],
        compiler_params=pltpu.CompilerParams(
            dimension_semantics=("arbitrary", "arbitrary"),
            vmem_limit_bytes=VMEM_LIMIT_BYTES),
        name="wkv",
    )(*ops, s0, rk, lng, lnb)
    return y.reshape(n, d_rwkv), s_new


def _wkv_steps_kernel(r_ref, nkk_ref, b_ref, k_ref, v_ref, lw_ref, g_ref, s0_ref,
                      rk_ref, lng_ref, lnb_ref, y_ref, sout_ref, vt_scr, o_scr, *, n_t):
    n_seq = s0_ref.shape[-1]

    def feature_major(ref, t):
        return ref[t * n_seq:(t + 1) * n_seq, :].T

    for t in range(n_t):
        src = s0_ref if t == 0 else sout_ref
        nkk = feature_major(nkk_ref, t)
        beta = feature_major(b_ref, t)
        key = feature_major(k_ref, t)
        rec = feature_major(r_ref, t)
        decay = jnp.exp(feature_major(lw_ref, t))
        vt_scr[...] = feature_major(v_ref, t)
        for h in range(2):
            hs = slice(h * HEAD, (h + 1) * HEAD)
            nkk_h, beta_h, key_h, rec_h, decay_h = nkk[hs], beta[hs], key[hs], rec[hs], decay[hs]

            def one_row(i, carry):
                s = src[h, i]
                sa = jnp.sum(s * nkk_h, axis=0, keepdims=True)
                v_i = vt_scr[pl.ds(h * HEAD + i, 1), :]
                s = s * decay_h + sa * beta_h + v_i * key_h
                sout_ref[h, i] = s
                o_scr[pl.ds(h * HEAD + i, 1), :] = jnp.sum(s * rec_h, axis=0, keepdims=True)
                return carry

            lax.fori_loop(0, HEAD, one_row, 0, unroll=4)

        v_t = vt_scr[...]
        bonus_prod = rec * key * rk_ref[...]
        ys = []
        for h in range(2):
            hs = slice(h * HEAD, (h + 1) * HEAD)
            o = o_scr[hs, :]
            mu = jnp.mean(o, axis=0, keepdims=True)
            d = o - mu
            var = jnp.mean(d * d, axis=0, keepdims=True)
            gn = d * lax.rsqrt(var + GN_EPS) * lng_ref[hs, :] + lnb_ref[hs, :]
            bonus = jnp.sum(bonus_prod[hs], axis=0, keepdims=True)
            ys.append(gn + bonus * v_t[hs])
        y_t = jnp.concatenate(ys, axis=0) * feature_major(g_ref, t)
        y_ref[t * n_seq:(t + 1) * n_seq, :] = y_t.T.astype(y_ref.dtype)


def _wkv_steps(ops, s0, rk, lng, lnb, *, n_t):
    n, d_rwkv = ops[0].shape
    n_heads, n_seq = s0.shape[0], s0.shape[-1]
    lanes = 2 * HEAD
    assert n == n_t * n_seq and n_seq == lanes

    def bcast(a):
        return jnp.broadcast_to(a.reshape(d_rwkv, 1), (d_rwkv, n_seq))

    tok = pl.BlockSpec((n, lanes), lambda p: (0, p))
    state = pl.BlockSpec((2, HEAD, HEAD, n_seq), lambda p: (p, 0, 0, 0))
    par = pl.BlockSpec((lanes, n_seq), lambda p: (p, 0))
    return pl.pallas_call(
        functools.partial(_wkv_steps_kernel, n_t=n_t),
        grid=(n_heads // 2,),
        in_specs=[tok] * 7 + [state] + [par] * 3,
        out_specs=[tok, state],
        out_shape=[jax.ShapeDtypeStruct((n, d_rwkv), BF16),
                   jax.ShapeDtypeStruct(s0.shape, F32)],
        scratch_shapes=[pltpu.VMEM((lanes, n_seq), F32), pltpu.VMEM((lanes, n_seq), F32)],
        compiler_params=pltpu.CompilerParams(
            dimension_semantics=("arbitrary",), vmem_limit_bytes=VMEM_LIMIT_BYTES),
        name="wkv_steps",
    )(*ops, s0, bcast(rk), bcast(lng), bcast(lnb))


def _out_ffn_kernel(xa_ref, yra_ref, yca_ref, xb_ref, yrb_ref, ycb_ref, wo_r_ref, wo_c_ref,
                    g2_ref, b2_ref, wg_ref, wu_ref, wd_ref, g3_ref, b3_ref, oa_ref, ob_ref, *, alpha):
    last = pl.num_programs(0) - 1
    for x_ref, yr_ref, yc_ref, o_ref, cond in (
            (xa_ref, yra_ref, yca_ref, oa_ref, pl.program_id(0) < last),
            (xb_ref, yrb_ref, ycb_ref, ob_ref, pl.program_id(0) == last)):
        @pl.when(cond)
        def _(x_ref=x_ref, yr_ref=yr_ref, yc_ref=yc_ref, o_ref=o_ref):
            mix = jnp.dot(yr_ref[...], wo_r_ref[...], preferred_element_type=F32)
            mix = mix + jnp.dot(yc_ref[...], wo_c_ref[...], preferred_element_type=F32)

            def mixed_rows(sl):
                return _layer_norm(alpha * x_ref[sl, :] + mix[sl], g2_ref[...], b2_ref[...])

            _ffn_post_ln(mixed_rows, wg_ref, wu_ref, wd_ref, g3_ref, b3_ref, o_ref, alpha)


def _out_ffn(group_a, group_b, wo_r, wo_c, g2, b2, wg, wu, wd, g3, b3, alpha):
    d = group_a[0].shape[1]
    dff = wg.shape[1]
    d_rwkv, d_conv = group_a[1].shape[1], group_a[2].shape[1]
    grid, long_rows, short_rows = _two_group_grid(group_a[0].shape[0], group_b[0].shape[0])
    return pl.pallas_call(
        functools.partial(_out_ffn_kernel, alpha=alpha),
        grid=grid,
        in_specs=[long_rows(d), long_rows(d_rwkv), long_rows(d_conv),
                  short_rows(d), short_rows(d_rwkv), short_rows(d_conv),
                  _resident((d_rwkv, d)), _resident((d_conv, d)),
                  _resident((1, d)), _resident((1, d)),
                  _resident((d, dff)), _resident((d, dff)), _resident((dff, d)),
                  _resident((1, d)), _resident((1, d))],
        out_specs=[long_rows(d), short_rows(d, single_buffer=False)],
        out_shape=[jax.ShapeDtypeStruct(group_a[0].shape, F32),
                   jax.ShapeDtypeStruct(group_b[0].shape, F32)],
        compiler_params=pltpu.CompilerParams(
            dimension_semantics=("arbitrary",), vmem_limit_bytes=VMEM_LIMIT_BYTES),
        name="out_ffn",
    )(*group_a, *group_b, wo_r, wo_c, g2, b2, wg, wu, wd, g3, b3)


def _trunk_layer(xp, xs, p, seg, alpha, *, tp, ts, wkv0_p, wkv0_s, init_s):
    x1p, x1s = _ffn_ln(xp, xs, p["ffn1_wg"], p["ffn1_wu"], p["ffn1_wd"], p["ln1_g"], p["ln1_b"],
                       alpha)
    pm_p = _premix(x1p, p, seg, seq_len=tp, init=None)
    pm_s = _premix(x1s, p, seg, seq_len=ts, init=init_s)
    yr_p, wkv_p = _wkv(pm_p[:7], wkv0_p, p["r_k"], p["lnx_g"], p["lnx_b"],
                       seq_len=tp, chunk=CHUNK, seqs=wkv0_p.shape[0])
    yr_s, wkv_s = _wkv_steps(pm_s[:7], wkv0_s, p["r_k"], p["lnx_g"], p["lnx_b"], n_t=ts)
    d_rwkv = yr_p.shape[1]
    yp, ys = _out_ffn((x1p, yr_p, pm_p[7]), (x1s, yr_s, pm_s[7]),
                      p["w_out"][:d_rwkv], p["w_out"][d_rwkv:],
                      p["ln2_g"], p["ln2_b"], p["ffn2_wg"], p["ffn2_wu"], p["ffn2_wd"],
                      p["ln3_g"], p["ln3_b"], alpha)
    return (yp, wkv_p, pm_p[8], pm_p[9]), (ys, wkv_s, pm_s[8], pm_s[9])


def kernel(x_prompt, x_sample, state_wkv, state_shift, state_conv, ln1_g, ln1_b, ffn1_wg, ffn1_wu, ffn1_wd, w_in, mu_shift, w0, w_lora_up, a0, a_lora_up, g_lora_up, k_k, k_a, r_k, lnx_g, lnx_b, conv_w, w_out, ln2_g, ln2_b, ffn2_wg, ffn2_wu, ffn2_wd, ln3_g, ln3_b):
    depth = ln1_g.shape[0]
    bp, tp, d = x_prompt.shape
    bs, ts, _ = x_sample.shape
    n_heads = state_wkv.shape[2]
    d_rwkv = n_heads * HEAD
    alpha = (2.0 * depth) ** 0.25
    assert ts >= CONV_W - 1

    head_id = jnp.arange(d_rwkv) // HEAD
    seg = (head_id[:, None] == head_id[None, :]).astype(BF16)

    xp = x_prompt.reshape(bp * tp, d)
    xs = x_sample.transpose(1, 0, 2).reshape(ts * bs, d)

    outs = {k: [] for k in ("wkv_p", "shift_p", "conv_p", "wkv_s", "shift_s", "conv_s")}
    for l in range(depth):
        vec = lambda a: a[l].reshape(1, -1).astype(F32)
        p = {
            "ln1_g": vec(ln1_g), "ln1_b": vec(ln1_b),
            "ffn1_wg": ffn1_wg[l].astype(BF16), "ffn1_wu": ffn1_wu[l].astype(BF16),
            "ffn1_wd": ffn1_wd[l].astype(BF16),
            "w_in": w_in[l].astype(BF16), "mu_shift": vec(mu_shift), "w0": vec(w0),
            "w_lora_up": w_lora_up[l].astype(BF16), "a0": vec(a0),
            "a_lora_up": a_lora_up[l].astype(BF16), "g_lora_up": g_lora_up[l].astype(BF16),
            "k_k": vec(k_k), "k_a": vec(k_a), "r_k": vec(r_k),
            "lnx_g": vec(lnx_g), "lnx_b": vec(lnx_b),
            "conv_w": conv_w[l].astype(F32), "w_out": w_out[l].astype(BF16),
            "ln2_g": vec(ln2_g), "ln2_b": vec(ln2_b),
            "ffn2_wg": ffn2_wg[l].astype(BF16), "ffn2_wu": ffn2_wu[l].astype(BF16),
            "ffn2_wd": ffn2_wd[l].astype(BF16),
            "ln3_g": vec(ln3_g), "ln3_b": vec(ln3_b),
        }
        wkv0_p = jnp.zeros((bp, n_heads, HEAD, HEAD), F32)
        init = (state_shift[l].astype(F32), state_conv[l].astype(F32).transpose(1, 0, 2))
        wkv0_s = state_wkv[l].astype(F32).transpose(1, 2, 3, 0)
        (xp, wp, sp, cp), (xs, wsm, ps_s, u_s) = _trunk_layer(
            xp, xs, p, seg, alpha, tp=tp, ts=ts, wkv0_p=wkv0_p, wkv0_s=wkv0_s, init_s=init)
        outs["wkv_p"].append(wp)
        outs["shift_p"].append(sp[:, 0, :])
        outs["conv_p"].append(cp)
        outs["wkv_s"].append(wsm.transpose(3, 0, 1, 2))
        outs["shift_s"].append(ps_s[(ts - 1) * bs:])
        outs["conv_s"].append(
            u_s[(ts - (CONV_W - 1)) * bs:].reshape(CONV_W - 1, bs, -1).transpose(1, 0, 2))

    y_prompt = xp.reshape(bp, tp, d)
    y_sample = xs.reshape(ts, bs, d).transpose(1, 0, 2)
    return (y_prompt, y_sample,
            jnp.stack(outs["wkv_p"]), jnp.stack(outs["shift_p"]), jnp.stack(outs["conv_p"]),
            jnp.stack(outs["wkv_s"]), jnp.stack(outs["shift_s"]), jnp.stack(outs["conv_s"]))
```

```python
import functools
import math

import jax
import jax.numpy as jnp
from jax import lax
from jax.experimental import pallas as pl
from jax.experimental.pallas import tpu as pltpu

F32 = jnp.float32
BF16 = jnp.bfloat16

HEAD = 64
LORA_W = 64
LORA_A = 64
LORA_G = 128
CONV_W = 3
LN_EPS = 1e-5
GN_EPS = 1e-5 * HEAD

VMEM_LIMIT_BYTES = 56 * 1024 * 1024
TOKEN_TILE = 512
FFN_TILE = 1024
FFN_GROUP = 256
CHUNK = 64
CHUNKS_PER_STEP = 1


def _resident(shape):
    nd = len(shape)
    return pl.BlockSpec(shape, lambda *_: (0,) * nd, pipeline_mode=pl.Buffered(1))


def _layer_norm(x, g, b):
    mu = jnp.mean(x, axis=-1, keepdims=True)
    d = x - mu
    var = jnp.mean(d * d, axis=-1, keepdims=True)
    return d * lax.rsqrt(var + LN_EPS) * g + b


def _ffn_post_ln(rows_of, wg_ref, wu_ref, wd_ref, g_ref, b_ref, o_ref, alpha):
    for s in range(o_ref.shape[0] // FFN_GROUP):
        sl = slice(s * FFN_GROUP, (s + 1) * FFN_GROUP)
        xs = rows_of(sl)
        xb = xs.astype(BF16)
        g = jnp.dot(xb, wg_ref[...], preferred_element_type=F32)
        u = jnp.dot(xb, wu_ref[...], preferred_element_type=F32)
        h = (jax.nn.silu(g) * u).astype(BF16)
        y = alpha * xs + 0.5 * jnp.dot(h, wd_ref[...], preferred_element_type=F32)
        o_ref[sl, :] = _layer_norm(y, g_ref[...], b_ref[...])


def _two_group_grid(n_long, n_short):
    tm = FFN_TILE
    assert n_long % tm == 0 and n_short % FFN_GROUP == 0
    n_tiles = n_long // tm

    def long_rows(c):
        return pl.BlockSpec((tm, c), lambda i: (jnp.minimum(i, n_tiles - 1), 0))

    def short_rows(c, single_buffer=True):
        mode = dict(pipeline_mode=pl.Buffered(1)) if single_buffer else {}
        return pl.BlockSpec((n_short, c), lambda i: (0, 0), **mode)

    return (n_tiles + 1,), long_rows, short_rows


def _ffn_ln_kernel(xa_ref, xb_ref, wg_ref, wu_ref, wd_ref, g_ref, b_ref, oa_ref, ob_ref, *, alpha):
    last = pl.num_programs(0) - 1
    for x_ref, o_ref, cond in ((xa_ref, oa_ref, pl.program_id(0) < last),
                               (xb_ref, ob_ref, pl.program_id(0) == last)):
        @pl.when(cond)
        def _(x_ref=x_ref, o_ref=o_ref):
            _ffn_post_ln(lambda sl: x_ref[sl, :], wg_ref, wu_ref, wd_ref, g_ref, b_ref,
                         o_ref, alpha)


def _ffn_ln(xa, xb, wg, wu, wd, g, b, alpha):
    d = xa.shape[1]
    dff = wg.shape[1]
    grid, long_rows, short_rows = _two_group_grid(xa.shape[0], xb.shape[0])
    return pl.pallas_call(
        functools.partial(_ffn_ln_kernel, alpha=alpha),
        grid=grid,
        in_specs=[long_rows(d), short_rows(d), _resident((d, dff)), _resident((d, dff)),
                  _resident((dff, d)), _resident((1, d)), _resident((1, d))],
        out_specs=[long_rows(d), short_rows(d, single_buffer=False)],
        out_shape=[jax.ShapeDtypeStruct(xa.shape, F32), jax.ShapeDtypeStruct(xb.shape, F32)],
        compiler_params=pltpu.CompilerParams(
            dimension_semantics=("arbitrary",), vmem_limit_bytes=VMEM_LIMIT_BYTES),
        name="ffn_ln",
    )(xa, xb, wg, wu, wd, g, b)


HDR = 8


def _shift_rows(x, head, k):
    rolled = pltpu.roll(x, k, axis=0)
    row8 = lax.broadcasted_iota(jnp.int32, (HDR, 1), 0)
    top = jnp.where(row8 < k, pltpu.roll(head, k, axis=0), rolled[:HDR])
    return jnp.concatenate([top, rolled[HDR:]], axis=0)


class _PremixRefs:
    def __init__(self, refs, tiles_per_seq, full_out):
        refs = list(refs)
        (self.x, self.win, self.mu, self.w0, self.wup, self.a0, self.aup, self.gup, self.kk,
         self.ka, self.cw) = refs[:11]
        refs = refs[11:]
        if tiles_per_seq == 0:
            self.sh0, self.cv0 = refs[:2]
            refs = refs[2:]
        (self.r, self.nkk, self.b, self.k, self.v, self.lw, self.g, self.yc) = refs[:8]
        refs = refs[8:]
        if full_out:
            self.ps_out, self.u_out = refs[:2]
        else:
            self.shift, self.tail = refs[:2]
        self.bufs = refs[2:]


def _premix_gates(rf, buf, first, *, tiles_per_seq, d_rwkv, d_conv):
    p_shift = 3 * d_rwkv + LORA_W + LORA_A + LORA_G
    tm = rf.x.shape[0]
    ps = buf[HDR:, :p_shift]
    c_b = buf[HDR:, p_shift:p_shift + d_conv]
    u = buf[HDR:, p_shift + d_conv:p_shift + 2 * d_conv] * buf[HDR:, p_shift + 2 * d_conv:]

    if tiles_per_seq:
        head = jnp.where(first, 0.0, buf[0:HDR, :])
        head_u = head[:, p_shift + d_conv:p_shift + 2 * d_conv] * head[:, p_shift + 2 * d_conv:]
        prev = _shift_rows(ps, head[:, :p_shift], 1)
        um1 = _shift_rows(u, head_u, 1)
        um2 = _shift_rows(u, head_u, 2)
    else:
        n_seq = rf.sh0.shape[0]
        prev = jnp.concatenate([rf.sh0[...], ps[:tm - n_seq]], axis=0)
        hist = jnp.concatenate([rf.cv0[t] for t in range(CONV_W - 1)] + [u], axis=0)
        um2 = hist[:tm]
        um1 = hist[n_seq:n_seq + tm]

    cw = rf.cw[...]
    z = cw[0:1, :] * um2
    z = z + cw[1:2, :] * um1
    z = z + cw[2:3, :] * u
    rf.yc[...] = (c_b * z).astype(BF16)
    if tiles_per_seq:
        rf.shift[0] = ps[tm - 1:tm, :]
        rf.tail[0] = u[tm - (CONV_W - 1):tm, :]
    else:
        rf.ps_out[...] = ps
        rf.u_out[...] = u

    mix = ps + rf.mu[...] * (prev - ps)
    r = mix[:, 0:d_rwkv]
    k = mix[:, d_rwkv:2 * d_rwkv]
    v = mix[:, 2 * d_rwkv:3 * d_rwkv]
    o = 3 * d_rwkv
    wd = mix[:, o:o + LORA_W]
    ad = mix[:, o + LORA_W:o + LORA_W + LORA_A]
    gd = mix[:, o + LORA_W + LORA_A:p_shift]

    rf.r[...] = r
    kk = k * rf.kk[...]
    dot_in = (jnp.tanh(wd).astype(BF16), ad.astype(BF16), jax.nn.sigmoid(gd).astype(BF16),
              kk * kk)
    return k, v, kk, dot_in


def _premix_dots(rf, dot_in):
    tanh_wd, ad, sig_gd, kk_sq = dot_in
    w_pre = jnp.dot(tanh_wd, rf.wup[...], preferred_element_type=F32)
    a_pre = jnp.dot(ad, rf.aup[...], preferred_element_type=F32)
    rf.g[...] = jnp.dot(sig_gd, rf.gup[...], preferred_element_type=F32)
    low = lax.broadcasted_iota(jnp.int32, (1, 2 * HEAD), 1) < HEAD
    parts = []
    for c0 in range(0, kk_sq.shape[1], 2 * HEAD):
        x = kk_sq[:, c0:c0 + 2 * HEAD]
        s_lo = jnp.sum(jnp.where(low, x, 0.0), axis=-1, keepdims=True)
        s_hi = jnp.sum(jnp.where(low, 0.0, x), axis=-1, keepdims=True)
        parts.append(jnp.where(low, s_lo, s_hi))
    ss = jnp.concatenate(parts, axis=1)
    return w_pre, a_pre, ss


def _premix_finish(rf, gates, dots):
    k, v, kk, _ = gates
    w_pre, a_pre, ss = dots
    log_decay = -math.exp(-0.5) * jax.nn.sigmoid(rf.w0[...] + w_pre)
    a = jax.nn.sigmoid(rf.a0[...] + a_pre)
    kk = kk * lax.rsqrt(jnp.maximum(ss, 1e-24))
    k = k * (1.0 + (a - 1.0) * rf.ka[...])
    nkk = -kk
    b = kk * a
    rf.nkk[...] = nkk
    rf.b[...] = b
    rf.k[...] = k
    rf.v[...] = v
    rf.lw[...] = log_decay


def _premix_kernel(*refs, tiles_per_seq, d_rwkv, d_conv):
    rf = _PremixRefs(refs, tiles_per_seq, full_out=tiles_per_seq == 0)
    buf, = rf.bufs
    tm = rf.x.shape[0]
    i = pl.program_id(0)
    cfg = dict(tiles_per_seq=tiles_per_seq, d_rwkv=d_rwkv, d_conv=d_conv)

    if tiles_per_seq:
        @pl.when(i == 0)
        def _():
            buf[0:HDR, :] = jnp.zeros((HDR, buf.shape[1]), F32)

    buf[HDR:, :] = jnp.dot(rf.x[...].astype(BF16), rf.win[...], preferred_element_type=F32)
    first = (i % tiles_per_seq == 0) if tiles_per_seq else None
    gates = _premix_gates(rf, buf, first, **cfg)
    _premix_finish(rf, gates, _premix_dots(rf, gates[3]))
    if tiles_per_seq:
        buf[0:HDR, :] = buf[tm:tm + HDR, :]


def _premix(x1, p, *, seq_len, init=None):
    n, d = x1.shape
    tm = TOKEN_TILE
    d_rwkv = p["w0"].shape[1]
    d_conv = p["conv_w"].shape[1]
    p_shift = 3 * d_rwkv + LORA_W + LORA_A + LORA_G
    p_total = p_shift + 3 * d_conv
    n_seq = n // seq_len
    if init is None:
        assert seq_len % tm == 0
        tiles_per_seq = seq_len // tm
    else:
        assert n == tm and n_seq % HDR == 0
        tiles_per_seq = 0
    full_out = init is not None

    def row(c):
        return pl.BlockSpec((tm, c), lambda i: (i, 0))

    in_specs = [row(d), _resident((d, p_total)), _resident((1, p_shift)),
                _resident((1, d_rwkv)), _resident((LORA_W, d_rwkv)),
                _resident((1, d_rwkv)), _resident((LORA_A, d_rwkv)),
                _resident((LORA_G, d_rwkv)), _resident((1, d_rwkv)), _resident((1, d_rwkv)),
                _resident((CONV_W, d_conv))]
    args = [x1, p["w_in"], p["mu_shift"], p["w0"], p["w_lora_up"], p["a0"], p["a_lora_up"],
            p["g_lora_up"], p["k_k"], p["k_a"], p["conv_w"]]
    if init is not None:
        in_specs += [_resident((n_seq, p_shift)), _resident((CONV_W - 1, n_seq, d_conv))]
        args += list(init)

    out_specs = [row(d_rwkv)] * 7 + [row(d_conv)]
    out_shape = [jax.ShapeDtypeStruct((n, d_rwkv), F32)] * 7 + [
        jax.ShapeDtypeStruct((n, d_conv), BF16)]
    if full_out:
        out_specs += [row(p_shift), row(d_conv)]
        out_shape += [jax.ShapeDtypeStruct((n, p_shift), F32),
                      jax.ShapeDtypeStruct((n, d_conv), F32)]
    else:
        out_specs += [
            pl.BlockSpec((1, 1, p_shift), lambda i: (i // tiles_per_seq, 0, 0)),
            pl.BlockSpec((1, CONV_W - 1, d_conv), lambda i: (i // tiles_per_seq, 0, 0))]
        out_shape += [jax.ShapeDtypeStruct((n_seq, 1, p_shift), F32),
                      jax.ShapeDtypeStruct((n_seq, CONV_W - 1, d_conv), F32)]

    return pl.pallas_call(
        functools.partial(_premix_kernel, tiles_per_seq=tiles_per_seq, d_rwkv=d_rwkv,
                          d_conv=d_conv),
        grid=(n // tm,),
        in_specs=in_specs,
        out_specs=out_specs,
        out_shape=out_shape,
        scratch_shapes=[pltpu.VMEM((HDR + tm, p_total), F32)],
        compiler_params=pltpu.CompilerParams(
            dimension_semantics=("arbitrary",), vmem_limit_bytes=VMEM_LIMIT_BYTES),
        name="premix",
    )(*args)


def _bdot(a, b):
    return jnp.dot(a.astype(BF16), b.astype(BF16), preferred_element_type=F32)


def _bdot_nt(a, b):
    return lax.dot_general(a.astype(BF16), b.astype(BF16), (((1,), (1,)), ((), ())),
                           preferred_element_type=F32)


def _bdot_tn(a, b):
    return lax.dot_general(a.astype(BF16), b.astype(BF16), (((0,), (0,)), ((), ())),
                           preferred_element_type=F32)


def _wkv_kernel(r_ref, nkk_ref, b_ref, k_ref, v_ref, lw_ref, g_ref, s0_ref,
                rk_ref, lng_ref, lnb_ref, y_ref, sout_ref, s_scr, *, seqs, chunk, cps, n_heads):
    c = pl.program_id(1)
    n_pairs = n_heads // 2
    c2 = 2 * chunk
    lanes = 2 * HEAD

    @pl.when(c == 0)
    def _():
        for q in range(seqs):
            for p in range(n_pairs):
                s_scr[q, p] = jnp.concatenate([s0_ref[q, 2 * p], s0_ref[q, 2 * p + 1]], axis=1)

    t_row = lax.broadcasted_iota(jnp.int32, (chunk, 1), 0)
    n_doublings = chunk.bit_length() - 1

    ri = lax.broadcasted_iota(jnp.int32, (c2, c2), 0)
    ci = lax.broadcasted_iota(jnp.int32, (c2, c2), 1)
    same = (ri // chunk) == (ci // chunk)
    strict = same & (ri % chunk > ci % chunk)
    incl = same & (ri % chunk >= ci % chunk)
    row_head = lax.broadcasted_iota(jnp.int32, (c2, lanes), 0) // chunk
    lane_head = lax.broadcasted_iota(jnp.int32, (c2, lanes), 1) // HEAD
    own = row_head == lane_head
    lo = lane_head == 0
    g_own = (lax.broadcasted_iota(jnp.int32, (lanes, lanes), 0) // HEAD
             == lax.broadcasted_iota(jnp.int32, (lanes, lanes), 1) // HEAD)
    lo_c = lax.broadcasted_iota(jnp.int32, (chunk, lanes), 1) < HEAD

    def stack(x):
        return jnp.concatenate([x, x], axis=0)

    def bd(x):
        return jnp.where(own, stack(x), 0.0)

    def own_half(x, low):
        swapped = pltpu.roll(x, HEAD, axis=1)
        rows = jnp.concatenate([x, swapped] if low else [swapped, x], axis=0)
        return jnp.where(lo if low else ~lo, rows, 0.0)

    ar_bd, bk_st, a_lo, r_lo, v_hi, bkh_bd, gam, unit = [], [], [], [], [], [], [], []
    for cc, q in [(cc, q) for cc in range(cps) for q in range(seqs)]:
        rows = slice(cc * chunk, (cc + 1) * chunk)
        lw = lw_ref[q, rows, :]
        cum = lw
        for step in range(n_doublings):
            shift = 1 << step
            cum = cum + jnp.where(t_row >= shift, pltpu.roll(cum, shift, axis=0), 0.0)
        last = cum[chunk - 1:chunk, :]
        e_in = jnp.exp(cum)
        e_out = jnp.exp(-cum)
        e_tail = jnp.exp(last - cum)
        a_t = nkk_ref[q, rows, :] * jnp.exp(cum - lw)
        r_t = r_ref[q, rows, :] * e_in
        b_t = b_ref[q, rows, :] * e_out
        k_t = k_ref[q, rows, :] * e_out
        b_h = b_ref[q, rows, :] * e_tail
        k_h = k_ref[q, rows, :] * e_tail
        gamma = jnp.exp(last)
        v_all = v_ref[q, rows, :]
        for p in range(n_pairs):
            ps = slice(p * lanes, (p + 1) * lanes)
            ar_bd.append(jnp.concatenate([bd(a_t[:, ps]), bd(r_t[:, ps])], axis=0))
            bk_st.append(jnp.concatenate([stack(b_t[:, ps]), stack(k_t[:, ps])], axis=0))
            a_lo.append(own_half(a_t[:, ps], True))
            r_lo.append(own_half(r_t[:, ps], True))
            v_hi.append(own_half(v_all[:, ps], False))
            bkh_bd.append(jnp.concatenate([bd(b_h[:, ps]), bd(k_h[:, ps])], axis=0))
            gam.append(gamma[:, ps])
            unit.append((q, p, ps, rows))

    units = range(cps * seqs * n_pairs)
    m = [_bdot_nt(ar_bd[u], bk_st[u]) for u in units]
    l_ab = [jnp.where(strict, m[u][:c2, :c2], 0.0) for u in units]
    l_ak = [jnp.where(strict, m[u][:c2, c2:], 0.0) for u in units]
    l_rbk = [jnp.concatenate([jnp.where(incl, m[u][c2:, :c2], 0.0),
                              jnp.where(incl, m[u][c2:, c2:], 0.0)], axis=1) for u in units]
    y = [a_lo[u] + _bdot(l_ak[u], v_hi[u]) for u in units]
    pw = l_ab
    for step in range(n_doublings):
        skip = (1 << step) if (1 << step) % 8 == 0 else 0

        def live_rows(x):
            return x if skip == 0 else jnp.concatenate([x[skip:chunk], x[chunk + skip:]], axis=0)

        def all_rows(x):
            if skip == 0:
                return x
            zero = jnp.zeros((skip, x.shape[1]), x.dtype)
            return jnp.concatenate([zero, x[:chunk - skip], zero, x[chunk - skip:]], axis=0)

        if step < n_doublings - 1:
            py = [all_rows(_bdot(live_rows(pw[u]), jnp.concatenate([y[u], pw[u]], axis=1)))
                  for u in units]
            pw = [py[u][:, lanes:] for u in units]
            y = [y[u] + py[u][:, :lanes] for u in units]
        else:
            y = [y[u] + all_rows(_bdot(live_rows(pw[u]), y[u])) for u in units]
    yv = [jnp.concatenate([y[u], v_hi[u]], axis=0) for u in units]
    gh = [_bdot_tn(yv[u], bkh_bd[u]) for u in units]
    lyv = [_bdot(l_rbk[u], yv[u]) for u in units]
    q_lo = [jnp.where(lo, r_lo[u] + lyv[u], 0.0) for u in units]
    q_bd = [jnp.where(own, q_lo[u] + pltpu.roll(q_lo[u], HEAD, axis=1), 0.0) for u in units]
    g_bd = [jnp.where(g_own, stack(gh[u][:HEAD]), 0.0) for u in units]
    zero_s = jnp.zeros((HEAD, lanes), F32)
    per_chunk = seqs * n_pairs
    state = [s_scr[unit[u][0], unit[u][1]] for u in range(per_chunk)]
    o = []
    for cc in range(cps):
        us = range(cc * per_chunk, (cc + 1) * per_chunk)
        o += [_bdot_nt(q_bd[u], jnp.concatenate([zero_s, state[u % per_chunk]], axis=0))
              for u in us]
        sg = [_bdot(state[u % per_chunk], g_bd[u]) for u in us]
        state = [state[u % per_chunk] * gam[u] + sg[u % per_chunk] + gh[u][HEAD:] for u in us]
    for u in range(per_chunk):
        s_scr[unit[u][0], unit[u][1]] = state[u]

    head_sum = g_own.astype(BF16)
    o = [o[u] + lyv[u] for u in units]
    o = [jnp.where(lo_c, pltpu.roll(o[u][:chunk], HEAD, axis=1), o[u][chunk:]) for u in units]
    o_hi = [o[u].astype(BF16) for u in units]
    mu = [(jnp.dot(o_hi[u], head_sum, preferred_element_type=F32)
           + _bdot(o[u] - o_hi[u].astype(F32), head_sum)) * (1.0 / HEAD) for u in units]
    d = [o[u] - mu[u] for u in units]
    var = [_bdot(d[u] * d[u], head_sum) * (1.0 / HEAD) for u in units]
    bonus = [_bdot(r_ref[unit[u][0], unit[u][3], unit[u][2]] * k_ref[unit[u][0], unit[u][3], unit[u][2]]
                   * rk_ref[:, unit[u][2]], head_sum) for u in units]
    for cc, q in [(cc, q) for cc in range(cps) for q in range(seqs)]:
        y_pairs = []
        for p in range(n_pairs):
            u = (cc * seqs + q) * n_pairs + p
            _, _, ps, rows = unit[u]
            gn = d[u] * lax.rsqrt(var[u] + GN_EPS) * lng_ref[:, ps] + lnb_ref[:, ps]
            y_pairs.append((gn + bonus[u] * v_ref[q, rows, ps]) * g_ref[q, rows, ps])
        y_ref[q, cc * chunk:(cc + 1) * chunk, :] = jnp.concatenate(y_pairs, axis=1).astype(BF16)

    @pl.when(c == pl.num_programs(1) - 1)
    def _():
        for q in range(seqs):
            for p in range(n_pairs):
                sp = s_scr[q, p]
                sout_ref[q, 2 * p] = sp[:, :HEAD]
                sout_ref[q, 2 * p + 1] = sp[:, HEAD:]


def _wkv(ops, s0, rk, lng, lnb, *, seq_len, chunk, seqs, cps):
    n, d_rwkv = ops[0].shape
    n_seq, n_heads = s0.shape[0], s0.shape[1]
    assert n_seq % seqs == 0 and seq_len % (cps * chunk) == 0
    ops = [o.reshape(n_seq, seq_len, d_rwkv) for o in ops]
    tok = pl.BlockSpec((seqs, cps * chunk, d_rwkv), lambda q, c: (q, c, 0))
    state = pl.BlockSpec((seqs, n_heads, HEAD, HEAD), lambda q, c: (q, 0, 0, 0))
    y, s_new = pl.pallas_call(
        functools.partial(_wkv_kernel, seqs=seqs, chunk=chunk, cps=cps, n_heads=n_heads),
        grid=(n_seq // seqs, seq_len // (cps * chunk)),
        in_specs=[tok] * 7 + [state] + [pl.BlockSpec((1, d_rwkv), lambda q, c: (0, 0))] * 3,
        out_specs=[tok, state],
        out_shape=[jax.ShapeDtypeStruct((n_seq, seq_len, d_rwkv), BF16),
                   jax.ShapeDtypeStruct(s0.shape, F32)],
        scratch_shapes=[pltpu.VMEM((seqs, n_heads // 2, HEAD, 2 * HEAD), F32)],
        compiler_params=pltpu.CompilerParams(
            dimension_semantics=("arbitrary", "arbitrary"),
            vmem_limit_bytes=VMEM_LIMIT_BYTES),
        name="wkv",
    )(*ops, s0, rk, lng, lnb)
    return y.reshape(n, d_rwkv), s_new


def _wkv_steps_kernel(r_ref, nkk_ref, b_ref, k_ref, v_ref, lw_ref, g_ref, s0_ref,
                      rk_ref, lng_ref, lnb_ref, y_ref, sout_ref, vt_scr, o_scr, *, n_t):
    n_seq = s0_ref.shape[-1]

    def feature_major(ref, t):
        return ref[t * n_seq:(t + 1) * n_seq, :].T

    for t in range(n_t):
        src = s0_ref if t == 0 else sout_ref
        nkk = feature_major(nkk_ref, t)
        beta = feature_major(b_ref, t)
        key = feature_major(k_ref, t)
        rec = feature_major(r_ref, t)
        decay = jnp.exp(feature_major(lw_ref, t))
        vt_scr[...] = feature_major(v_ref, t)
        for h in range(2):
            hs = slice(h * HEAD, (h + 1) * HEAD)
            nkk_h, beta_h, key_h, rec_h, decay_h = nkk[hs], beta[hs], key[hs], rec[hs], decay[hs]

            def one_row(i, carry):
                s = src[h, i]
                sa = jnp.sum(s * nkk_h, axis=0, keepdims=True)
                v_i = vt_scr[pl.ds(h * HEAD + i, 1), :]
                s = s * decay_h + sa * beta_h + v_i * key_h
                sout_ref[h, i] = s
                o_scr[pl.ds(h * HEAD + i, 1), :] = jnp.sum(s * rec_h, axis=0, keepdims=True)
                return carry

            lax.fori_loop(0, HEAD, one_row, 0, unroll=8)

        v_t = vt_scr[...]
        bonus_prod = rec * key * rk_ref[...]
        ys = []
        for h in range(2):
            hs = slice(h * HEAD, (h + 1) * HEAD)
            o = o_scr[hs, :]
            mu = jnp.mean(o, axis=0, keepdims=True)
            d = o - mu
            var = jnp.mean(d * d, axis=0, keepdims=True)
            gn = d * lax.rsqrt(var + GN_EPS) * lng_ref[hs, :] + lnb_ref[hs, :]
            bonus = jnp.sum(bonus_prod[hs], axis=0, keepdims=True)
            ys.append(gn + bonus * v_t[hs])
        y_t = jnp.concatenate(ys, axis=0) * feature_major(g_ref, t)
        y_ref[t * n_seq:(t + 1) * n_seq, :] = y_t.T.astype(y_ref.dtype)


def _wkv_steps(ops, s0, rk, lng, lnb, *, n_t):
    n, d_rwkv = ops[0].shape
    n_heads, n_seq = s0.shape[0], s0.shape[-1]
    lanes = 2 * HEAD
    assert n == n_t * n_seq and n_seq == lanes

    def bcast(a):
        return jnp.broadcast_to(a.reshape(d_rwkv, 1), (d_rwkv, n_seq))

    tok = pl.BlockSpec((n, lanes), lambda p: (0, p))
    state = pl.BlockSpec((2, HEAD, HEAD, n_seq), lambda p: (p, 0, 0, 0))
    par = pl.BlockSpec((lanes, n_seq), lambda p: (p, 0))
    return pl.pallas_call(
        functools.partial(_wkv_steps_kernel, n_t=n_t),
        grid=(n_heads // 2,),
        in_specs=[tok] * 7 + [state] + [par] * 3,
        out_specs=[tok, state],
        out_shape=[jax.ShapeDtypeStruct((n, d_rwkv), BF16),
                   jax.ShapeDtypeStruct(s0.shape, F32)],
        scratch_shapes=[pltpu.VMEM((lanes, n_seq), F32), pltpu.VMEM((lanes, n_seq), F32)],
        compiler_params=pltpu.CompilerParams(
            dimension_semantics=("arbitrary",), vmem_limit_bytes=VMEM_LIMIT_BYTES),
        name="wkv_steps",
    )(*ops, s0, bcast(rk), bcast(lng), bcast(lnb))


def _out_ffn_kernel(xa_ref, yra_ref, yca_ref, xb_ref, yrb_ref, ycb_ref, wo_r_ref, wo_c_ref,
                    g2_ref, b2_ref, wg_ref, wu_ref, wd_ref, g3_ref, b3_ref, oa_ref, ob_ref, *, alpha):
    last = pl.num_programs(0) - 1
    for x_ref, yr_ref, yc_ref, o_ref, cond in (
            (xa_ref, yra_ref, yca_ref, oa_ref, pl.program_id(0) < last),
            (xb_ref, yrb_ref, ycb_ref, ob_ref, pl.program_id(0) == last)):
        @pl.when(cond)
        def _(x_ref=x_ref, yr_ref=yr_ref, yc_ref=yc_ref, o_ref=o_ref):
            mix = jnp.dot(yr_ref[...], wo_r_ref[...], preferred_element_type=F32)
            mix = mix + jnp.dot(yc_ref[...], wo_c_ref[...], preferred_element_type=F32)

            def mixed_rows(sl):
                return _layer_norm(alpha * x_ref[sl, :] + mix[sl], g2_ref[...], b2_ref[...])

            _ffn_post_ln(mixed_rows, wg_ref, wu_ref, wd_ref, g3_ref, b3_ref, o_ref, alpha)


def _out_ffn(group_a, group_b, wo_r, wo_c, g2, b2, wg, wu, wd, g3, b3, alpha):
    d = group_a[0].shape[1]
    dff = wg.shape[1]
    d_rwkv, d_conv = group_a[1].shape[1], group_a[2].shape[1]
    grid, long_rows, short_rows = _two_group_grid(group_a[0].shape[0], group_b[0].shape[0])
    return pl.pallas_call(
        functools.partial(_out_ffn_kernel, alpha=alpha),
        grid=grid,
        in_specs=[long_rows(d), long_rows(d_rwkv), long_rows(d_conv),
                  short_rows(d), short_rows(d_rwkv), short_rows(d_conv),
                  _resident((d_rwkv, d)), _resident((d_conv, d)),
                  _resident((1, d)), _resident((1, d)),
                  _resident((d, dff)), _resident((d, dff)), _resident((dff, d)),
                  _resident((1, d)), _resident((1, d))],
        out_specs=[long_rows(d), short_rows(d, single_buffer=False)],
        out_shape=[jax.ShapeDtypeStruct(group_a[0].shape, F32),
                   jax.ShapeDtypeStruct(group_b[0].shape, F32)],
        compiler_params=pltpu.CompilerParams(
            dimension_semantics=("arbitrary",), vmem_limit_bytes=VMEM_LIMIT_BYTES),
        name="out_ffn",
    )(*group_a, *group_b, wo_r, wo_c, g2, b2, wg, wu, wd, g3, b3)


def _trunk_layer(xp, xs, p, alpha, *, tp, ts, wkv0_p, wkv0_s, init_s):
    x1p, x1s = _ffn_ln(xp, xs, p["ffn1_wg"], p["ffn1_wu"], p["ffn1_wd"], p["ln1_g"], p["ln1_b"],
                       alpha)
    pm_p = _premix(x1p, p, seq_len=tp, init=None)
    pm_s = _premix(x1s, p, seq_len=ts, init=init_s)
    yr_p, wkv_p = _wkv(pm_p[:7], wkv0_p, p["r_k"], p["lnx_g"], p["lnx_b"],
                       seq_len=tp, chunk=CHUNK, seqs=wkv0_p.shape[0], cps=CHUNKS_PER_STEP)
    yr_s, wkv_s = _wkv_steps(pm_s[:7], wkv0_s, p["r_k"], p["lnx_g"], p["lnx_b"], n_t=ts)
    d_rwkv = yr_p.shape[1]
    yp, ys = _out_ffn((x1p, yr_p, pm_p[7]), (x1s, yr_s, pm_s[7]),
                      p["w_out"][:d_rwkv], p["w_out"][d_rwkv:],
                      p["ln2_g"], p["ln2_b"], p["ffn2_wg"], p["ffn2_wu"], p["ffn2_wd"],
                      p["ln3_g"], p["ln3_b"], alpha)
    return (yp, wkv_p, pm_p[8], pm_p[9]), (ys, wkv_s, pm_s[8], pm_s[9])


def kernel(x_prompt, x_sample, state_wkv, state_shift, state_conv, ln1_g, ln1_b, ffn1_wg, ffn1_wu, ffn1_wd, w_in, mu_shift, w0, w_lora_up, a0, a_lora_up, g_lora_up, k_k, k_a, r_k, lnx_g, lnx_b, conv_w, w_out, ln2_g, ln2_b, ffn2_wg, ffn2_wu, ffn2_wd, ln3_g, ln3_b):
    depth = ln1_g.shape[0]
    bp, tp, d = x_prompt.shape
    bs, ts, _ = x_sample.shape
    n_heads = state_wkv.shape[2]
    alpha = (2.0 * depth) ** 0.25
    assert ts >= CONV_W - 1

    xp = x_prompt.reshape(bp * tp, d)
    xs = x_sample.transpose(1, 0, 2).reshape(ts * bs, d)

    outs = {k: [] for k in ("wkv_p", "shift_p", "conv_p", "wkv_s", "shift_s", "conv_s")}
    for l in range(depth):
        vec = lambda a: a[l].reshape(1, -1).astype(F32)
        p = {
            "ln1_g": vec(ln1_g), "ln1_b": vec(ln1_b),
            "ffn1_wg": ffn1_wg[l].astype(BF16), "ffn1_wu": ffn1_wu[l].astype(BF16),
            "ffn1_wd": ffn1_wd[l].astype(BF16),
            "w_in": w_in[l].astype(BF16), "mu_shift": vec(mu_shift), "w0": vec(w0),
            "w_lora_up": w_lora_up[l].astype(BF16), "a0": vec(a0),
            "a_lora_up": a_lora_up[l].astype(BF16), "g_lora_up": g_lora_up[l].astype(BF16),
            "k_k": vec(k_k), "k_a": vec(k_a), "r_k": vec(r_k),
            "lnx_g": vec(lnx_g), "lnx_b": vec(lnx_b),
            "conv_w": conv_w[l].astype(F32), "w_out": w_out[l].astype(BF16),
            "ln2_g": vec(ln2_g), "ln2_b": vec(ln2_b),
            "ffn2_wg": ffn2_wg[l].astype(BF16), "ffn2_wu": ffn2_wu[l].astype(BF16),
            "ffn2_wd": ffn2_wd[l].astype(BF16),
            "ln3_g": vec(ln3_g), "ln3_b": vec(ln3_b),
        }
        wkv0_p = jnp.zeros((bp, n_heads, HEAD, HEAD), F32)
        init = (state_shift[l].astype(F32), state_conv[l].astype(F32).transpose(1, 0, 2))
        wkv0_s = state_wkv[l].astype(F32).transpose(1, 2, 3, 0)
        (xp, wp, sp, cp), (xs, wsm, ps_s, u_s) = _trunk_layer(
            xp, xs, p, alpha, tp=tp, ts=ts, wkv0_p=wkv0_p, wkv0_s=wkv0_s, init_s=init)
        outs["wkv_p"].append(wp)
        outs["shift_p"].append(sp[:, 0, :])
        outs["conv_p"].append(cp)
        outs["wkv_s"].append(wsm.transpose(3, 0, 1, 2))
        outs["shift_s"].append(ps_s[(ts - 1) * bs:])
        outs["conv_s"].append(
            u_s[(ts - (CONV_W - 1)) * bs:].reshape(CONV_W - 1, bs, -1).transpose(1, 0, 2))

    y_prompt = xp.reshape(bp, tp, d)
    y_sample = xs.reshape(ts, bs, d).transpose(1, 0, 2)
    return (y_prompt, y_sample,
            jnp.stack(outs["wkv_p"]), jnp.stack(outs["shift_p"]), jnp.stack(outs["conv_p"]),
            jnp.stack(outs["wkv_s"]), jnp.stack(outs["shift_s"]), jnp.stack(outs["conv_s"]))
```

```python
import functools
import math

import jax
import jax.numpy as jnp
from jax import lax
from jax.experimental import pallas as pl
from jax.experimental.pallas import tpu as pltpu

F32 = jnp.float32
BF16 = jnp.bfloat16

HEAD = 64
LORA_W = 64
LORA_A = 64
LORA_G = 128
CONV_W = 3
LN_EPS = 1e-5
GN_EPS = 1e-5 * HEAD

VMEM_LIMIT_BYTES = 56 * 1024 * 1024
TOKEN_TILE = 512
FFN_TILE = 1024
FFN_GROUP = 512
CHUNK = 64
CHUNKS_PER_STEP = 1


def _resident(shape):
    nd = len(shape)
    return pl.BlockSpec(shape, lambda *_: (0,) * nd, pipeline_mode=pl.Buffered(1))


def _layer_norm(x, g, b):
    mu = jnp.mean(x, axis=-1, keepdims=True)
    d = x - mu
    var = jnp.mean(d * d, axis=-1, keepdims=True)
    return d * lax.rsqrt(var + LN_EPS) * g + b


def _ffn_post_ln(rows_of, wg_ref, wu_ref, wd_ref, g_ref, b_ref, o_ref, alpha):
    for s in range(o_ref.shape[0] // FFN_GROUP):
        sl = slice(s * FFN_GROUP, (s + 1) * FFN_GROUP)
        xs = rows_of(sl)
        xb = xs.astype(BF16)
        g = jnp.dot(xb, wg_ref[...], preferred_element_type=F32)
        u = jnp.dot(xb, wu_ref[...], preferred_element_type=F32)
        h = (jax.nn.silu(g) * u).astype(BF16)
        y = alpha * xs + 0.5 * jnp.dot(h, wd_ref[...], preferred_element_type=F32)
        o_ref[sl, :] = _layer_norm(y, g_ref[...], b_ref[...])


def _two_group_grid(n_long, n_short):
    tm = FFN_TILE
    assert n_long % tm == 0 and n_short % FFN_GROUP == 0
    n_tiles = n_long // tm

    def long_rows(c):
        return pl.BlockSpec((tm, c), lambda i: (jnp.minimum(i, n_tiles - 1), 0))

    def short_rows(c, single_buffer=True):
        mode = dict(pipeline_mode=pl.Buffered(1)) if single_buffer else {}
        return pl.BlockSpec((n_short, c), lambda i: (0, 0), **mode)

    return (n_tiles + 1,), long_rows, short_rows


def _ffn_ln_kernel(xa_ref, xb_ref, wg_ref, wu_ref, wd_ref, g_ref, b_ref, oa_ref, ob_ref, *, alpha):
    last = pl.num_programs(0) - 1
    for x_ref, o_ref, cond in ((xa_ref, oa_ref, pl.program_id(0) < last),
                               (xb_ref, ob_ref, pl.program_id(0) == last)):
        @pl.when(cond)
        def _(x_ref=x_ref, o_ref=o_ref):
            _ffn_post_ln(lambda sl: x_ref[sl, :], wg_ref, wu_ref, wd_ref, g_ref, b_ref,
                         o_ref, alpha)


def _ffn_ln(xa, xb, wg, wu, wd, g, b, alpha):
    d = xa.shape[1]
    dff = wg.shape[1]
    grid, long_rows, short_rows = _two_group_grid(xa.shape[0], xb.shape[0])
    return pl.pallas_call(
        functools.partial(_ffn_ln_kernel, alpha=alpha),
        grid=grid,
        in_specs=[long_rows(d), short_rows(d), _resident((d, dff)), _resident((d, dff)),
                  _resident((dff, d)), _resident((1, d)), _resident((1, d))],
        out_specs=[long_rows(d), short_rows(d, single_buffer=False)],
        out_shape=[jax.ShapeDtypeStruct(xa.shape, F32), jax.ShapeDtypeStruct(xb.shape, F32)],
        compiler_params=pltpu.CompilerParams(
            dimension_semantics=("arbitrary",), vmem_limit_bytes=VMEM_LIMIT_BYTES),
        name="ffn_ln",
    )(xa, xb, wg, wu, wd, g, b)


HDR = 8


def _shift_rows(x, head, k):
    rolled = pltpu.roll(x, k, axis=0)
    row8 = lax.broadcasted_iota(jnp.int32, (HDR, 1), 0)
    top = jnp.where(row8 < k, pltpu.roll(head, k, axis=0), rolled[:HDR])
    return jnp.concatenate([top, rolled[HDR:]], axis=0)


class _PremixRefs:
    def __init__(self, refs, tiles_per_seq, full_out):
        refs = list(refs)
        (self.x, self.win, self.mu, self.w0, self.wup, self.a0, self.aup, self.gup, self.kk,
         self.ka, self.cw) = refs[:11]
        refs = refs[11:]
        if tiles_per_seq == 0:
            self.sh0, self.cv0 = refs[:2]
            refs = refs[2:]
        (self.r, self.nkk, self.b, self.k, self.v, self.lw, self.g, self.yc) = refs[:8]
        refs = refs[8:]
        if full_out:
            self.ps_out, self.u_out = refs[:2]
        else:
            self.shift, self.tail = refs[:2]
        self.bufs = refs[2:]


def _premix_gates(rf, buf, first, *, tiles_per_seq, d_rwkv, d_conv):
    p_shift = 3 * d_rwkv + LORA_W + LORA_A + LORA_G
    tm = rf.x.shape[0]
    ps = buf[HDR:, :p_shift]
    c_b = buf[HDR:, p_shift:p_shift + d_conv]
    u = buf[HDR:, p_shift + d_conv:p_shift + 2 * d_conv] * buf[HDR:, p_shift + 2 * d_conv:]

    if tiles_per_seq:
        head = jnp.where(first, 0.0, buf[0:HDR, :])
        head_u = head[:, p_shift + d_conv:p_shift + 2 * d_conv] * head[:, p_shift + 2 * d_conv:]
        prev = _shift_rows(ps, head[:, :p_shift], 1)
        um1 = _shift_rows(u, head_u, 1)
        um2 = _shift_rows(u, head_u, 2)
    else:
        n_seq = rf.sh0.shape[0]
        prev = jnp.concatenate([rf.sh0[...], ps[:tm - n_seq]], axis=0)
        hist = jnp.concatenate([rf.cv0[t] for t in range(CONV_W - 1)] + [u], axis=0)
        um2 = hist[:tm]
        um1 = hist[n_seq:n_seq + tm]

    cw = rf.cw[...]
    z = cw[0:1, :] * um2
    z = z + cw[1:2, :] * um1
    z = z + cw[2:3, :] * u
    rf.yc[...] = (c_b * z).astype(BF16)
    if tiles_per_seq:
        rf.shift[0] = ps[tm - 1:tm, :]
        rf.tail[0] = u[tm - (CONV_W - 1):tm, :]
    else:
        rf.ps_out[...] = ps
        rf.u_out[...] = u

    mix = ps + rf.mu[...] * (prev - ps)
    r = mix[:, 0:d_rwkv]
    k = mix[:, d_rwkv:2 * d_rwkv]
    v = mix[:, 2 * d_rwkv:3 * d_rwkv]
    o = 3 * d_rwkv
    wd = mix[:, o:o + LORA_W]
    ad = mix[:, o + LORA_W:o + LORA_W + LORA_A]
    gd = mix[:, o + LORA_W + LORA_A:p_shift]

    rf.r[...] = r
    kk = k * rf.kk[...]
    dot_in = (jnp.tanh(wd).astype(BF16), ad.astype(BF16), jax.nn.sigmoid(gd).astype(BF16),
              kk * kk)
    return k, v, kk, dot_in


def _premix_dots(rf, dot_in):
    tanh_wd, ad, sig_gd, kk_sq = dot_in
    w_pre = jnp.dot(tanh_wd, rf.wup[...], preferred_element_type=F32)
    a_pre = jnp.dot(ad, rf.aup[...], preferred_element_type=F32)
    rf.g[...] = jnp.dot(sig_gd, rf.gup[...], preferred_element_type=F32)
    low = lax.broadcasted_iota(jnp.int32, (1, 2 * HEAD), 1) < HEAD
    parts = []
    for c0 in range(0, kk_sq.shape[1], 2 * HEAD):
        x = kk_sq[:, c0:c0 + 2 * HEAD]
        s_lo = jnp.sum(jnp.where(low, x, 0.0), axis=-1, keepdims=True)
        s_hi = jnp.sum(jnp.where(low, 0.0, x), axis=-1, keepdims=True)
        parts.append(jnp.where(low, s_lo, s_hi))
    ss = jnp.concatenate(parts, axis=1)
    return w_pre, a_pre, ss


def _premix_finish(rf, gates, dots):
    k, v, kk, _ = gates
    w_pre, a_pre, ss = dots
    log_decay = -math.exp(-0.5) * jax.nn.sigmoid(rf.w0[...] + w_pre)
    a = jax.nn.sigmoid(rf.a0[...] + a_pre)
    kk = kk * lax.rsqrt(jnp.maximum(ss, 1e-24))
    k = k * (1.0 + (a - 1.0) * rf.ka[...])
    nkk = -kk
    b = kk * a
    rf.nkk[...] = nkk
    rf.b[...] = b
    rf.k[...] = k
    rf.v[...] = v
    rf.lw[...] = log_decay


def _premix_kernel(*refs, tiles_per_seq, d_rwkv, d_conv):
    rf = _PremixRefs(refs, tiles_per_seq, full_out=tiles_per_seq == 0)
    buf, = rf.bufs
    tm = rf.x.shape[0]
    i = pl.program_id(0)
    cfg = dict(tiles_per_seq=tiles_per_seq, d_rwkv=d_rwkv, d_conv=d_conv)

    if tiles_per_seq:
        @pl.when(i == 0)
        def _():
            buf[0:HDR, :] = jnp.zeros((HDR, buf.shape[1]), F32)

    buf[HDR:, :] = jnp.dot(rf.x[...].astype(BF16), rf.win[...], preferred_element_type=F32)
    first = (i % tiles_per_seq == 0) if tiles_per_seq else None
    gates = _premix_gates(rf, buf, first, **cfg)
    _premix_finish(rf, gates, _premix_dots(rf, gates[3]))
    if tiles_per_seq:
        buf[0:HDR, :] = buf[tm:tm + HDR, :]


def _premix(x1, p, *, seq_len, init=None):
    n, d = x1.shape
    tm = TOKEN_TILE
    d_rwkv = p["w0"].shape[1]
    d_conv = p["conv_w"].shape[1]
    p_shift = 3 * d_rwkv + LORA_W + LORA_A + LORA_G
    p_total = p_shift + 3 * d_conv
    n_seq = n // seq_len
    if init is None:
        assert seq_len % tm == 0
        tiles_per_seq = seq_len // tm
    else:
        assert n == tm and n_seq % HDR == 0
        tiles_per_seq = 0
    full_out = init is not None

    def row(c):
        return pl.BlockSpec((tm, c), lambda i: (i, 0))

    in_specs = [row(d), _resident((d, p_total)), _resident((1, p_shift)),
                _resident((1, d_rwkv)), _resident((LORA_W, d_rwkv)),
                _resident((1, d_rwkv)), _resident((LORA_A, d_rwkv)),
                _resident((LORA_G, d_rwkv)), _resident((1, d_rwkv)), _resident((1, d_rwkv)),
                _resident((CONV_W, d_conv))]
    args = [x1, p["w_in"], p["mu_shift"], p["w0"], p["w_lora_up"], p["a0"], p["a_lora_up"],
            p["g_lora_up"], p["k_k"], p["k_a"], p["conv_w"]]
    if init is not None:
        in_specs += [_resident((n_seq, p_shift)), _resident((CONV_W - 1, n_seq, d_conv))]
        args += list(init)

    out_specs = [row(d_rwkv)] * 7 + [row(d_conv)]
    out_shape = [jax.ShapeDtypeStruct((n, d_rwkv), F32)] * 7 + [
        jax.ShapeDtypeStruct((n, d_conv), BF16)]
    if full_out:
        out_specs += [row(p_shift), row(d_conv)]
        out_shape += [jax.ShapeDtypeStruct((n, p_shift), F32),
                      jax.ShapeDtypeStruct((n, d_conv), F32)]
    else:
        out_specs += [
            pl.BlockSpec((1, 1, p_shift), lambda i: (i // tiles_per_seq, 0, 0)),
            pl.BlockSpec((1, CONV_W - 1, d_conv), lambda i: (i // tiles_per_seq, 0, 0))]
        out_shape += [jax.ShapeDtypeStruct((n_seq, 1, p_shift), F32),
                      jax.ShapeDtypeStruct((n_seq, CONV_W - 1, d_conv), F32)]

    return pl.pallas_call(
        functools.partial(_premix_kernel, tiles_per_seq=tiles_per_seq, d_rwkv=d_rwkv,
                          d_conv=d_conv),
        grid=(n // tm,),
        in_specs=in_specs,
        out_specs=out_specs,
        out_shape=out_shape,
        scratch_shapes=[pltpu.VMEM((HDR + tm, p_total), F32)],
        compiler_params=pltpu.CompilerParams(
            dimension_semantics=("arbitrary",), vmem_limit_bytes=VMEM_LIMIT_BYTES),
        name="premix",
    )(*args)


def _bdot(a, b):
    return jnp.dot(a.astype(BF16), b.astype(BF16), preferred_element_type=F32)


def _bdot_nt(a, b):
    return lax.dot_general(a.astype(BF16), b.astype(BF16), (((1,), (1,)), ((), ())),
                           preferred_element_type=F32)


def _bdot_tn(a, b):
    return lax.dot_general(a.astype(BF16), b.astype(BF16), (((0,), (0,)), ((), ())),
                           preferred_element_type=F32)


def _wkv_kernel(r_ref, nkk_ref, b_ref, k_ref, v_ref, lw_ref, g_ref, s0_ref,
                rk_ref, lng_ref, lnb_ref, y_ref, sout_ref, s_scr, *, seqs, chunk, cps, n_heads):
    c = pl.program_id(1)
    n_pairs = n_heads // 2
    c2 = 2 * chunk
    lanes = 2 * HEAD

    @pl.when(c == 0)
    def _():
        for q in range(seqs):
            for p in range(n_pairs):
                s_scr[q, p] = jnp.concatenate([s0_ref[q, 2 * p], s0_ref[q, 2 * p + 1]], axis=1)

    t_row = lax.broadcasted_iota(jnp.int32, (chunk, 1), 0)
    n_doublings = chunk.bit_length() - 1

    ri = lax.broadcasted_iota(jnp.int32, (c2, c2), 0)
    ci = lax.broadcasted_iota(jnp.int32, (c2, c2), 1)
    same = (ri // chunk) == (ci // chunk)
    strict = same & (ri % chunk > ci % chunk)
    incl = same & (ri % chunk >= ci % chunk)
    row_head = lax.broadcasted_iota(jnp.int32, (c2, lanes), 0) // chunk
    lane_head = lax.broadcasted_iota(jnp.int32, (c2, lanes), 1) // HEAD
    own = row_head == lane_head
    lo = lane_head == 0
    g_own = (lax.broadcasted_iota(jnp.int32, (lanes, lanes), 0) // HEAD
             == lax.broadcasted_iota(jnp.int32, (lanes, lanes), 1) // HEAD)
    lo_c = lax.broadcasted_iota(jnp.int32, (chunk, lanes), 1) < HEAD

    def stack(x):
        return jnp.concatenate([x, x], axis=0)

    def bd(x):
        return jnp.where(own, stack(x), 0.0)

    def own_half(x, low):
        swapped = pltpu.roll(x, HEAD, axis=1)
        rows = jnp.concatenate([x, swapped] if low else [swapped, x], axis=0)
        return jnp.where(lo if low else ~lo, rows, 0.0)

    ar_bd, bk_st, a_lo, r_lo, v_hi, bkh_bd, gam, unit = [], [], [], [], [], [], [], []
    for cc, q in [(cc, q) for cc in range(cps) for q in range(seqs)]:
        rows = slice(cc * chunk, (cc + 1) * chunk)
        lw = lw_ref[q, rows, :]
        cum = lw
        for step in range(n_doublings):
            shift = 1 << step
            cum = cum + jnp.where(t_row >= shift, pltpu.roll(cum, shift, axis=0), 0.0)
        last = cum[chunk - 1:chunk, :]
        e_in = jnp.exp(cum)
        e_out = jnp.exp(-cum)
        e_tail = jnp.exp(last - cum)
        a_t = nkk_ref[q, rows, :] * jnp.exp(cum - lw)
        r_t = r_ref[q, rows, :] * e_in
        b_t = b_ref[q, rows, :] * e_out
        k_t = k_ref[q, rows, :] * e_out
        b_h = b_ref[q, rows, :] * e_tail
        k_h = k_ref[q, rows, :] * e_tail
        gamma = jnp.exp(last)
        v_all = v_ref[q, rows, :]
        for p in range(n_pairs):
            ps = slice(p * lanes, (p + 1) * lanes)
            ar_bd.append(jnp.concatenate([bd(a_t[:, ps]), bd(r_t[:, ps])], axis=0))
            bk_st.append(jnp.concatenate([stack(b_t[:, ps]), stack(k_t[:, ps])], axis=0))
            a_lo.append(own_half(a_t[:, ps], True))
            r_lo.append(own_half(r_t[:, ps], True))
            v_hi.append(own_half(v_all[:, ps], False))
            bkh_bd.append(jnp.concatenate([bd(b_h[:, ps]), bd(k_h[:, ps])], axis=0))
            gam.append(gamma[:, ps])
            unit.append((q, p, ps, rows))

    units = range(cps * seqs * n_pairs)
    m = [_bdot_nt(ar_bd[u], bk_st[u]) for u in units]
    l_ab = [jnp.where(strict, m[u][:c2, :c2], 0.0) for u in units]
    l_ak = [jnp.where(strict, m[u][:c2, c2:], 0.0) for u in units]
    l_rbk = [jnp.concatenate([jnp.where(incl, m[u][c2:, :c2], 0.0),
                              jnp.where(incl, m[u][c2:, c2:], 0.0)], axis=1) for u in units]
    y = [a_lo[u] + _bdot(l_ak[u], v_hi[u]) for u in units]
    pw = l_ab
    for step in range(n_doublings):
        skip = (1 << step) if (1 << step) % 8 == 0 else 0

        def live_rows(x):
            return x if skip == 0 else jnp.concatenate([x[skip:chunk], x[chunk + skip:]], axis=0)

        def all_rows(x):
            if skip == 0:
                return x
            zero = jnp.zeros((skip, x.shape[1]), x.dtype)
            return jnp.concatenate([zero, x[:chunk - skip], zero, x[chunk - skip:]], axis=0)

        if step < n_doublings - 1:
            py = [all_rows(_bdot(live_rows(pw[u]), jnp.concatenate([y[u], pw[u]], axis=1)))
                  for u in units]
            pw = [py[u][:, lanes:] for u in units]
            y = [y[u] + py[u][:, :lanes] for u in units]
        else:
            y = [y[u] + all_rows(_bdot(live_rows(pw[u]), y[u])) for u in units]
    yv = [jnp.concatenate([y[u], v_hi[u]], axis=0) for u in units]
    gh = [_bdot_tn(yv[u], bkh_bd[u]) for u in units]
    lyv = [_bdot(l_rbk[u], yv[u]) for u in units]
    q_lo = [jnp.where(lo, r_lo[u] + lyv[u], 0.0) for u in units]
    q_bd = [jnp.where(own, q_lo[u] + pltpu.roll(q_lo[u], HEAD, axis=1), 0.0) for u in units]
    g_bd = [jnp.where(g_own, stack(gh[u][:HEAD]), 0.0) for u in units]
    zero_s = jnp.zeros((HEAD, lanes), F32)
    per_chunk = seqs * n_pairs
    state = [s_scr[unit[u][0], unit[u][1]] for u in range(per_chunk)]
    o = []
    for cc in range(cps):
        us = range(cc * per_chunk, (cc + 1) * per_chunk)
        o += [_bdot_nt(q_bd[u], jnp.concatenate([zero_s, state[u % per_chunk]], axis=0))
              for u in us]
        sg = [_bdot(state[u % per_chunk], g_bd[u]) for u in us]
        state = [state[u % per_chunk] * gam[u] + sg[u % per_chunk] + gh[u][HEAD:] for u in us]
    for u in range(per_chunk):
        s_scr[unit[u][0], unit[u][1]] = state[u]

    head_sum = g_own.astype(BF16)
    o = [o[u] + lyv[u] for u in units]
    o = [jnp.where(lo_c, pltpu.roll(o[u][:chunk], HEAD, axis=1), o[u][chunk:]) for u in units]
    o_hi = [o[u].astype(BF16) for u in units]
    mu = [(jnp.dot(o_hi[u], head_sum, preferred_element_type=F32)
           + _bdot(o[u] - o_hi[u].astype(F32), head_sum)) * (1.0 / HEAD) for u in units]
    d = [o[u] - mu[u] for u in units]
    var = [_bdot(d[u] * d[u], head_sum) * (1.0 / HEAD) for u in units]
    bonus = [_bdot(r_ref[unit[u][0], unit[u][3], unit[u][2]] * k_ref[unit[u][0], unit[u][3], unit[u][2]]
                   * rk_ref[:, unit[u][2]], head_sum) for u in units]
    for cc, q in [(cc, q) for cc in range(cps) for q in range(seqs)]:
        y_pairs = []
        for p in range(n_pairs):
            u = (cc * seqs + q) * n_pairs + p
            _, _, ps, rows = unit[u]
            gn = d[u] * lax.rsqrt(var[u] + GN_EPS) * lng_ref[:, ps] + lnb_ref[:, ps]
            y_pairs.append((gn + bonus[u] * v_ref[q, rows, ps]) * g_ref[q, rows, ps])
        y_ref[q, cc * chunk:(cc + 1) * chunk, :] = jnp.concatenate(y_pairs, axis=1).astype(BF16)

    @pl.when(c == pl.num_programs(1) - 1)
    def _():
        for q in range(seqs):
            for p in range(n_pairs):
                sp = s_scr[q, p]
                sout_ref[q, 2 * p] = sp[:, :HEAD]
                sout_ref[q, 2 * p + 1] = sp[:, HEAD:]


def _wkv(ops, s0, rk, lng, lnb, *, seq_len, chunk, seqs, cps):
    n, d_rwkv = ops[0].shape
    n_seq, n_heads = s0.shape[0], s0.shape[1]
    assert n_seq % seqs == 0 and seq_len % (cps * chunk) == 0
    ops = [o.reshape(n_seq, seq_len, d_rwkv) for o in ops]
    tok = pl.BlockSpec((seqs, cps * chunk, d_rwkv), lambda q, c: (q, c, 0))
    state = pl.BlockSpec((seqs, n_heads, HEAD, HEAD), lambda q, c: (q, 0, 0, 0))
    y, s_new = pl.pallas_call(
        functools.partial(_wkv_kernel, seqs=seqs, chunk=chunk, cps=cps, n_heads=n_heads),
        grid=(n_seq // seqs, seq_len // (cps * chunk)),
        in_specs=[tok] * 7 + [state] + [pl.BlockSpec((1, d_rwkv), lambda q, c: (0, 0))] * 3,
        out_specs=[tok, state],
        out_shape=[jax.ShapeDtypeStruct((n_seq, seq_len, d_rwkv), BF16),
                   jax.ShapeDtypeStruct(s0.shape, F32)],
        scratch_shapes=[pltpu.VMEM((seqs, n_heads // 2, HEAD, 2 * HEAD), F32)],
        compiler_params=pltpu.CompilerParams(
            dimension_semantics=("arbitrary", "arbitrary"),
            vmem_limit_bytes=VMEM_LIMIT_BYTES),
        name="wkv",
    )(*ops, s0, rk, lng, lnb)
    return y.reshape(n, d_rwkv), s_new


def _wkv_steps_kernel(r_ref, nkk_ref, b_ref, k_ref, v_ref, lw_ref, g_ref, s0_ref,
                      rk_ref, lng_ref, lnb_ref, y_ref, sout_ref, vt_scr, o_scr, *, n_t):
    n_seq = s0_ref.shape[-1]

    def feature_major(ref, t):
        return ref[t * n_seq:(t + 1) * n_seq, :].T

    for t in range(n_t):
        src = s0_ref if t == 0 else sout_ref
        nkk = feature_major(nkk_ref, t)
        beta = feature_major(b_ref, t)
        key = feature_major(k_ref, t)
        rec = feature_major(r_ref, t)
        decay = jnp.exp(feature_major(lw_ref, t))
        vt_scr[...] = feature_major(v_ref, t)
        for h in range(2):
            hs = slice(h * HEAD, (h + 1) * HEAD)
            nkk_h, beta_h, key_h, rec_h, decay_h = nkk[hs], beta[hs], key[hs], rec[hs], decay[hs]

            def one_row(i, carry):
                s = src[h, i]
                sa = jnp.sum(s * nkk_h, axis=0, keepdims=True)
                v_i = vt_scr[pl.ds(h * HEAD + i, 1), :]
                s = s * decay_h + sa * beta_h + v_i * key_h
                sout_ref[h, i] = s
                o_scr[pl.ds(h * HEAD + i, 1), :] = jnp.sum(s * rec_h, axis=0, keepdims=True)
                return carry

            lax.fori_loop(0, HEAD, one_row, 0, unroll=8)

        v_t = vt_scr[...]
        bonus_prod = rec * key * rk_ref[...]
        ys = []
        for h in range(2):
            hs = slice(h * HEAD, (h + 1) * HEAD)
            o = o_scr[hs, :]
            mu = jnp.mean(o, axis=0, keepdims=True)
            d = o - mu
            var = jnp.mean(d * d, axis=0, keepdims=True)
            gn = d * lax.rsqrt(var + GN_EPS) * lng_ref[hs, :] + lnb_ref[hs, :]
            bonus = jnp.sum(bonus_prod[hs], axis=0, keepdims=True)
            ys.append(gn + bonus * v_t[hs])
        y_t = jnp.concatenate(ys, axis=0) * feature_major(g_ref, t)
        y_ref[t * n_seq:(t + 1) * n_seq, :] = y_t.T.astype(y_ref.dtype)


def _wkv_steps(ops, s0, rk, lng, lnb, *, n_t):
    n, d_rwkv = ops[0].shape
    n_heads, n_seq = s0.shape[0], s0.shape[-1]
    lanes = 2 * HEAD
    assert n == n_t * n_seq and n_seq == lanes

    def bcast(a):
        return jnp.broadcast_to(a.reshape(d_rwkv, 1), (d_rwkv, n_seq))

    tok = pl.BlockSpec((n, lanes), lambda p: (0, p))
    state = pl.BlockSpec((2, HEAD, HEAD, n_seq), lambda p: (p, 0, 0, 0))
    par = pl.BlockSpec((lanes, n_seq), lambda p: (p, 0))
    return pl.pallas_call(
        functools.partial(_wkv_steps_kernel, n_t=n_t),
        grid=(n_heads // 2,),
        in_specs=[tok] * 7 + [state] + [par] * 3,
        out_specs=[tok, state],
        out_shape=[jax.ShapeDtypeStruct((n, d_rwkv), BF16),
                   jax.ShapeDtypeStruct(s0.shape, F32)],
        scratch_shapes=[pltpu.VMEM((lanes, n_seq), F32), pltpu.VMEM((lanes, n_seq), F32)],
        compiler_params=pltpu.CompilerParams(
            dimension_semantics=("arbitrary",), vmem_limit_bytes=VMEM_LIMIT_BYTES),
        name="wkv_steps",
    )(*ops, s0, bcast(rk), bcast(lng), bcast(lnb))


def _out_ffn_kernel(xa_ref, yra_ref, yca_ref, xb_ref, yrb_ref, ycb_ref, wo_r_ref, wo_c_ref,
                    g2_ref, b2_ref, wg_ref, wu_ref, wd_ref, g3_ref, b3_ref, oa_ref, ob_ref, *, alpha):
    last = pl.num_programs(0) - 1
    for x_ref, yr_ref, yc_ref, o_ref, cond in (
            (xa_ref, yra_ref, yca_ref, oa_ref, pl.program_id(0) < last),
            (xb_ref, yrb_ref, ycb_ref, ob_ref, pl.program_id(0) == last)):
        @pl.when(cond)
        def _(x_ref=x_ref, yr_ref=yr_ref, yc_ref=yc_ref, o_ref=o_ref):
            mix = jnp.dot(yr_ref[...], wo_r_ref[...], preferred_element_type=F32)
            mix = mix + jnp.dot(yc_ref[...], wo_c_ref[...], preferred_element_type=F32)

            def mixed_rows(sl):
                return _layer_norm(alpha * x_ref[sl, :] + mix[sl], g2_ref[...], b2_ref[...])

            _ffn_post_ln(mixed_rows, wg_ref, wu_ref, wd_ref, g3_ref, b3_ref, o_ref, alpha)


def _out_ffn(group_a, group_b, wo_r, wo_c, g2, b2, wg, wu, wd, g3, b3, alpha):
    d = group_a[0].shape[1]
    dff = wg.shape[1]
    d_rwkv, d_conv = group_a[1].shape[1], group_a[2].shape[1]
    grid, long_rows, short_rows = _two_group_grid(group_a[0].shape[0], group_b[0].shape[0])
    return pl.pallas_call(
        functools.partial(_out_ffn_kernel, alpha=alpha),
        grid=grid,
        in_specs=[long_rows(d), long_rows(d_rwkv), long_rows(d_conv),
                  short_rows(d), short_rows(d_rwkv), short_rows(d_conv),
                  _resident((d_rwkv, d)), _resident((d_conv, d)),
                  _resident((1, d)), _resident((1, d)),
                  _resident((d, dff)), _resident((d, dff)), _resident((dff, d)),
                  _resident((1, d)), _resident((1, d))],
        out_specs=[long_rows(d), short_rows(d, single_buffer=False)],
        out_shape=[jax.ShapeDtypeStruct(group_a[0].shape, F32),
                   jax.ShapeDtypeStruct(group_b[0].shape, F32)],
        compiler_params=pltpu.CompilerParams(
            dimension_semantics=("arbitrary",), vmem_limit_bytes=VMEM_LIMIT_BYTES),
        name="out_ffn",
    )(*group_a, *group_b, wo_r, wo_c, g2, b2, wg, wu, wd, g3, b3)


def _trunk_layer(xp, xs, p, alpha, *, tp, ts, wkv0_p, wkv0_s, init_s):
    x1p, x1s = _ffn_ln(xp, xs, p["ffn1_wg"], p["ffn1_wu"], p["ffn1_wd"], p["ln1_g"], p["ln1_b"],
                       alpha)
    pm_p = _premix(x1p, p, seq_len=tp, init=None)
    pm_s = _premix(x1s, p, seq_len=ts, init=init_s)
    yr_p, wkv_p = _wkv(pm_p[:7], wkv0_p, p["r_k"], p["lnx_g"], p["lnx_b"],
                       seq_len=tp, chunk=CHUNK, seqs=wkv0_p.shape[0], cps=CHUNKS_PER_STEP)
    yr_s, wkv_s = _wkv_steps(pm_s[:7], wkv0_s, p["r_k"], p["lnx_g"], p["lnx_b"], n_t=ts)
    d_rwkv = yr_p.shape[1]
    yp, ys = _out_ffn((x1p, yr_p, pm_p[7]), (x1s, yr_s, pm_s[7]),
                      p["w_out"][:d_rwkv], p["w_out"][d_rwkv:],
                      p["ln2_g"], p["ln2_b"], p["ffn2_wg"], p["ffn2_wu"], p["ffn2_wd"],
                      p["ln3_g"], p["ln3_b"], alpha)
    return (yp, wkv_p, pm_p[8], pm_p[9]), (ys, wkv_s, pm_s[8], pm_s[9])


def kernel(x_prompt, x_sample, state_wkv, state_shift, state_conv, ln1_g, ln1_b, ffn1_wg, ffn1_wu, ffn1_wd, w_in, mu_shift, w0, w_lora_up, a0, a_lora_up, g_lora_up, k_k, k_a, r_k, lnx_g, lnx_b, conv_w, w_out, ln2_g, ln2_b, ffn2_wg, ffn2_wu, ffn2_wd, ln3_g, ln3_b):
    depth = ln1_g.shape[0]
    bp, tp, d = x_prompt.shape
    bs, ts, _ = x_sample.shape
    n_heads = state_wkv.shape[2]
    alpha = (2.0 * depth) ** 0.25
    assert ts >= CONV_W - 1

    xp = x_prompt.reshape(bp * tp, d)
    xs = x_sample.transpose(1, 0, 2).reshape(ts * bs, d)

    outs = {k: [] for k in ("wkv_p", "shift_p", "conv_p", "wkv_s", "shift_s", "conv_s")}
    for l in range(depth):
        vec = lambda a: a[l].reshape(1, -1).astype(F32)
        p = {
            "ln1_g": vec(ln1_g), "ln1_b": vec(ln1_b),
            "ffn1_wg": ffn1_wg[l].astype(BF16), "ffn1_wu": ffn1_wu[l].astype(BF16),
            "ffn1_wd": ffn1_wd[l].astype(BF16),
            "w_in": w_in[l].astype(BF16), "mu_shift": vec(mu_shift), "w0": vec(w0),
            "w_lora_up": w_lora_up[l].astype(BF16), "a0": vec(a0),
            "a_lora_up": a_lora_up[l].astype(BF16), "g_lora_up": g_lora_up[l].astype(BF16),
            "k_k": vec(k_k), "k_a": vec(k_a), "r_k": vec(r_k),
            "lnx_g": vec(lnx_g), "lnx_b": vec(lnx_b),
            "conv_w": conv_w[l].astype(F32), "w_out": w_out[l].astype(BF16),
            "ln2_g": vec(ln2_g), "ln2_b": vec(ln2_b),
            "ffn2_wg": ffn2_wg[l].astype(BF16), "ffn2_wu": ffn2_wu[l].astype(BF16),
            "ffn2_wd": ffn2_wd[l].astype(BF16),
            "ln3_g": vec(ln3_g), "ln3_b": vec(ln3_b),
        }
        wkv0_p = jnp.zeros((bp, n_heads, HEAD, HEAD), F32)
        init = (state_shift[l].astype(F32), state_conv[l].astype(F32).transpose(1, 0, 2))
        wkv0_s = state_wkv[l].astype(F32).transpose(1, 2, 3, 0)
        (xp, wp, sp, cp), (xs, wsm, ps_s, u_s) = _trunk_layer(
            xp, xs, p, alpha, tp=tp, ts=ts, wkv0_p=wkv0_p, wkv0_s=wkv0_s, init_s=init)
        outs["wkv_p"].append(wp)
        outs["shift_p"].append(sp[:, 0, :])
        outs["conv_p"].append(cp)
        outs["wkv_s"].append(wsm.transpose(3, 0, 1, 2))
        outs["shift_s"].append(ps_s[(ts - 1) * bs:])
        outs["conv_s"].append(
            u_s[(ts - (CONV_W - 1)) * bs:].reshape(CONV_W - 1, bs, -1).transpose(1, 0, 2))

    y_prompt = xp.reshape(bp, tp, d)
    y_sample = xs.reshape(ts, bs, d).transpose(1, 0, 2)
    return (y_prompt, y_sample,
            jnp.stack(outs["wkv_p"]), jnp.stack(outs["shift_p"]), jnp.stack(outs["conv_p"]),
            jnp.stack(outs["wkv_s"]), jnp.stack(outs["shift_s"]), jnp.stack(outs["conv_s"]))
```

```python
import functools
import math

import jax
import jax.numpy as jnp
from jax import lax
from jax.experimental import pallas as pl
from jax.experimental.pallas import tpu as pltpu

F32 = jnp.float32
BF16 = jnp.bfloat16

HEAD = 64
LORA_W = 64
LORA_A = 64
LORA_G = 128
CONV_W = 3
LN_EPS = 1e-5
GN_EPS = 1e-5 * HEAD

VMEM_LIMIT_BYTES = 56 * 1024 * 1024
TOKEN_TILE = 512
FFN_TILE = 1024
FFN1_GROUP = 256
FFN2_GROUP = 512
CHUNK = 64
CHUNKS_PER_STEP = 1
WKV_SEQS = 4


def _resident(shape):
    nd = len(shape)
    return pl.BlockSpec(shape, lambda *_: (0,) * nd, pipeline_mode=pl.Buffered(1))


def _layer_norm(x, g, b):
    mu = jnp.mean(x, axis=-1, keepdims=True)
    d = x - mu
    var = jnp.mean(d * d, axis=-1, keepdims=True)
    return d * lax.rsqrt(var + LN_EPS) * g + b


def _ffn_post_ln(rows_of, wg_ref, wu_ref, wd_ref, g_ref, b_ref, o_ref, alpha, group):
    for s in range(o_ref.shape[0] // group):
        sl = slice(s * group, (s + 1) * group)
        xs = rows_of(sl)
        xb = xs.astype(BF16)
        g = jnp.dot(xb, wg_ref[...], preferred_element_type=F32)
        u = jnp.dot(xb, wu_ref[...], preferred_element_type=F32)
        h = (jax.nn.silu(g) * u).astype(BF16)
        y = alpha * xs + 0.5 * jnp.dot(h, wd_ref[...], preferred_element_type=F32)
        o_ref[sl, :] = _layer_norm(y, g_ref[...], b_ref[...])


def _two_group_grid(n_long, n_short):
    tm = FFN_TILE
    assert n_long % tm == 0 and n_short % max(FFN1_GROUP, FFN2_GROUP) == 0
    n_tiles = n_long // tm

    def long_rows(c):
        return pl.BlockSpec((tm, c), lambda i: (jnp.minimum(i, n_tiles - 1), 0))

    def short_rows(c, single_buffer=True):
        mode = dict(pipeline_mode=pl.Buffered(1)) if single_buffer else {}
        return pl.BlockSpec((n_short, c), lambda i: (0, 0), **mode)

    return (n_tiles + 1,), long_rows, short_rows


def _ffn_ln_kernel(xa_ref, xb_ref, wg_ref, wu_ref, wd_ref, g_ref, b_ref, oa_ref, ob_ref, *, alpha):
    last = pl.num_programs(0) - 1
    for x_ref, o_ref, cond in ((xa_ref, oa_ref, pl.program_id(0) < last),
                               (xb_ref, ob_ref, pl.program_id(0) == last)):
        @pl.when(cond)
        def _(x_ref=x_ref, o_ref=o_ref):
            _ffn_post_ln(lambda sl: x_ref[sl, :], wg_ref, wu_ref, wd_ref, g_ref, b_ref,
                         o_ref, alpha, FFN1_GROUP)


def _ffn_ln(xa, xb, wg, wu, wd, g, b, alpha):
    d = xa.shape[1]
    dff = wg.shape[1]
    grid, long_rows, short_rows = _two_group_grid(xa.shape[0], xb.shape[0])
    return pl.pallas_call(
        functools.partial(_ffn_ln_kernel, alpha=alpha),
        grid=grid,
        in_specs=[long_rows(d), short_rows(d), _resident((d, dff)), _resident((d, dff)),
                  _resident((dff, d)), _resident((1, d)), _resident((1, d))],
        out_specs=[long_rows(d), short_rows(d, single_buffer=False)],
        out_shape=[jax.ShapeDtypeStruct(xa.shape, F32), jax.ShapeDtypeStruct(xb.shape, F32)],
        compiler_params=pltpu.CompilerParams(
            dimension_semantics=("arbitrary",), vmem_limit_bytes=VMEM_LIMIT_BYTES),
        name="ffn_ln",
    )(xa, xb, wg, wu, wd, g, b)


HDR = 8


def _shift_rows(x, head, k):
    rolled = pltpu.roll(x, k, axis=0)
    row8 = lax.broadcasted_iota(jnp.int32, (HDR, 1), 0)
    top = jnp.where(row8 < k, pltpu.roll(head, k, axis=0), rolled[:HDR])
    return jnp.concatenate([top, rolled[HDR:]], axis=0)


class _PremixRefs:
    def __init__(self, refs, tiles_per_seq, full_out):
        refs = list(refs)
        (self.x, self.win, self.mu, self.w0, self.wup, self.a0, self.aup, self.gup, self.kk,
         self.ka, self.cw) = refs[:11]
        refs = refs[11:]
        if tiles_per_seq == 0:
            self.sh0, self.cv0 = refs[:2]
            refs = refs[2:]
        (self.r, self.nkk, self.b, self.k, self.v, self.lw, self.g, self.yc) = refs[:8]
        refs = refs[8:]
        if full_out:
            self.ps_out, self.u_out = refs[:2]
        else:
            self.shift, self.tail = refs[:2]
        self.bufs = refs[2:]


def _premix_gates(rf, buf, first, *, tiles_per_seq, d_rwkv, d_conv):
    p_shift = 3 * d_rwkv + LORA_W + LORA_A + LORA_G
    tm = rf.x.shape[0]
    ps = buf[HDR:, :p_shift]
    c_b = buf[HDR:, p_shift:p_shift + d_conv]
    u = buf[HDR:, p_shift + d_conv:p_shift + 2 * d_conv] * buf[HDR:, p_shift + 2 * d_conv:]

    if tiles_per_seq:
        head = jnp.where(first, 0.0, buf[0:HDR, :])
        head_u = head[:, p_shift + d_conv:p_shift + 2 * d_conv] * head[:, p_shift + 2 * d_conv:]
        prev = _shift_rows(ps, head[:, :p_shift], 1)
        um1 = _shift_rows(u, head_u, 1)
        um2 = _shift_rows(u, head_u, 2)
    else:
        n_seq = rf.sh0.shape[0]
        prev = jnp.concatenate([rf.sh0[...], ps[:tm - n_seq]], axis=0)
        hist = jnp.concatenate([rf.cv0[t] for t in range(CONV_W - 1)] + [u], axis=0)
        um2 = hist[:tm]
        um1 = hist[n_seq:n_seq + tm]

    cw = rf.cw[...]
    z = cw[0:1, :] * um2
    z = z + cw[1:2, :] * um1
    z = z + cw[2:3, :] * u
    rf.yc[...] = (c_b * z).astype(BF16)
    if tiles_per_seq:
        rf.shift[0] = ps[tm - 1:tm, :]
        rf.tail[0] = u[tm - (CONV_W - 1):tm, :]
    else:
        rf.ps_out[...] = ps
        rf.u_out[...] = u

    mix = ps + rf.mu[...] * (prev - ps)
    r = mix[:, 0:d_rwkv]
    k = mix[:, d_rwkv:2 * d_rwkv]
    v = mix[:, 2 * d_rwkv:3 * d_rwkv]
    o = 3 * d_rwkv
    wd = mix[:, o:o + LORA_W]
    ad = mix[:, o + LORA_W:o + LORA_W + LORA_A]
    gd = mix[:, o + LORA_W + LORA_A:p_shift]

    rf.r[...] = r
    kk = k * rf.kk[...]
    dot_in = (jnp.tanh(wd).astype(BF16), ad.astype(BF16), jax.nn.sigmoid(gd).astype(BF16),
              kk * kk)
    return k, v, kk, dot_in


def _premix_dots(rf, dot_in):
    tanh_wd, ad, sig_gd, kk_sq = dot_in
    w_pre = jnp.dot(tanh_wd, rf.wup[...], preferred_element_type=F32)
    a_pre = jnp.dot(ad, rf.aup[...], preferred_element_type=F32)
    rf.g[...] = jnp.dot(sig_gd, rf.gup[...], preferred_element_type=F32)
    low = lax.broadcasted_iota(jnp.int32, (1, 2 * HEAD), 1) < HEAD
    parts = []
    for c0 in range(0, kk_sq.shape[1], 2 * HEAD):
        x = kk_sq[:, c0:c0 + 2 * HEAD]
        s_lo = jnp.sum(jnp.where(low, x, 0.0), axis=-1, keepdims=True)
        s_hi = jnp.sum(jnp.where(low, 0.0, x), axis=-1, keepdims=True)
        parts.append(jnp.where(low, s_lo, s_hi))
    ss = jnp.concatenate(parts, axis=1)
    return w_pre, a_pre, ss


def _premix_finish(rf, gates, dots):
    k, v, kk, _ = gates
    w_pre, a_pre, ss = dots
    log_decay = -math.exp(-0.5) * jax.nn.sigmoid(rf.w0[...] + w_pre)
    a = jax.nn.sigmoid(rf.a0[...] + a_pre)
    kk = kk * lax.rsqrt(jnp.maximum(ss, 1e-24))
    k = k * (1.0 + (a - 1.0) * rf.ka[...])
    nkk = -kk
    b = kk * a
    rf.nkk[...] = nkk
    rf.b[...] = b
    rf.k[...] = k
    rf.v[...] = v
    rf.lw[...] = log_decay


def _premix_kernel(*refs, tiles_per_seq, d_rwkv, d_conv):
    rf = _PremixRefs(refs, tiles_per_seq, full_out=tiles_per_seq == 0)
    buf, = rf.bufs
    tm = rf.x.shape[0]
    i = pl.program_id(0)
    cfg = dict(tiles_per_seq=tiles_per_seq, d_rwkv=d_rwkv, d_conv=d_conv)

    if tiles_per_seq:
        @pl.when(i == 0)
        def _():
            buf[0:HDR, :] = jnp.zeros((HDR, buf.shape[1]), F32)

    buf[HDR:, :] = jnp.dot(rf.x[...].astype(BF16), rf.win[...], preferred_element_type=F32)
    first = (i % tiles_per_seq == 0) if tiles_per_seq else None
    gates = _premix_gates(rf, buf, first, **cfg)
    _premix_finish(rf, gates, _premix_dots(rf, gates[3]))
    if tiles_per_seq:
        buf[0:HDR, :] = buf[tm:tm + HDR, :]


def _premix(x1, p, *, seq_len, init=None):
    n, d = x1.shape
    tm = TOKEN_TILE
    d_rwkv = p["w0"].shape[1]
    d_conv = p["conv_w"].shape[1]
    p_shift = 3 * d_rwkv + LORA_W + LORA_A + LORA_G
    p_total = p_shift + 3 * d_conv
    n_seq = n // seq_len
    if init is None:
        assert seq_len % tm == 0
        tiles_per_seq = seq_len // tm
    else:
        assert n == tm and n_seq % HDR == 0
        tiles_per_seq = 0
    full_out = init is not None

    def row(c):
        return pl.BlockSpec((tm, c), lambda i: (i, 0))

    in_specs = [row(d), _resident((d, p_total)), _resident((1, p_shift)),
                _resident((1, d_rwkv)), _resident((LORA_W, d_rwkv)),
                _resident((1, d_rwkv)), _resident((LORA_A, d_rwkv)),
                _resident((LORA_G, d_rwkv)), _resident((1, d_rwkv)), _resident((1, d_rwkv)),
                _resident((CONV_W, d_conv))]
    args = [x1, p["w_in"], p["mu_shift"], p["w0"], p["w_lora_up"], p["a0"], p["a_lora_up"],
            p["g_lora_up"], p["k_k"], p["k_a"], p["conv_w"]]
    if init is not None:
        in_specs += [_resident((n_seq, p_shift)), _resident((CONV_W - 1, n_seq, d_conv))]
        args += list(init)

    out_specs = [row(d_rwkv)] * 7 + [row(d_conv)]
    out_shape = [jax.ShapeDtypeStruct((n, d_rwkv), F32)] * 7 + [
        jax.ShapeDtypeStruct((n, d_conv), BF16)]
    if full_out:
        out_specs += [row(p_shift), row(d_conv)]
        out_shape += [jax.ShapeDtypeStruct((n, p_shift), F32),
                      jax.ShapeDtypeStruct((n, d_conv), F32)]
    else:
        out_specs += [
            pl.BlockSpec((1, 1, p_shift), lambda i: (i // tiles_per_seq, 0, 0)),
            pl.BlockSpec((1, CONV_W - 1, d_conv), lambda i: (i // tiles_per_seq, 0, 0))]
        out_shape += [jax.ShapeDtypeStruct((n_seq, 1, p_shift), F32),
                      jax.ShapeDtypeStruct((n_seq, CONV_W - 1, d_conv), F32)]

    return pl.pallas_call(
        functools.partial(_premix_kernel, tiles_per_seq=tiles_per_seq, d_rwkv=d_rwkv,
                          d_conv=d_conv),
        grid=(n // tm,),
        in_specs=in_specs,
        out_specs=out_specs,
        out_shape=out_shape,
        scratch_shapes=[pltpu.VMEM((HDR + tm, p_total), F32)],
        compiler_params=pltpu.CompilerParams(
            dimension_semantics=("arbitrary",), vmem_limit_bytes=VMEM_LIMIT_BYTES),
        name="premix",
    )(*args)


def _bdot(a, b):
    return jnp.dot(a.astype(BF16), b.astype(BF16), preferred_element_type=F32)


def _bdot_nt(a, b):
    return lax.dot_general(a.astype(BF16), b.astype(BF16), (((1,), (1,)), ((), ())),
                           preferred_element_type=F32)


def _bdot_tn(a, b):
    return lax.dot_general(a.astype(BF16), b.astype(BF16), (((0,), (0,)), ((), ())),
                           preferred_element_type=F32)


def _wkv_kernel(r_ref, nkk_ref, b_ref, k_ref, v_ref, lw_ref, g_ref, s0_ref,
                rk_ref, lng_ref, lnb_ref, y_ref, sout_ref, s_scr, *, seqs, chunk, cps, n_heads):
    c = pl.program_id(1)
    n_pairs = n_heads // 2
    c2 = 2 * chunk
    lanes = 2 * HEAD

    @pl.when(c == 0)
    def _():
        for q in range(seqs):
            for p in range(n_pairs):
                s_scr[q, p] = jnp.concatenate([s0_ref[q, 2 * p], s0_ref[q, 2 * p + 1]], axis=1)

    t_row = lax.broadcasted_iota(jnp.int32, (chunk, 1), 0)
    n_doublings = chunk.bit_length() - 1

    ri = lax.broadcasted_iota(jnp.int32, (c2, c2), 0)
    ci = lax.broadcasted_iota(jnp.int32, (c2, c2), 1)
    same = (ri // chunk) == (ci // chunk)
    strict = same & (ri % chunk > ci % chunk)
    incl = same & (ri % chunk >= ci % chunk)
    row_head = lax.broadcasted_iota(jnp.int32, (c2, lanes), 0) // chunk
    lane_head = lax.broadcasted_iota(jnp.int32, (c2, lanes), 1) // HEAD
    own = row_head == lane_head
    lo = lane_head == 0
    g_own = (lax.broadcasted_iota(jnp.int32, (lanes, lanes), 0) // HEAD
             == lax.broadcasted_iota(jnp.int32, (lanes, lanes), 1) // HEAD)
    lo_c = lax.broadcasted_iota(jnp.int32, (chunk, lanes), 1) < HEAD

    def stack(x):
        return jnp.concatenate([x, x], axis=0)

    def bd(x):
        return jnp.where(own, stack(x), 0.0)

    def own_half(x, low):
        swapped = pltpu.roll(x, HEAD, axis=1)
        rows = jnp.concatenate([x, swapped] if low else [swapped, x], axis=0)
        return jnp.where(lo if low else ~lo, rows, 0.0)

    ar_bd, bk_st, a_lo, r_lo, v_hi, bkh_bd, gam, unit = [], [], [], [], [], [], [], []
    for cc, q in [(cc, q) for cc in range(cps) for q in range(seqs)]:
        rows = slice(cc * chunk, (cc + 1) * chunk)
        lw = lw_ref[q, rows, :]
        cum = lw
        for step in range(n_doublings):
            shift = 1 << step
            cum = cum + jnp.where(t_row >= shift, pltpu.roll(cum, shift, axis=0), 0.0)
        last = cum[chunk - 1:chunk, :]
        e_in = jnp.exp(cum)
        e_out = jnp.exp(-cum)
        e_tail = jnp.exp(last - cum)
        a_t = nkk_ref[q, rows, :] * jnp.exp(cum - lw)
        r_t = r_ref[q, rows, :] * e_in
        b_t = b_ref[q, rows, :] * e_out
        k_t = k_ref[q, rows, :] * e_out
        b_h = b_ref[q, rows, :] * e_tail
        k_h = k_ref[q, rows, :] * e_tail
        gamma = jnp.exp(last)
        v_all = v_ref[q, rows, :]
        for p in range(n_pairs):
            ps = slice(p * lanes, (p + 1) * lanes)
            ar_bd.append(jnp.concatenate([bd(a_t[:, ps]), bd(r_t[:, ps])], axis=0))
            bk_st.append(jnp.concatenate([stack(b_t[:, ps]), stack(k_t[:, ps])], axis=0))
            a_lo.append(own_half(a_t[:, ps], True))
            r_lo.append(own_half(r_t[:, ps], True))
            v_hi.append(own_half(v_all[:, ps], False))
            bkh_bd.append(jnp.concatenate([bd(b_h[:, ps]), bd(k_h[:, ps])], axis=0))
            gam.append(gamma[:, ps])
            unit.append((q, p, ps, rows))

    units = range(cps * seqs * n_pairs)
    m = [_bdot_nt(ar_bd[u], bk_st[u]) for u in units]
    l_ab = [jnp.where(strict, m[u][:c2, :c2], 0.0) for u in units]
    l_ak = [jnp.where(strict, m[u][:c2, c2:], 0.0) for u in units]
    l_rbk = [jnp.concatenate([jnp.where(incl, m[u][c2:, :c2], 0.0),
                              jnp.where(incl, m[u][c2:, c2:], 0.0)], axis=1) for u in units]
    y = [a_lo[u] + _bdot(l_ak[u], v_hi[u]) for u in units]
    pw = l_ab
    for step in range(n_doublings):
        skip = (1 << step) if (1 << step) % 8 == 0 else 0

        def live_rows(x):
            return x if skip == 0 else jnp.concatenate([x[skip:chunk], x[chunk + skip:]], axis=0)

        def all_rows(x):
            if skip == 0:
                return x
            zero = jnp.zeros((skip, x.shape[1]), x.dtype)
            return jnp.concatenate([zero, x[:chunk - skip], zero, x[chunk - skip:]], axis=0)

        if step < n_doublings - 1:
            py = [all_rows(_bdot(live_rows(pw[u]), jnp.concatenate([y[u], pw[u]], axis=1)))
                  for u in units]
            pw = [py[u][:, lanes:] for u in units]
            y = [y[u] + py[u][:, :lanes] for u in units]
        else:
            y = [y[u] + all_rows(_bdot(live_rows(pw[u]), y[u])) for u in units]
    yv = [jnp.concatenate([y[u], v_hi[u]], axis=0) for u in units]
    gh = [_bdot_tn(yv[u], bkh_bd[u]) for u in units]
    lyv = [_bdot(l_rbk[u], yv[u]) for u in units]
    q_lo = [jnp.where(lo, r_lo[u] + lyv[u], 0.0) for u in units]
    q_bd = [jnp.where(own, q_lo[u] + pltpu.roll(q_lo[u], HEAD, axis=1), 0.0) for u in units]
    g_bd = [jnp.where(g_own, stack(gh[u][:HEAD]), 0.0) for u in units]
    zero_s = jnp.zeros((HEAD, lanes), F32)
    per_chunk = seqs * n_pairs
    state = [s_scr[unit[u][0], unit[u][1]] for u in range(per_chunk)]
    o = []
    for cc in range(cps):
        us = range(cc * per_chunk, (cc + 1) * per_chunk)
        o += [_bdot_nt(q_bd[u], jnp.concatenate([zero_s, state[u % per_chunk]], axis=0))
              for u in us]
        sg = [_bdot(state[u % per_chunk], g_bd[u]) for u in us]
        state = [state[u % per_chunk] * gam[u] + sg[u % per_chunk] + gh[u][HEAD:] for u in us]
    for u in range(per_chunk):
        s_scr[unit[u][0], unit[u][1]] = state[u]

    head_sum = g_own.astype(BF16)
    o = [o[u] + lyv[u] for u in units]
    o = [jnp.where(lo_c, pltpu.roll(o[u][:chunk], HEAD, axis=1), o[u][chunk:]) for u in units]
    o_hi = [o[u].astype(BF16) for u in units]
    mu = [(jnp.dot(o_hi[u], head_sum, preferred_element_type=F32)
           + _bdot(o[u] - o_hi[u].astype(F32), head_sum)) * (1.0 / HEAD) for u in units]
    d = [o[u] - mu[u] for u in units]
    var = [_bdot(d[u] * d[u], head_sum) * (1.0 / HEAD) for u in units]
    bonus = [_bdot(r_ref[unit[u][0], unit[u][3], unit[u][2]] * k_ref[unit[u][0], unit[u][3], unit[u][2]]
                   * rk_ref[:, unit[u][2]], head_sum) for u in units]
    for cc, q in [(cc, q) for cc in range(cps) for q in range(seqs)]:
        y_pairs = []
        for p in range(n_pairs):
            u = (cc * seqs + q) * n_pairs + p
            _, _, ps, rows = unit[u]
            gn = d[u] * lax.rsqrt(var[u] + GN_EPS) * lng_ref[:, ps] + lnb_ref[:, ps]
            y_pairs.append((gn + bonus[u] * v_ref[q, rows, ps]) * g_ref[q, rows, ps])
        y_ref[q, cc * chunk:(cc + 1) * chunk, :] = jnp.concatenate(y_pairs, axis=1).astype(BF16)

    @pl.when(c == pl.num_programs(1) - 1)
    def _():
        for q in range(seqs):
            for p in range(n_pairs):
                sp = s_scr[q, p]
                sout_ref[q, 2 * p] = sp[:, :HEAD]
                sout_ref[q, 2 * p + 1] = sp[:, HEAD:]


def _wkv(ops, s0, rk, lng, lnb, *, seq_len, chunk, seqs, cps):
    n, d_rwkv = ops[0].shape
    n_seq, n_heads = s0.shape[0], s0.shape[1]
    assert n_seq % seqs == 0 and seq_len % (cps * chunk) == 0
    ops = [o.reshape(n_seq, seq_len, d_rwkv) for o in ops]
    tok = pl.BlockSpec((seqs, cps * chunk, d_rwkv), lambda q, c: (q, c, 0))
    state = pl.BlockSpec((seqs, n_heads, HEAD, HEAD), lambda q, c: (q, 0, 0, 0))
    y, s_new = pl.pallas_call(
        functools.partial(_wkv_kernel, seqs=seqs, chunk=chunk, cps=cps, n_heads=n_heads),
        grid=(n_seq // seqs, seq_len // (cps * chunk)),
        in_specs=[tok] * 7 + [state] + [pl.BlockSpec((1, d_rwkv), lambda q, c: (0, 0))] * 3,
        out_specs=[tok, state],
        out_shape=[jax.ShapeDtypeStruct((n_seq, seq_len, d_rwkv), BF16),
                   jax.ShapeDtypeStruct(s0.shape, F32)],
        scratch_shapes=[pltpu.VMEM((seqs, n_heads // 2, HEAD, 2 * HEAD), F32)],
        compiler_params=pltpu.CompilerParams(
            dimension_semantics=("arbitrary", "arbitrary"),
            vmem_limit_bytes=VMEM_LIMIT_BYTES),
        name="wkv",
    )(*ops, s0, rk, lng, lnb)
    return y.reshape(n, d_rwkv), s_new


def _wkv_steps_kernel(r_ref, nkk_ref, b_ref, k_ref, v_ref, lw_ref, g_ref, s0_ref,
                      rk_ref, lng_ref, lnb_ref, y_ref, sout_ref, vt_scr, o_scr, *, n_t):
    n_seq = s0_ref.shape[-1]

    def feature_major(ref, t):
        return ref[t * n_seq:(t + 1) * n_seq, :].T

    for t in range(n_t):
        src = s0_ref if t == 0 else sout_ref
        nkk = feature_major(nkk_ref, t)
        beta = feature_major(b_ref, t)
        key = feature_major(k_ref, t)
        rec = feature_major(r_ref, t)
        decay = jnp.exp(feature_major(lw_ref, t))
        vt_scr[...] = feature_major(v_ref, t)
        for h in range(2):
            hs = slice(h * HEAD, (h + 1) * HEAD)
            nkk_h, beta_h, key_h, rec_h, decay_h = nkk[hs], beta[hs], key[hs], rec[hs], decay[hs]

            def one_row(i, carry):
                s = src[h, i]
                sa = jnp.sum(s * nkk_h, axis=0, keepdims=True)
                v_i = vt_scr[pl.ds(h * HEAD + i, 1), :]
                s = s * decay_h + sa * beta_h + v_i * key_h
                sout_ref[h, i] = s
                o_scr[pl.ds(h * HEAD + i, 1), :] = jnp.sum(s * rec_h, axis=0, keepdims=True)
                return carry

            lax.fori_loop(0, HEAD, one_row, 0, unroll=8)

        v_t = vt_scr[...]
        bonus_prod = rec * key * rk_ref[...]
        ys = []
        for h in range(2):
            hs = slice(h * HEAD, (h + 1) * HEAD)
            o = o_scr[hs, :]
            mu = jnp.mean(o, axis=0, keepdims=True)
            d = o - mu
            var = jnp.mean(d * d, axis=0, keepdims=True)
            gn = d * lax.rsqrt(var + GN_EPS) * lng_ref[hs, :] + lnb_ref[hs, :]
            bonus = jnp.sum(bonus_prod[hs], axis=0, keepdims=True)
            ys.append(gn + bonus * v_t[hs])
        y_t = jnp.concatenate(ys, axis=0) * feature_major(g_ref, t)
        y_ref[t * n_seq:(t + 1) * n_seq, :] = y_t.T.astype(y_ref.dtype)


def _wkv_steps(ops, s0, rk, lng, lnb, *, n_t):
    n, d_rwkv = ops[0].shape
    n_heads, n_seq = s0.shape[0], s0.shape[-1]
    lanes = 2 * HEAD
    assert n == n_t * n_seq and n_seq == lanes

    def bcast(a):
        return jnp.broadcast_to(a.reshape(d_rwkv, 1), (d_rwkv, n_seq))

    tok = pl.BlockSpec((n, lanes), lambda p: (0, p))
    state = pl.BlockSpec((2, HEAD, HEAD, n_seq), lambda p: (p, 0, 0, 0))
    par = pl.BlockSpec((lanes, n_seq), lambda p: (p, 0))
    return pl.pallas_call(
        functools.partial(_wkv_steps_kernel, n_t=n_t),
        grid=(n_heads // 2,),
        in_specs=[tok] * 7 + [state] + [par] * 3,
        out_specs=[tok, state],
        out_shape=[jax.ShapeDtypeStruct((n, d_rwkv), BF16),
                   jax.ShapeDtypeStruct(s0.shape, F32)],
        scratch_shapes=[pltpu.VMEM((lanes, n_seq), F32), pltpu.VMEM((lanes, n_seq), F32)],
        compiler_params=pltpu.CompilerParams(
            dimension_semantics=("arbitrary",), vmem_limit_bytes=VMEM_LIMIT_BYTES),
        name="wkv_steps",
    )(*ops, s0, bcast(rk), bcast(lng), bcast(lnb))


def _out_ffn_kernel(xa_ref, yra_ref, yca_ref, xb_ref, yrb_ref, ycb_ref, wo_r_ref, wo_c_ref,
                    g2_ref, b2_ref, wg_ref, wu_ref, wd_ref, g3_ref, b3_ref, oa_ref, ob_ref, *, alpha):
    last = pl.num_programs(0) - 1
    for x_ref, yr_ref, yc_ref, o_ref, cond in (
            (xa_ref, yra_ref, yca_ref, oa_ref, pl.program_id(0) < last),
            (xb_ref, yrb_ref, ycb_ref, ob_ref, pl.program_id(0) == last)):
        @pl.when(cond)
        def _(x_ref=x_ref, yr_ref=yr_ref, yc_ref=yc_ref, o_ref=o_ref):
            mix = jnp.dot(yr_ref[...], wo_r_ref[...], preferred_element_type=F32)
            mix = mix + jnp.dot(yc_ref[...], wo_c_ref[...], preferred_element_type=F32)

            def mixed_rows(sl):
                return _layer_norm(alpha * x_ref[sl, :] + mix[sl], g2_ref[...], b2_ref[...])

            _ffn_post_ln(mixed_rows, wg_ref, wu_ref, wd_ref, g3_ref, b3_ref, o_ref, alpha,
                         FFN2_GROUP)


def _out_ffn(group_a, group_b, wo_r, wo_c, g2, b2, wg, wu, wd, g3, b3, alpha):
    d = group_a[0].shape[1]
    dff = wg.shape[1]
    d_rwkv, d_conv = group_a[1].shape[1], group_a[2].shape[1]
    grid, long_rows, short_rows = _two_group_grid(group_a[0].shape[0], group_b[0].shape[0])
    return pl.pallas_call(
        functools.partial(_out_ffn_kernel, alpha=alpha),
        grid=grid,
        in_specs=[long_rows(d), long_rows(d_rwkv), long_rows(d_conv),
                  short_rows(d), short_rows(d_rwkv), short_rows(d_conv),
                  _resident((d_rwkv, d)), _resident((d_conv, d)),
                  _resident((1, d)), _resident((1, d)),
                  _resident((d, dff)), _resident((d, dff)), _resident((dff, d)),
                  _resident((1, d)), _resident((1, d))],
        out_specs=[long_rows(d), short_rows(d, single_buffer=False)],
        out_shape=[jax.ShapeDtypeStruct(group_a[0].shape, F32),
                   jax.ShapeDtypeStruct(group_b[0].shape, F32)],
        compiler_params=pltpu.CompilerParams(
            dimension_semantics=("arbitrary",), vmem_limit_bytes=VMEM_LIMIT_BYTES),
        name="out_ffn",
    )(*group_a, *group_b, wo_r, wo_c, g2, b2, wg, wu, wd, g3, b3)


def _trunk_layer(xp, xs, p, alpha, *, tp, ts, wkv0_p, wkv0_s, init_s):
    x1p, x1s = _ffn_ln(xp, xs, p["ffn1_wg"], p["ffn1_wu"], p["ffn1_wd"], p["ln1_g"], p["ln1_b"],
                       alpha)
    pm_p = _premix(x1p, p, seq_len=tp, init=None)
    pm_s = _premix(x1s, p, seq_len=ts, init=init_s)
    yr_p, wkv_p = _wkv(pm_p[:7], wkv0_p, p["r_k"], p["lnx_g"], p["lnx_b"],
                       seq_len=tp, chunk=CHUNK, seqs=WKV_SEQS, cps=CHUNKS_PER_STEP)
    yr_s, wkv_s = _wkv_steps(pm_s[:7], wkv0_s, p["r_k"], p["lnx_g"], p["lnx_b"], n_t=ts)
    d_rwkv = yr_p.shape[1]
    yp, ys = _out_ffn((x1p, yr_p, pm_p[7]), (x1s, yr_s, pm_s[7]),
                      p["w_out"][:d_rwkv], p["w_out"][d_rwkv:],
                      p["ln2_g"], p["ln2_b"], p["ffn2_wg"], p["ffn2_wu"], p["ffn2_wd"],
                      p["ln3_g"], p["ln3_b"], alpha)
    return (yp, wkv_p, pm_p[8], pm_p[9]), (ys, wkv_s, pm_s[8], pm_s[9])


def kernel(x_prompt, x_sample, state_wkv, state_shift, state_conv, ln1_g, ln1_b, ffn1_wg, ffn1_wu, ffn1_wd, w_in, mu_shift, w0, w_lora_up, a0, a_lora_up, g_lora_up, k_k, k_a, r_k, lnx_g, lnx_b, conv_w, w_out, ln2_g, ln2_b, ffn2_wg, ffn2_wu, ffn2_wd, ln3_g, ln3_b):
    depth = ln1_g.shape[0]
    bp, tp, d = x_prompt.shape
    bs, ts, _ = x_sample.shape
    n_heads = state_wkv.shape[2]
    alpha = (2.0 * depth) ** 0.25
    assert ts >= CONV_W - 1

    xp = x_prompt.reshape(bp * tp, d)
    xs = x_sample.transpose(1, 0, 2).reshape(ts * bs, d)

    outs = {k: [] for k in ("wkv_p", "shift_p", "conv_p", "wkv_s", "shift_s", "conv_s")}
    for l in range(depth):
        vec = lambda a: a[l].reshape(1, -1).astype(F32)
        p = {
            "ln1_g": vec(ln1_g), "ln1_b": vec(ln1_b),
            "ffn1_wg": ffn1_wg[l].astype(BF16), "ffn1_wu": ffn1_wu[l].astype(BF16),
            "ffn1_wd": ffn1_wd[l].astype(BF16),
            "w_in": w_in[l].astype(BF16), "mu_shift": vec(mu_shift), "w0": vec(w0),
            "w_lora_up": w_lora_up[l].astype(BF16), "a0": vec(a0),
            "a_lora_up": a_lora_up[l].astype(BF16), "g_lora_up": g_lora_up[l].astype(BF16),
            "k_k": vec(k_k), "k_a": vec(k_a), "r_k": vec(r_k),
            "lnx_g": vec(lnx_g), "lnx_b": vec(lnx_b),
            "conv_w": conv_w[l].astype(F32), "w_out": w_out[l].astype(BF16),
            "ln2_g": vec(ln2_g), "ln2_b": vec(ln2_b),
            "ffn2_wg": ffn2_wg[l].astype(BF16), "ffn2_wu": ffn2_wu[l].astype(BF16),
            "ffn2_wd": ffn2_wd[l].astype(BF16),
            "ln3_g": vec(ln3_g), "ln3_b": vec(ln3_b),
        }
        wkv0_p = jnp.zeros((bp, n_heads, HEAD, HEAD), F32)
        init = (state_shift[l].astype(F32), state_conv[l].astype(F32).transpose(1, 0, 2))
        wkv0_s = state_wkv[l].astype(F32).transpose(1, 2, 3, 0)
        (xp, wp, sp, cp), (xs, wsm, ps_s, u_s) = _trunk_layer(
            xp, xs, p, alpha, tp=tp, ts=ts, wkv0_p=wkv0_p, wkv0_s=wkv0_s, init_s=init)
        outs["wkv_p"].append(wp)
        outs["shift_p"].append(sp[:, 0, :])
        outs["conv_p"].append(cp)
        outs["wkv_s"].append(wsm.transpose(3, 0, 1, 2))
        outs["shift_s"].append(ps_s[(ts - 1) * bs:])
        outs["conv_s"].append(
            u_s[(ts - (CONV_W - 1)) * bs:].reshape(CONV_W - 1, bs, -1).transpose(1, 0, 2))

    y_prompt = xp.reshape(bp, tp, d)
    y_sample = xs.reshape(ts, bs, d).transpose(1, 0, 2)
    return (y_prompt, y_sample,
            jnp.stack(outs["wkv_p"]), jnp.stack(outs["shift_p"]), jnp.stack(outs["conv_p"]),
            jnp.stack(outs["wkv_s"]), jnp.stack(outs["shift_s"]), jnp.stack(outs["conv_s"]))
```

```python
import functools
import math

import jax
import jax.numpy as jnp
from jax import lax
from jax.experimental import pallas as pl
from jax.experimental.pallas import tpu as pltpu

F32 = jnp.float32
BF16 = jnp.bfloat16

HEAD = 64
LORA_W = 64
LORA_A = 64
LORA_G = 128
CONV_W = 3
LN_EPS = 1e-5
GN_EPS = 1e-5 * HEAD

VMEM_LIMIT_BYTES = 56 * 1024 * 1024
TOKEN_TILE = 512
FFN_TILE = 1024
FFN1_GROUP = 128
FFN2_GROUP = 512
CHUNK = 64
CHUNKS_PER_STEP = 1


def _resident(shape):
    nd = len(shape)
    return pl.BlockSpec(shape, lambda *_: (0,) * nd, pipeline_mode=pl.Buffered(1))


def _layer_norm(x, g, b):
    mu = jnp.mean(x, axis=-1, keepdims=True)
    d = x - mu
    var = jnp.mean(d * d, axis=-1, keepdims=True)
    return d * lax.rsqrt(var + LN_EPS) * g + b


def _ffn_post_ln(rows_of, wg_ref, wu_ref, wd_ref, g_ref, b_ref, o_ref, alpha, group):
    for s in range(o_ref.shape[0] // group):
        sl = slice(s * group, (s + 1) * group)
        xs = rows_of(sl)
        xb = xs.astype(BF16)
        g = jnp.dot(xb, wg_ref[...], preferred_element_type=F32)
        u = jnp.dot(xb, wu_ref[...], preferred_element_type=F32)
        h = (jax.nn.silu(g) * u).astype(BF16)
        y = alpha * xs + 0.5 * jnp.dot(h, wd_ref[...], preferred_element_type=F32)
        o_ref[sl, :] = _layer_norm(y, g_ref[...], b_ref[...])


def _two_group_grid(n_long, n_short):
    tm = FFN_TILE
    assert n_long % tm == 0 and n_short % max(FFN1_GROUP, FFN2_GROUP) == 0
    n_tiles = n_long // tm

    def long_rows(c):
        return pl.BlockSpec((tm, c), lambda i: (jnp.minimum(i, n_tiles - 1), 0))

    def short_rows(c, single_buffer=True):
        mode = dict(pipeline_mode=pl.Buffered(1)) if single_buffer else {}
        return pl.BlockSpec((n_short, c), lambda i: (0, 0), **mode)

    return (n_tiles + 1,), long_rows, short_rows


def _ffn_ln_kernel(xa_ref, xb_ref, wg_ref, wu_ref, wd_ref, g_ref, b_ref, oa_ref, ob_ref, *, alpha):
    last = pl.num_programs(0) - 1
    for x_ref, o_ref, cond in ((xa_ref, oa_ref, pl.program_id(0) < last),
                               (xb_ref, ob_ref, pl.program_id(0) == last)):
        @pl.when(cond)
        def _(x_ref=x_ref, o_ref=o_ref):
            _ffn_post_ln(lambda sl: x_ref[sl, :], wg_ref, wu_ref, wd_ref, g_ref, b_ref,
                         o_ref, alpha, FFN1_GROUP)


def _ffn_ln(xa, xb, wg, wu, wd, g, b, alpha):
    d = xa.shape[1]
    dff = wg.shape[1]
    grid, long_rows, short_rows = _two_group_grid(xa.shape[0], xb.shape[0])
    return pl.pallas_call(
        functools.partial(_ffn_ln_kernel, alpha=alpha),
        grid=grid,
        in_specs=[long_rows(d), short_rows(d), _resident((d, dff)), _resident((d, dff)),
                  _resident((dff, d)), _resident((1, d)), _resident((1, d))],
        out_specs=[long_rows(d), short_rows(d, single_buffer=False)],
        out_shape=[jax.ShapeDtypeStruct(xa.shape, F32), jax.ShapeDtypeStruct(xb.shape, F32)],
        compiler_params=pltpu.CompilerParams(
            dimension_semantics=("arbitrary",), vmem_limit_bytes=VMEM_LIMIT_BYTES),
        name="ffn_ln",
    )(xa, xb, wg, wu, wd, g, b)


HDR = 8


def _shift_rows(x, head, k):
    rolled = pltpu.roll(x, k, axis=0)
    row8 = lax.broadcasted_iota(jnp.int32, (HDR, 1), 0)
    top = jnp.where(row8 < k, pltpu.roll(head, k, axis=0), rolled[:HDR])
    return jnp.concatenate([top, rolled[HDR:]], axis=0)


class _PremixRefs:
    def __init__(self, refs, tiles_per_seq, full_out):
        refs = list(refs)
        (self.x, self.win, self.mu, self.w0, self.wup, self.a0, self.aup, self.gup, self.kk,
         self.ka, self.cw) = refs[:11]
        refs = refs[11:]
        if tiles_per_seq == 0:
            self.sh0, self.cv0 = refs[:2]
            refs = refs[2:]
        (self.r, self.nkk, self.b, self.k, self.v, self.lw, self.g, self.yc) = refs[:8]
        refs = refs[8:]
        if full_out:
            self.ps_out, self.u_out = refs[:2]
        else:
            self.shift, self.tail = refs[:2]
        self.bufs = refs[2:]


def _premix_gates(rf, buf, first, *, tiles_per_seq, d_rwkv, d_conv):
    p_shift = 3 * d_rwkv + LORA_W + LORA_A + LORA_G
    tm = rf.x.shape[0]
    ps = buf[HDR:, :p_shift]
    c_b = buf[HDR:, p_shift:p_shift + d_conv]
    u = buf[HDR:, p_shift + d_conv:p_shift + 2 * d_conv] * buf[HDR:, p_shift + 2 * d_conv:]

    if tiles_per_seq:
        head = jnp.where(first, 0.0, buf[0:HDR, :])
        head_u = head[:, p_shift + d_conv:p_shift + 2 * d_conv] * head[:, p_shift + 2 * d_conv:]
        prev = _shift_rows(ps, head[:, :p_shift], 1)
        um1 = _shift_rows(u, head_u, 1)
        um2 = _shift_rows(u, head_u, 2)
    else:
        n_seq = rf.sh0.shape[0]
        prev = jnp.concatenate([rf.sh0[...], ps[:tm - n_seq]], axis=0)
        hist = jnp.concatenate([rf.cv0[t] for t in range(CONV_W - 1)] + [u], axis=0)
        um2 = hist[:tm]
        um1 = hist[n_seq:n_seq + tm]

    cw = rf.cw[...]
    z = cw[0:1, :] * um2
    z = z + cw[1:2, :] * um1
    z = z + cw[2:3, :] * u
    rf.yc[...] = (c_b * z).astype(BF16)
    if tiles_per_seq:
        rf.shift[0] = ps[tm - 1:tm, :]
        rf.tail[0] = u[tm - (CONV_W - 1):tm, :]
    else:
        rf.ps_out[...] = ps
        rf.u_out[...] = u

    mix = ps + rf.mu[...] * (prev - ps)
    r = mix[:, 0:d_rwkv]
    k = mix[:, d_rwkv:2 * d_rwkv]
    v = mix[:, 2 * d_rwkv:3 * d_rwkv]
    o = 3 * d_rwkv
    wd = mix[:, o:o + LORA_W]
    ad = mix[:, o + LORA_W:o + LORA_W + LORA_A]
    gd = mix[:, o + LORA_W + LORA_A:p_shift]

    rf.r[...] = r
    kk = k * rf.kk[...]
    dot_in = (jnp.tanh(wd).astype(BF16), ad.astype(BF16), jax.nn.sigmoid(gd).astype(BF16),
              kk * kk)
    return k, v, kk, dot_in


def _premix_dots(rf, dot_in):
    tanh_wd, ad, sig_gd, kk_sq = dot_in
    w_pre = jnp.dot(tanh_wd, rf.wup[...], preferred_element_type=F32)
    a_pre = jnp.dot(ad, rf.aup[...], preferred_element_type=F32)
    rf.g[...] = jnp.dot(sig_gd, rf.gup[...], preferred_element_type=F32)
    low = lax.broadcasted_iota(jnp.int32, (1, 2 * HEAD), 1) < HEAD
    parts = []
    for c0 in range(0, kk_sq.shape[1], 2 * HEAD):
        x = kk_sq[:, c0:c0 + 2 * HEAD]
        s_lo = jnp.sum(jnp.where(low, x, 0.0), axis=-1, keepdims=True)
        s_hi = jnp.sum(jnp.where(low, 0.0, x), axis=-1, keepdims=True)
        parts.append(jnp.where(low, s_lo, s_hi))
    ss = jnp.concatenate(parts, axis=1)
    return w_pre, a_pre, ss


def _premix_finish(rf, gates, dots):
    k, v, kk, _ = gates
    w_pre, a_pre, ss = dots
    log_decay = -math.exp(-0.5) * jax.nn.sigmoid(rf.w0[...] + w_pre)
    a = jax.nn.sigmoid(rf.a0[...] + a_pre)
    kk = kk * lax.rsqrt(jnp.maximum(ss, 1e-24))
    k = k * (1.0 + (a - 1.0) * rf.ka[...])
    nkk = -kk
    b = kk * a
    rf.nkk[...] = nkk
    rf.b[...] = b
    rf.k[...] = k
    rf.v[...] = v
    rf.lw[...] = log_decay


def _premix_kernel(*refs, tiles_per_seq, d_rwkv, d_conv):
    rf = _PremixRefs(refs, tiles_per_seq, full_out=tiles_per_seq == 0)
    buf, = rf.bufs
    tm = rf.x.shape[0]
    i = pl.program_id(0)
    cfg = dict(tiles_per_seq=tiles_per_seq, d_rwkv=d_rwkv, d_conv=d_conv)

    if tiles_per_seq:
        @pl.when(i == 0)
        def _():
            buf[0:HDR, :] = jnp.zeros((HDR, buf.shape[1]), F32)

    buf[HDR:, :] = jnp.dot(rf.x[...].astype(BF16), rf.win[...], preferred_element_type=F32)
    first = (i % tiles_per_seq == 0) if tiles_per_seq else None
    gates = _premix_gates(rf, buf, first, **cfg)
    _premix_finish(rf, gates, _premix_dots(rf, gates[3]))
    if tiles_per_seq:
        buf[0:HDR, :] = buf[tm:tm + HDR, :]


def _premix(x1, p, *, seq_len, init=None):
    n, d = x1.shape
    tm = TOKEN_TILE
    d_rwkv = p["w0"].shape[1]
    d_conv = p["conv_w"].shape[1]
    p_shift = 3 * d_rwkv + LORA_W + LORA_A + LORA_G
    p_total = p_shift + 3 * d_conv
    n_seq = n // seq_len
    if init is None:
        assert seq_len % tm == 0
        tiles_per_seq = seq_len // tm
    else:
        assert n == tm and n_seq % HDR == 0
        tiles_per_seq = 0
    full_out = init is not None

    def row(c):
        return pl.BlockSpec((tm, c), lambda i: (i, 0))

    in_specs = [row(d), _resident((d, p_total)), _resident((1, p_shift)),
                _resident((1, d_rwkv)), _resident((LORA_W, d_rwkv)),
                _resident((1, d_rwkv)), _resident((LORA_A, d_rwkv)),
                _resident((LORA_G, d_rwkv)), _resident((1, d_rwkv)), _resident((1, d_rwkv)),
                _resident((CONV_W, d_conv))]
    args = [x1, p["w_in"], p["mu_shift"], p["w0"], p["w_lora_up"], p["a0"], p["a_lora_up"],
            p["g_lora_up"], p["k_k"], p["k_a"], p["conv_w"]]
    if init is not None:
        in_specs += [_resident((n_seq, p_shift)), _resident((CONV_W - 1, n_seq, d_conv))]
        args += list(init)

    out_specs = [row(d_rwkv)] * 7 + [row(d_conv)]
    out_shape = [jax.ShapeDtypeStruct((n, d_rwkv), F32)] * 7 + [
        jax.ShapeDtypeStruct((n, d_conv), BF16)]
    if full_out:
        out_specs += [row(p_shift), row(d_conv)]
        out_shape += [jax.ShapeDtypeStruct((n, p_shift), F32),
                      jax.ShapeDtypeStruct((n, d_conv), F32)]
    else:
        out_specs += [
            pl.BlockSpec((1, 1, p_shift), lambda i: (i // tiles_per_seq, 0, 0)),
            pl.BlockSpec((1, CONV_W - 1, d_conv), lambda i: (i // tiles_per_seq, 0, 0))]
        out_shape += [jax.ShapeDtypeStruct((n_seq, 1, p_shift), F32),
                      jax.ShapeDtypeStruct((n_seq, CONV_W - 1, d_conv), F32)]

    return pl.pallas_call(
        functools.partial(_premix_kernel, tiles_per_seq=tiles_per_seq, d_rwkv=d_rwkv,
                          d_conv=d_conv),
        grid=(n // tm,),
        in_specs=in_specs,
        out_specs=out_specs,
        out_shape=out_shape,
        scratch_shapes=[pltpu.VMEM((HDR + tm, p_total), F32)],
        compiler_params=pltpu.CompilerParams(
            dimension_semantics=("arbitrary",), vmem_limit_bytes=VMEM_LIMIT_BYTES),
        name="premix",
    )(*args)


def _bdot(a, b):
    return jnp.dot(a.astype(BF16), b.astype(BF16), preferred_element_type=F32)


def _bdot_nt(a, b):
    return lax.dot_general(a.astype(BF16), b.astype(BF16), (((1,), (1,)), ((), ())),
                           preferred_element_type=F32)


def _bdot_tn(a, b):
    return lax.dot_general(a.astype(BF16), b.astype(BF16), (((0,), (0,)), ((), ())),
                           preferred_element_type=F32)


def _wkv_kernel(r_ref, nkk_ref, b_ref, k_ref, v_ref, lw_ref, g_ref, s0_ref,
                rk_ref, lng_ref, lnb_ref, y_ref, sout_ref, s_scr, *, seqs, chunk, cps, n_heads):
    c = pl.program_id(1)
    n_pairs = n_heads // 2
    c2 = 2 * chunk
    lanes = 2 * HEAD

    @pl.when(c == 0)
    def _():
        for q in range(seqs):
            for p in range(n_pairs):
                s_scr[q, p] = jnp.concatenate([s0_ref[q, 2 * p], s0_ref[q, 2 * p + 1]], axis=1)

    t_row = lax.broadcasted_iota(jnp.int32, (chunk, 1), 0)
    n_doublings = chunk.bit_length() - 1

    ri = lax.broadcasted_iota(jnp.int32, (c2, c2), 0)
    ci = lax.broadcasted_iota(jnp.int32, (c2, c2), 1)
    same = (ri // chunk) == (ci // chunk)
    strict = same & (ri % chunk > ci % chunk)
    incl = same & (ri % chunk >= ci % chunk)
    row_head = lax.broadcasted_iota(jnp.int32, (c2, lanes), 0) // chunk
    lane_head = lax.broadcasted_iota(jnp.int32, (c2, lanes), 1) // HEAD
    own = row_head == lane_head
    lo = lane_head == 0
    g_own = (lax.broadcasted_iota(jnp.int32, (lanes, lanes), 0) // HEAD
             == lax.broadcasted_iota(jnp.int32, (lanes, lanes), 1) // HEAD)
    lo_c = lax.broadcasted_iota(jnp.int32, (chunk, lanes), 1) < HEAD

    def stack(x):
        return jnp.concatenate([x, x], axis=0)

    def bd(x):
        return jnp.where(own, stack(x), 0.0)

    def own_half(x, low):
        swapped = pltpu.roll(x, HEAD, axis=1)
        rows = jnp.concatenate([x, swapped] if low else [swapped, x], axis=0)
        return jnp.where(lo if low else ~lo, rows, 0.0)

    ar_bd, bk_st, a_lo, r_lo, v_hi, bkh_bd, gam, unit = [], [], [], [], [], [], [], []
    for cc, q in [(cc, q) for cc in range(cps) for q in range(seqs)]:
        rows = slice(cc * chunk, (cc + 1) * chunk)
        lw = lw_ref[q, rows, :]
        cum = lw
        for step in range(n_doublings):
            shift = 1 << step
            cum = cum + jnp.where(t_row >= shift, pltpu.roll(cum, shift, axis=0), 0.0)
        last = cum[chunk - 1:chunk, :]
        e_in = jnp.exp(cum)
        e_out = jnp.exp(-cum)
        e_tail = jnp.exp(last - cum)
        a_t = nkk_ref[q, rows, :] * jnp.exp(cum - lw)
        r_t = r_ref[q, rows, :] * e_in
        b_t = b_ref[q, rows, :] * e_out
        k_t = k_ref[q, rows, :] * e_out
        b_h = b_ref[q, rows, :] * e_tail
        k_h = k_ref[q, rows, :] * e_tail
        gamma = jnp.exp(last)
        v_all = v_ref[q, rows, :]
        for p in range(n_pairs):
            ps = slice(p * lanes, (p + 1) * lanes)
            ar_bd.append(jnp.concatenate([bd(a_t[:, ps]), bd(r_t[:, ps])], axis=0))
            bk_st.append(jnp.concatenate([stack(b_t[:, ps]), stack(k_t[:, ps])], axis=0))
            a_lo.append(own_half(a_t[:, ps], True))
            r_lo.append(own_half(r_t[:, ps], True))
            v_hi.append(own_half(v_all[:, ps], False))
            bkh_bd.append(jnp.concatenate([bd(b_h[:, ps]), bd(k_h[:, ps])], axis=0))
            gam.append(gamma[:, ps])
            unit.append((q, p, ps, rows))

    units = range(cps * seqs * n_pairs)
    m = [_bdot_nt(ar_bd[u], bk_st[u]) for u in units]
    l_ab = [jnp.where(strict, m[u][:c2, :c2], 0.0) for u in units]
    l_ak = [jnp.where(strict, m[u][:c2, c2:], 0.0) for u in units]
    l_rbk = [jnp.concatenate([jnp.where(incl, m[u][c2:, :c2], 0.0),
                              jnp.where(incl, m[u][c2:, c2:], 0.0)], axis=1) for u in units]
    y = [a_lo[u] + _bdot(l_ak[u], v_hi[u]) for u in units]
    pw = l_ab
    for step in range(n_doublings):
        skip = (1 << step) if (1 << step) % 8 == 0 else 0

        def live_rows(x):
            return x if skip == 0 else jnp.concatenate([x[skip:chunk], x[chunk + skip:]], axis=0)

        def all_rows(x):
            if skip == 0:
                return x
            zero = jnp.zeros((skip, x.shape[1]), x.dtype)
            return jnp.concatenate([zero, x[:chunk - skip], zero, x[chunk - skip:]], axis=0)

        if step < n_doublings - 1:
            py = [all_rows(_bdot(live_rows(pw[u]), jnp.concatenate([y[u], pw[u]], axis=1)))
                  for u in units]
            pw = [py[u][:, lanes:] for u in units]
            y = [y[u] + py[u][:, :lanes] for u in units]
        else:
            y = [y[u] + all_rows(_bdot(live_rows(pw[u]), y[u])) for u in units]
    yv = [jnp.concatenate([y[u], v_hi[u]], axis=0) for u in units]
    gh = [_bdot_tn(yv[u], bkh_bd[u]) for u in units]
    lyv = [_bdot(l_rbk[u], yv[u]) for u in units]
    q_lo = [jnp.where(lo, r_lo[u] + lyv[u], 0.0) for u in units]
    q_bd = [jnp.where(own, q_lo[u] + pltpu.roll(q_lo[u], HEAD, axis=1), 0.0) for u in units]
    g_bd = [jnp.where(g_own, stack(gh[u][:HEAD]), 0.0) for u in units]
    zero_s = jnp.zeros((HEAD, lanes), F32)
    per_chunk = seqs * n_pairs
    state = [s_scr[unit[u][0], unit[u][1]] for u in range(per_chunk)]
    o = []
    for cc in range(cps):
        us = range(cc * per_chunk, (cc + 1) * per_chunk)
        o += [_bdot_nt(q_bd[u], jnp.concatenate([zero_s, state[u % per_chunk]], axis=0))
              for u in us]
        sg = [_bdot(state[u % per_chunk], g_bd[u]) for u in us]
        state = [state[u % per_chunk] * gam[u] + sg[u % per_chunk] + gh[u][HEAD:] for u in us]
    for u in range(per_chunk):
        s_scr[unit[u][0], unit[u][1]] = state[u]

    head_sum = g_own.astype(BF16)
    o = [o[u] + lyv[u] for u in units]
    o = [jnp.where(lo_c, pltpu.roll(o[u][:chunk], HEAD, axis=1), o[u][chunk:]) for u in units]
    o_hi = [o[u].astype(BF16) for u in units]
    mu = [(jnp.dot(o_hi[u], head_sum, preferred_element_type=F32)
           + _bdot(o[u] - o_hi[u].astype(F32), head_sum)) * (1.0 / HEAD) for u in units]
    d = [o[u] - mu[u] for u in units]
    var = [_bdot(d[u] * d[u], head_sum) * (1.0 / HEAD) for u in units]
    bonus = [_bdot(r_ref[unit[u][0], unit[u][3], unit[u][2]] * k_ref[unit[u][0], unit[u][3], unit[u][2]]
                   * rk_ref[:, unit[u][2]], head_sum) for u in units]
    for cc, q in [(cc, q) for cc in range(cps) for q in range(seqs)]:
        y_pairs = []
        for p in range(n_pairs):
            u = (cc * seqs + q) * n_pairs + p
            _, _, ps, rows = unit[u]
            gn = d[u] * lax.rsqrt(var[u] + GN_EPS) * lng_ref[:, ps] + lnb_ref[:, ps]
            y_pairs.append((gn + bonus[u] * v_ref[q, rows, ps]) * g_ref[q, rows, ps])
        y_ref[q, cc * chunk:(cc + 1) * chunk, :] = jnp.concatenate(y_pairs, axis=1).astype(BF16)

    @pl.when(c == pl.num_programs(1) - 1)
    def _():
        for q in range(seqs):
            for p in range(n_pairs):
                sp = s_scr[q, p]
                sout_ref[q, 2 * p] = sp[:, :HEAD]
                sout_ref[q, 2 * p + 1] = sp[:, HEAD:]


def _wkv(ops, s0, rk, lng, lnb, *, seq_len, chunk, seqs, cps):
    n, d_rwkv = ops[0].shape
    n_seq, n_heads = s0.shape[0], s0.shape[1]
    assert n_seq % seqs == 0 and seq_len % (cps * chunk) == 0
    ops = [o.reshape(n_seq, seq_len, d_rwkv) for o in ops]
    tok = pl.BlockSpec((seqs, cps * chunk, d_rwkv), lambda q, c: (q, c, 0))
    state = pl.BlockSpec((seqs, n_heads, HEAD, HEAD), lambda q, c: (q, 0, 0, 0))
    y, s_new = pl.pallas_call(
        functools.partial(_wkv_kernel, seqs=seqs, chunk=chunk, cps=cps, n_heads=n_heads),
        grid=(n_seq // seqs, seq_len // (cps * chunk)),
        in_specs=[tok] * 7 + [state] + [pl.BlockSpec((1, d_rwkv), lambda q, c: (0, 0))] * 3,
        out_specs=[tok, state],
        out_shape=[jax.ShapeDtypeStruct((n_seq, seq_len, d_rwkv), BF16),
                   jax.ShapeDtypeStruct(s0.shape, F32)],
        scratch_shapes=[pltpu.VMEM((seqs, n_heads // 2, HEAD, 2 * HEAD), F32)],
        compiler_params=pltpu.CompilerParams(
            dimension_semantics=("arbitrary", "arbitrary"),
            vmem_limit_bytes=VMEM_LIMIT_BYTES),
        name="wkv",
    )(*ops, s0, rk, lng, lnb)
    return y.reshape(n, d_rwkv), s_new


def _wkv_steps_kernel(r_ref, nkk_ref, b_ref, k_ref, v_ref, lw_ref, g_ref, s0_ref,
                      rk_ref, lng_ref, lnb_ref, y_ref, sout_ref, vt_scr, o_scr, *, n_t):
    n_seq = s0_ref.shape[-1]

    def feature_major(ref, t):
        return ref[t * n_seq:(t + 1) * n_seq, :].T

    for t in range(n_t):
        src = s0_ref if t == 0 else sout_ref
        nkk = feature_major(nkk_ref, t)
        beta = feature_major(b_ref, t)
        key = feature_major(k_ref, t)
        rec = feature_major(r_ref, t)
        decay = jnp.exp(feature_major(lw_ref, t))
        vt_scr[...] = feature_major(v_ref, t)
        for h in range(2):
            hs = slice(h * HEAD, (h + 1) * HEAD)
            nkk_h, beta_h, key_h, rec_h, decay_h = nkk[hs], beta[hs], key[hs], rec[hs], decay[hs]

            def one_row(i, carry):
                s = src[h, i]
                sa = jnp.sum(s * nkk_h, axis=0, keepdims=True)
                v_i = vt_scr[pl.ds(h * HEAD + i, 1), :]
                s = s * decay_h + sa * beta_h + v_i * key_h
                sout_ref[h, i] = s
                o_scr[pl.ds(h * HEAD + i, 1), :] = jnp.sum(s * rec_h, axis=0, keepdims=True)
                return carry

            lax.fori_loop(0, HEAD, one_row, 0, unroll=8)

        v_t = vt_scr[...]
        bonus_prod = rec * key * rk_ref[...]
        ys = []
        for h in range(2):
            hs = slice(h * HEAD, (h + 1) * HEAD)
            o = o_scr[hs, :]
            mu = jnp.mean(o, axis=0, keepdims=True)
            d = o - mu
            var = jnp.mean(d * d, axis=0, keepdims=True)
            gn = d * lax.rsqrt(var + GN_EPS) * lng_ref[hs, :] + lnb_ref[hs, :]
            bonus = jnp.sum(bonus_prod[hs], axis=0, keepdims=True)
            ys.append(gn + bonus * v_t[hs])
        y_t = jnp.concatenate(ys, axis=0) * feature_major(g_ref, t)
        y_ref[t * n_seq:(t + 1) * n_seq, :] = y_t.T.astype(y_ref.dtype)


def _wkv_steps(ops, s0, rk, lng, lnb, *, n_t):
    n, d_rwkv = ops[0].shape
    n_heads, n_seq = s0.shape[0], s0.shape[-1]
    lanes = 2 * HEAD
    assert n == n_t * n_seq and n_seq == lanes

    def bcast(a):
        return jnp.broadcast_to(a.reshape(d_rwkv, 1), (d_rwkv, n_seq))

    tok = pl.BlockSpec((n, lanes), lambda p: (0, p))
    state = pl.BlockSpec((2, HEAD, HEAD, n_seq), lambda p: (p, 0, 0, 0))
    par = pl.BlockSpec((lanes, n_seq), lambda p: (p, 0))
    return pl.pallas_call(
        functools.partial(_wkv_steps_kernel, n_t=n_t),
        grid=(n_heads // 2,),
        in_specs=[tok] * 7 + [state] + [par] * 3,
        out_specs=[tok, state],
        out_shape=[jax.ShapeDtypeStruct((n, d_rwkv), BF16),
                   jax.ShapeDtypeStruct(s0.shape, F32)],
        scratch_shapes=[pltpu.VMEM((lanes, n_seq), F32), pltpu.VMEM((lanes, n_seq), F32)],
        compiler_params=pltpu.CompilerParams(
            dimension_semantics=("arbitrary",), vmem_limit_bytes=VMEM_LIMIT_BYTES),
        name="wkv_steps",
    )(*ops, s0, bcast(rk), bcast(lng), bcast(lnb))


def _out_ffn_kernel(xa_ref, yra_ref, yca_ref, xb_ref, yrb_ref, ycb_ref, wo_ref,
                    g2_ref, b2_ref, wg_ref, wu_ref, wd_ref, g3_ref, b3_ref, oa_ref, ob_ref, *, alpha):
    last = pl.num_programs(0) - 1
    d_rwkv = yra_ref.shape[1]
    for x_ref, yr_ref, yc_ref, o_ref, cond in (
            (xa_ref, yra_ref, yca_ref, oa_ref, pl.program_id(0) < last),
            (xb_ref, yrb_ref, ycb_ref, ob_ref, pl.program_id(0) == last)):
        @pl.when(cond)
        def _(x_ref=x_ref, yr_ref=yr_ref, yc_ref=yc_ref, o_ref=o_ref):
            mix = jnp.dot(yr_ref[...], wo_ref[:d_rwkv, :], preferred_element_type=F32)
            mix = mix + jnp.dot(yc_ref[...], wo_ref[d_rwkv:, :], preferred_element_type=F32)

            def mixed_rows(sl):
                return _layer_norm(alpha * x_ref[sl, :] + mix[sl], g2_ref[...], b2_ref[...])

            _ffn_post_ln(mixed_rows, wg_ref, wu_ref, wd_ref, g3_ref, b3_ref, o_ref, alpha,
                         FFN2_GROUP)


def _out_ffn(group_a, group_b, w_out, g2, b2, wg, wu, wd, g3, b3, alpha):
    d = group_a[0].shape[1]
    dff = wg.shape[1]
    d_rwkv, d_conv = group_a[1].shape[1], group_a[2].shape[1]
    grid, long_rows, short_rows = _two_group_grid(group_a[0].shape[0], group_b[0].shape[0])
    return pl.pallas_call(
        functools.partial(_out_ffn_kernel, alpha=alpha),
        grid=grid,
        in_specs=[long_rows(d), long_rows(d_rwkv), long_rows(d_conv),
                  short_rows(d), short_rows(d_rwkv), short_rows(d_conv),
                  _resident((d_rwkv + d_conv, d)),
                  _resident((1, d)), _resident((1, d)),
                  _resident((d, dff)), _resident((d, dff)), _resident((dff, d)),
                  _resident((1, d)), _resident((1, d))],
        out_specs=[long_rows(d), short_rows(d, single_buffer=False)],
        out_shape=[jax.ShapeDtypeStruct(group_a[0].shape, F32),
                   jax.ShapeDtypeStruct(group_b[0].shape, F32)],
        compiler_params=pltpu.CompilerParams(
            dimension_semantics=("arbitrary",), vmem_limit_bytes=VMEM_LIMIT_BYTES),
        name="out_ffn",
    )(*group_a, *group_b, w_out, g2, b2, wg, wu, wd, g3, b3)


def _trunk_layer(xp, xs, p, alpha, *, tp, ts, wkv0_p, wkv0_s, init_s):
    x1p, x1s = _ffn_ln(xp, xs, p["ffn1_wg"], p["ffn1_wu"], p["ffn1_wd"], p["ln1_g"], p["ln1_b"],
                       alpha)
    pm_p = _premix(x1p, p, seq_len=tp, init=None)
    pm_s = _premix(x1s, p, seq_len=ts, init=init_s)
    yr_p, wkv_p = _wkv(pm_p[:7], wkv0_p, p["r_k"], p["lnx_g"], p["lnx_b"],
                       seq_len=tp, chunk=CHUNK, seqs=wkv0_p.shape[0], cps=CHUNKS_PER_STEP)
    yr_s, wkv_s = _wkv_steps(pm_s[:7], wkv0_s, p["r_k"], p["lnx_g"], p["lnx_b"], n_t=ts)
    yp, ys = _out_ffn((x1p, yr_p, pm_p[7]), (x1s, yr_s, pm_s[7]), p["w_out"],
                      p["ln2_g"], p["ln2_b"], p["ffn2_wg"], p["ffn2_wu"], p["ffn2_wd"],
                      p["ln3_g"], p["ln3_b"], alpha)
    return (yp, wkv_p, pm_p[8], pm_p[9]), (ys, wkv_s, pm_s[8], pm_s[9])


def kernel(x_prompt, x_sample, state_wkv, state_shift, state_conv, ln1_g, ln1_b, ffn1_wg, ffn1_wu, ffn1_wd, w_in, mu_shift, w0, w_lora_up, a0, a_lora_up, g_lora_up, k_k, k_a, r_k, lnx_g, lnx_b, conv_w, w_out, ln2_g, ln2_b, ffn2_wg, ffn2_wu, ffn2_wd, ln3_g, ln3_b):
    depth = ln1_g.shape[0]
    bp, tp, d = x_prompt.shape
    bs, ts, _ = x_sample.shape
    n_heads = state_wkv.shape[2]
    alpha = (2.0 * depth) ** 0.25
    assert ts >= CONV_W - 1

    xp = x_prompt.reshape(bp * tp, d)
    xs = x_sample.transpose(1, 0, 2).reshape(ts * bs, d)

    outs = {k: [] for k in ("wkv_p", "shift_p", "conv_p", "wkv_s", "shift_s", "conv_s")}
    for l in range(depth):
        vec = lambda a: a[l].reshape(1, -1).astype(F32)
        p = {
            "ln1_g": vec(ln1_g), "ln1_b": vec(ln1_b),
            "ffn1_wg": ffn1_wg[l].astype(BF16), "ffn1_wu": ffn1_wu[l].astype(BF16),
            "ffn1_wd": ffn1_wd[l].astype(BF16),
            "w_in": w_in[l].astype(BF16), "mu_shift": vec(mu_shift), "w0": vec(w0),
            "w_lora_up": w_lora_up[l].astype(BF16), "a0": vec(a0),
            "a_lora_up": a_lora_up[l].astype(BF16), "g_lora_up": g_lora_up[l].astype(BF16),
            "k_k": vec(k_k), "k_a": vec(k_a), "r_k": vec(r_k),
            "lnx_g": vec(lnx_g), "lnx_b": vec(lnx_b),
            "conv_w": conv_w[l].astype(F32), "w_out": w_out[l].astype(BF16),
            "ln2_g": vec(ln2_g), "ln2_b": vec(ln2_b),
            "ffn2_wg": ffn2_wg[l].astype(BF16), "ffn2_wu": ffn2_wu[l].astype(BF16),
            "ffn2_wd": ffn2_wd[l].astype(BF16),
            "ln3_g": vec(ln3_g), "ln3_b": vec(ln3_b),
        }
        wkv0_p = jnp.zeros((bp, n_heads, HEAD, HEAD), F32)
        init = (state_shift[l].astype(F32), state_conv[l].astype(F32).transpose(1, 0, 2))
        wkv0_s = state_wkv[l].astype(F32).transpose(1, 2, 3, 0)
        (xp, wp, sp, cp), (xs, wsm, ps_s, u_s) = _trunk_layer(
            xp, xs, p, alpha, tp=tp, ts=ts, wkv0_p=wkv0_p, wkv0_s=wkv0_s, init_s=init)
        outs["wkv_p"].append(wp)
        outs["shift_p"].append(sp[:, 0, :])
        outs["conv_p"].append(cp)
        outs["wkv_s"].append(wsm.transpose(3, 0, 1, 2))
        outs["shift_s"].append(ps_s[(ts - 1) * bs:])
        outs["conv_s"].append(
            u_s[(ts - (CONV_W - 1)) * bs:].reshape(CONV_W - 1, bs, -1).transpose(1, 0, 2))

    y_prompt = xp.reshape(bp, tp, d)
    y_sample = xs.reshape(ts, bs, d).transpose(1, 0, 2)
    return (y_prompt, y_sample,
            jnp.stack(outs["wkv_p"]), jnp.stack(outs["shift_p"]), jnp.stack(outs["conv_p"]),
            jnp.stack(outs["wkv_s"]), jnp.stack(outs["shift_s"]), jnp.stack(outs["conv_s"]))
```

```python
import functools
import math

import jax
import jax.numpy as jnp
from jax import lax
from jax.experimental import pallas as pl
from jax.experimental.pallas import tpu as pltpu

F32 = jnp.float32
BF16 = jnp.bfloat16

HEAD = 64
LORA_W = 64
LORA_A = 64
LORA_G = 128
CONV_W = 3
LN_EPS = 1e-5
GN_EPS = 1e-5 * HEAD

VMEM_LIMIT_BYTES = 56 * 1024 * 1024
TOKEN_TILE = 512
FFN_TILE = 1024
FFN1_GROUP = 256
FFN2_GROUP = 512
FFN_COLS = 1024
CHUNK = 64
CHUNKS_PER_STEP = 1


def _resident(shape):
    nd = len(shape)
    return pl.BlockSpec(shape, lambda *_: (0,) * nd, pipeline_mode=pl.Buffered(1))


def _layer_norm(x, g, b):
    mu = jnp.mean(x, axis=-1, keepdims=True)
    d = x - mu
    var = jnp.mean(d * d, axis=-1, keepdims=True)
    return d * lax.rsqrt(var + LN_EPS) * g + b


def _ffn_post_ln(rows_of, wg_ref, wu_ref, wd_ref, g_ref, b_ref, o_ref, alpha, group):
    dff = wg_ref.shape[1]
    for s in range(o_ref.shape[0] // group):
        sl = slice(s * group, (s + 1) * group)
        xs = rows_of(sl)
        xb = xs.astype(BF16)
        down = None
        for c0 in range(0, dff, FFN_COLS):
            cs = slice(c0, min(c0 + FFN_COLS, dff))
            g = jnp.dot(xb, wg_ref[:, cs], preferred_element_type=F32)
            u = jnp.dot(xb, wu_ref[:, cs], preferred_element_type=F32)
            h = (jax.nn.silu(g) * u).astype(BF16)
            part = jnp.dot(h, wd_ref[cs, :], preferred_element_type=F32)
            down = part if down is None else down + part
        y = alpha * xs + 0.5 * down
        o_ref[sl, :] = _layer_norm(y, g_ref[...], b_ref[...])


def _two_group_grid(n_long, n_short):
    tm = FFN_TILE
    assert n_long % tm == 0 and n_short % max(FFN1_GROUP, FFN2_GROUP) == 0
    n_tiles = n_long // tm

    def long_rows(c):
        return pl.BlockSpec((tm, c), lambda i: (jnp.minimum(i, n_tiles - 1), 0))

    def short_rows(c, single_buffer=True):
        mode = dict(pipeline_mode=pl.Buffered(1)) if single_buffer else {}
        return pl.BlockSpec((n_short, c), lambda i: (0, 0), **mode)

    return (n_tiles + 1,), long_rows, short_rows


def _ffn_ln_kernel(xa_ref, xb_ref, wg_ref, wu_ref, wd_ref, g_ref, b_ref, oa_ref, ob_ref, *, alpha):
    last = pl.num_programs(0) - 1
    for x_ref, o_ref, cond in ((xa_ref, oa_ref, pl.program_id(0) < last),
                               (xb_ref, ob_ref, pl.program_id(0) == last)):
        @pl.when(cond)
        def _(x_ref=x_ref, o_ref=o_ref):
            _ffn_post_ln(lambda sl: x_ref[sl, :], wg_ref, wu_ref, wd_ref, g_ref, b_ref,
                         o_ref, alpha, FFN1_GROUP)


def _ffn_ln(xa, xb, wg, wu, wd, g, b, alpha):
    d = xa.shape[1]
    dff = wg.shape[1]
    grid, long_rows, short_rows = _two_group_grid(xa.shape[0], xb.shape[0])
    return pl.pallas_call(
        functools.partial(_ffn_ln_kernel, alpha=alpha),
        grid=grid,
        in_specs=[long_rows(d), short_rows(d), _resident((d, dff)), _resident((d, dff)),
                  _resident((dff, d)), _resident((1, d)), _resident((1, d))],
        out_specs=[long_rows(d), short_rows(d, single_buffer=False)],
        out_shape=[jax.ShapeDtypeStruct(xa.shape, F32), jax.ShapeDtypeStruct(xb.shape, F32)],
        compiler_params=pltpu.CompilerParams(
            dimension_semantics=("arbitrary",), vmem_limit_bytes=VMEM_LIMIT_BYTES),
        name="ffn_ln",
    )(xa, xb, wg, wu, wd, g, b)


HDR = 8


def _shift_rows(x, head, k):
    rolled = pltpu.roll(x, k, axis=0)
    row8 = lax.broadcasted_iota(jnp.int32, (HDR, 1), 0)
    top = jnp.where(row8 < k, pltpu.roll(head, k, axis=0), rolled[:HDR])
    return jnp.concatenate([top, rolled[HDR:]], axis=0)


class _PremixRefs:
    def __init__(self, refs, tiles_per_seq, full_out):
        refs = list(refs)
        (self.x, self.win, self.mu, self.w0, self.wup, self.a0, self.aup, self.gup, self.kk,
         self.ka, self.cw) = refs[:11]
        refs = refs[11:]
        if tiles_per_seq == 0:
            self.sh0, self.cv0 = refs[:2]
            refs = refs[2:]
        (self.r, self.nkk, self.b, self.k, self.v, self.lw, self.g, self.yc) = refs[:8]
        refs = refs[8:]
        if full_out:
            self.ps_out, self.u_out = refs[:2]
        else:
            self.shift, self.tail = refs[:2]
        self.bufs = refs[2:]


def _premix_gates(rf, buf, first, *, tiles_per_seq, d_rwkv, d_conv):
    p_shift = 3 * d_rwkv + LORA_W + LORA_A + LORA_G
    tm = rf.x.shape[0]
    ps = buf[HDR:, :p_shift]
    c_b = buf[HDR:, p_shift:p_shift + d_conv]
    u = buf[HDR:, p_shift + d_conv:p_shift + 2 * d_conv] * buf[HDR:, p_shift + 2 * d_conv:]

    if tiles_per_seq:
        head = jnp.where(first, 0.0, buf[0:HDR, :])
        head_u = head[:, p_shift + d_conv:p_shift + 2 * d_conv] * head[:, p_shift + 2 * d_conv:]
        prev = _shift_rows(ps, head[:, :p_shift], 1)
        um1 = _shift_rows(u, head_u, 1)
        um2 = _shift_rows(u, head_u, 2)
    else:
        n_seq = rf.sh0.shape[0]
        prev = jnp.concatenate([rf.sh0[...], ps[:tm - n_seq]], axis=0)
        hist = jnp.concatenate([rf.cv0[t] for t in range(CONV_W - 1)] + [u], axis=0)
        um2 = hist[:tm]
        um1 = hist[n_seq:n_seq + tm]

    cw = rf.cw[...]
    z = cw[0:1, :] * um2
    z = z + cw[1:2, :] * um1
    z = z + cw[2:3, :] * u
    rf.yc[...] = (c_b * z).astype(BF16)
    if tiles_per_seq:
        rf.shift[0] = ps[tm - 1:tm, :]
        rf.tail[0] = u[tm - (CONV_W - 1):tm, :]
    else:
        rf.ps_out[...] = ps
        rf.u_out[...] = u

    mix = ps + rf.mu[...] * (prev - ps)
    r = mix[:, 0:d_rwkv]
    k = mix[:, d_rwkv:2 * d_rwkv]
    v = mix[:, 2 * d_rwkv:3 * d_rwkv]
    o = 3 * d_rwkv
    wd = mix[:, o:o + LORA_W]
    ad = mix[:, o + LORA_W:o + LORA_W + LORA_A]
    gd = mix[:, o + LORA_W + LORA_A:p_shift]

    rf.r[...] = r
    kk = k * rf.kk[...]
    dot_in = (jnp.tanh(wd).astype(BF16), ad.astype(BF16), jax.nn.sigmoid(gd).astype(BF16),
              kk * kk)
    return k, v, kk, dot_in


def _premix_dots(rf, dot_in):
    tanh_wd, ad, sig_gd, kk_sq = dot_in
    w_pre = jnp.dot(tanh_wd, rf.wup[...], preferred_element_type=F32)
    a_pre = jnp.dot(ad, rf.aup[...], preferred_element_type=F32)
    rf.g[...] = jnp.dot(sig_gd, rf.gup[...], preferred_element_type=F32)
    low = lax.broadcasted_iota(jnp.int32, (1, 2 * HEAD), 1) < HEAD
    parts = []
    for c0 in range(0, kk_sq.shape[1], 2 * HEAD):
        x = kk_sq[:, c0:c0 + 2 * HEAD]
        s_lo = jnp.sum(jnp.where(low, x, 0.0), axis=-1, keepdims=True)
        s_hi = jnp.sum(jnp.where(low, 0.0, x), axis=-1, keepdims=True)
        parts.append(jnp.where(low, s_lo, s_hi))
    ss = jnp.concatenate(parts, axis=1)
    return w_pre, a_pre, ss


def _premix_finish(rf, gates, dots):
    k, v, kk, _ = gates
    w_pre, a_pre, ss = dots
    log_decay = -math.exp(-0.5) * jax.nn.sigmoid(rf.w0[...] + w_pre)
    a = jax.nn.sigmoid(rf.a0[...] + a_pre)
    kk = kk * lax.rsqrt(jnp.maximum(ss, 1e-24))
    k = k * (1.0 + (a - 1.0) * rf.ka[...])
    nkk = -kk
    b = kk * a
    rf.nkk[...] = nkk
    rf.b[...] = b
    rf.k[...] = k
    rf.v[...] = v
    rf.lw[...] = log_decay


def _premix_kernel(*refs, tiles_per_seq, d_rwkv, d_conv):
    rf = _PremixRefs(refs, tiles_per_seq, full_out=tiles_per_seq == 0)
    buf, = rf.bufs
    tm = rf.x.shape[0]
    i = pl.program_id(0)
    cfg = dict(tiles_per_seq=tiles_per_seq, d_rwkv=d_rwkv, d_conv=d_conv)

    if tiles_per_seq:
        @pl.when(i == 0)
        def _():
            buf[0:HDR, :] = jnp.zeros((HDR, buf.shape[1]), F32)

    buf[HDR:, :] = jnp.dot(rf.x[...].astype(BF16), rf.win[...], preferred_element_type=F32)
    first = (i % tiles_per_seq == 0) if tiles_per_seq else None
    gates = _premix_gates(rf, buf, first, **cfg)
    _premix_finish(rf, gates, _premix_dots(rf, gates[3]))
    if tiles_per_seq:
        buf[0:HDR, :] = buf[tm:tm + HDR, :]


def _premix(x1, p, *, seq_len, init=None):
    n, d = x1.shape
    tm = TOKEN_TILE
    d_rwkv = p["w0"].shape[1]
    d_conv = p["conv_w"].shape[1]
    p_shift = 3 * d_rwkv + LORA_W + LORA_A + LORA_G
    p_total = p_shift + 3 * d_conv
    n_seq = n // seq_len
    if init is None:
        assert seq_len % tm == 0
        tiles_per_seq = seq_len // tm
    else:
        assert n == tm and n_seq % HDR == 0
        tiles_per_seq = 0
    full_out = init is not None

    def row(c):
        return pl.BlockSpec((tm, c), lambda i: (i, 0))

    in_specs = [row(d), _resident((d, p_total)), _resident((1, p_shift)),
                _resident((1, d_rwkv)), _resident((LORA_W, d_rwkv)),
                _resident((1, d_rwkv)), _resident((LORA_A, d_rwkv)),
                _resident((LORA_G, d_rwkv)), _resident((1, d_rwkv)), _resident((1, d_rwkv)),
                _resident((CONV_W, d_conv))]
    args = [x1, p["w_in"], p["mu_shift"], p["w0"], p["w_lora_up"], p["a0"], p["a_lora_up"],
            p["g_lora_up"], p["k_k"], p["k_a"], p["conv_w"]]
    if init is not None:
        in_specs += [_resident((n_seq, p_shift)), _resident((CONV_W - 1, n_seq, d_conv))]
        args += list(init)

    out_specs = [row(d_rwkv)] * 7 + [row(d_conv)]
    out_shape = [jax.ShapeDtypeStruct((n, d_rwkv), F32)] * 7 + [
        jax.ShapeDtypeStruct((n, d_conv), BF16)]
    if full_out:
        out_specs += [row(p_shift), row(d_conv)]
        out_shape += [jax.ShapeDtypeStruct((n, p_shift), F32),
                      jax.ShapeDtypeStruct((n, d_conv), F32)]
    else:
        out_specs += [
            pl.BlockSpec((1, 1, p_shift), lambda i: (i // tiles_per_seq, 0, 0)),
            pl.BlockSpec((1, CONV_W - 1, d_conv), lambda i: (i // tiles_per_seq, 0, 0))]
        out_shape += [jax.ShapeDtypeStruct((n_seq, 1, p_shift), F32),
                      jax.ShapeDtypeStruct((n_seq, CONV_W - 1, d_conv), F32)]

    return pl.pallas_call(
        functools.partial(_premix_kernel, tiles_per_seq=tiles_per_seq, d_rwkv=d_rwkv,
                          d_conv=d_conv),
        grid=(n // tm,),
        in_specs=in_specs,
        out_specs=out_specs,
        out_shape=out_shape,
        scratch_shapes=[pltpu.VMEM((HDR + tm, p_total), F32)],
        compiler_params=pltpu.CompilerParams(
            dimension_semantics=("arbitrary",), vmem_limit_bytes=VMEM_LIMIT_BYTES),
        name="premix",
    )(*args)


def _bdot(a, b):
    return jnp.dot(a.astype(BF16), b.astype(BF16), preferred_element_type=F32)


def _bdot_nt(a, b):
    return lax.dot_general(a.astype(BF16), b.astype(BF16), (((1,), (1,)), ((), ())),
                           preferred_element_type=F32)


def _bdot_tn(a, b):
    return lax.dot_general(a.astype(BF16), b.astype(BF16), (((0,), (0,)), ((), ())),
                           preferred_element_type=F32)


def _wkv_kernel(r_ref, nkk_ref, b_ref, k_ref, v_ref, lw_ref, g_ref, s0_ref,
                rk_ref, lng_ref, lnb_ref, y_ref, sout_ref, s_scr, *, seqs, chunk, cps, n_heads):
    c = pl.program_id(1)
    n_pairs = n_heads // 2
    c2 = 2 * chunk
    lanes = 2 * HEAD

    @pl.when(c == 0)
    def _():
        for q in range(seqs):
            for p in range(n_pairs):
                s_scr[q, p] = jnp.concatenate([s0_ref[q, 2 * p], s0_ref[q, 2 * p + 1]], axis=1)

    t_row = lax.broadcasted_iota(jnp.int32, (chunk, 1), 0)
    n_doublings = chunk.bit_length() - 1

    ri = lax.broadcasted_iota(jnp.int32, (c2, c2), 0)
    ci = lax.broadcasted_iota(jnp.int32, (c2, c2), 1)
    same = (ri // chunk) == (ci // chunk)
    strict = same & (ri % chunk > ci % chunk)
    incl = same & (ri % chunk >= ci % chunk)
    row_head = lax.broadcasted_iota(jnp.int32, (c2, lanes), 0) // chunk
    lane_head = lax.broadcasted_iota(jnp.int32, (c2, lanes), 1) // HEAD
    own = row_head == lane_head
    lo = lane_head == 0
    g_own = (lax.broadcasted_iota(jnp.int32, (lanes, lanes), 0) // HEAD
             == lax.broadcasted_iota(jnp.int32, (lanes, lanes), 1) // HEAD)
    lo_c = lax.broadcasted_iota(jnp.int32, (chunk, lanes), 1) < HEAD

    def stack(x):
        return jnp.concatenate([x, x], axis=0)

    def bd(x):
        return jnp.where(own, stack(x), 0.0)

    def own_half(x, low):
        swapped = pltpu.roll(x, HEAD, axis=1)
        rows = jnp.concatenate([x, swapped] if low else [swapped, x], axis=0)
        return jnp.where(lo if low else ~lo, rows, 0.0)

    ar_bd, bk_st, a_lo, r_lo, v_hi, bkh_bd, gam, unit = [], [], [], [], [], [], [], []
    for cc, q in [(cc, q) for cc in range(cps) for q in range(seqs)]:
        rows = slice(cc * chunk, (cc + 1) * chunk)
        lw = lw_ref[q, rows, :]
        cum = lw
        for step in range(n_doublings):
            shift = 1 << step
            cum = cum + jnp.where(t_row >= shift, pltpu.roll(cum, shift, axis=0), 0.0)
        last = cum[chunk - 1:chunk, :]
        e_in = jnp.exp(cum)
        e_out = jnp.exp(-cum)
        e_tail = jnp.exp(last - cum)
        a_t = nkk_ref[q, rows, :] * jnp.exp(cum - lw)
        r_t = r_ref[q, rows, :] * e_in
        b_t = b_ref[q, rows, :] * e_out
        k_t = k_ref[q, rows, :] * e_out
        b_h = b_ref[q, rows, :] * e_tail
        k_h = k_ref[q, rows, :] * e_tail
        gamma = jnp.exp(last)
        v_all = v_ref[q, rows, :]
        for p in range(n_pairs):
            ps = slice(p * lanes, (p + 1) * lanes)
            ar_bd.append(jnp.concatenate([bd(a_t[:, ps]), bd(r_t[:, ps])], axis=0))
            bk_st.append(jnp.concatenate([stack(b_t[:, ps]), stack(k_t[:, ps])], axis=0))
            a_lo.append(own_half(a_t[:, ps], True))
            r_lo.append(own_half(r_t[:, ps], True))
            v_hi.append(own_half(v_all[:, ps], False))
            bkh_bd.append(jnp.concatenate([bd(b_h[:, ps]), bd(k_h[:, ps])], axis=0))
            gam.append(gamma[:, ps])
            unit.append((q, p, ps, rows))

    units = range(cps * seqs * n_pairs)
    m = [_bdot_nt(ar_bd[u], bk_st[u]) for u in units]
    l_ab = [jnp.where(strict, m[u][:c2, :c2], 0.0) for u in units]
    l_ak = [jnp.where(strict, m[u][:c2, c2:], 0.0) for u in units]
    l_rbk = [jnp.concatenate([jnp.where(incl, m[u][c2:, :c2], 0.0),
                              jnp.where(incl, m[u][c2:, c2:], 0.0)], axis=1) for u in units]
    y = [a_lo[u] + _bdot(l_ak[u], v_hi[u]) for u in units]
    pw = l_ab
    for step in range(n_doublings):
        skip = (1 << step) if (1 << step) % 8 == 0 else 0

        def live_rows(x):
            return x if skip == 0 else jnp.concatenate([x[skip:chunk], x[chunk + skip:]], axis=0)

        def all_rows(x):
            if skip == 0:
                return x
            zero = jnp.zeros((skip, x.shape[1]), x.dtype)
            return jnp.concatenate([zero, x[:chunk - skip], zero, x[chunk - skip:]], axis=0)

        if step < n_doublings - 1:
            py = [all_rows(_bdot(live_rows(pw[u]), jnp.concatenate([y[u], pw[u]], axis=1)))
                  for u in units]
            pw = [py[u][:, lanes:] for u in units]
            y = [y[u] + py[u][:, :lanes] for u in units]
        else:
            y = [y[u] + all_rows(_bdot(live_rows(pw[u]), y[u])) for u in units]
    yv = [jnp.concatenate([y[u], v_hi[u]], axis=0) for u in units]
    gh = [_bdot_tn(yv[u], bkh_bd[u]) for u in units]
    lyv = [_bdot(l_rbk[u], yv[u]) for u in units]
    q_lo = [jnp.where(lo, r_lo[u] + lyv[u], 0.0) for u in units]
    q_bd = [jnp.where(own, q_lo[u] + pltpu.roll(q_lo[u], HEAD, axis=1), 0.0) for u in units]
    g_bd = [jnp.where(g_own, stack(gh[u][:HEAD]), 0.0) for u in units]
    zero_s = jnp.zeros((HEAD, lanes), F32)
    per_chunk = seqs * n_pairs
    state = [s_scr[unit[u][0], unit[u][1]] for u in range(per_chunk)]
    o = []
    for cc in range(cps):
        us = range(cc * per_chunk, (cc + 1) * per_chunk)
        o += [_bdot_nt(q_bd[u], jnp.concatenate([zero_s, state[u % per_chunk]], axis=0))
              for u in us]
        sg = [_bdot(state[u % per_chunk], g_bd[u]) for u in us]
        state = [state[u % per_chunk] * gam[u] + sg[u % per_chunk] + gh[u][HEAD:] for u in us]
    for u in range(per_chunk):
        s_scr[unit[u][0], unit[u][1]] = state[u]

    head_sum = g_own.astype(BF16)
    o = [o[u] + lyv[u] for u in units]
    o = [jnp.where(lo_c, pltpu.roll(o[u][:chunk], HEAD, axis=1), o[u][chunk:]) for u in units]
    o_hi = [o[u].astype(BF16) for u in units]
    mu = [(jnp.dot(o_hi[u], head_sum, preferred_element_type=F32)
           + _bdot(o[u] - o_hi[u].astype(F32), head_sum)) * (1.0 / HEAD) for u in units]
    d = [o[u] - mu[u] for u in units]
    var = [_bdot(d[u] * d[u], head_sum) * (1.0 / HEAD) for u in units]
    bonus = [_bdot(r_ref[unit[u][0], unit[u][3], unit[u][2]] * k_ref[unit[u][0], unit[u][3], unit[u][2]]
                   * rk_ref[:, unit[u][2]], head_sum) for u in units]
    for cc, q in [(cc, q) for cc in range(cps) for q in range(seqs)]:
        y_pairs = []
        for p in range(n_pairs):
            u = (cc * seqs + q) * n_pairs + p
            _, _, ps, rows = unit[u]
            gn = d[u] * lax.rsqrt(var[u] + GN_EPS) * lng_ref[:, ps] + lnb_ref[:, ps]
            y_pairs.append((gn + bonus[u] * v_ref[q, rows, ps]) * g_ref[q, rows, ps])
        y_ref[q, cc * chunk:(cc + 1) * chunk, :] = jnp.concatenate(y_pairs, axis=1).astype(BF16)

    @pl.when(c == pl.num_programs(1) - 1)
    def _():
        for q in range(seqs):
            for p in range(n_pairs):
                sp = s_scr[q, p]
                sout_ref[q, 2 * p] = sp[:, :HEAD]
                sout_ref[q, 2 * p + 1] = sp[:, HEAD:]


def _wkv(ops, s0, rk, lng, lnb, *, seq_len, chunk, seqs, cps):
    n, d_rwkv = ops[0].shape
    n_seq, n_heads = s0.shape[0], s0.shape[1]
    assert n_seq % seqs == 0 and seq_len % (cps * chunk) == 0
    ops = [o.reshape(n_seq, seq_len, d_rwkv) for o in ops]
    tok = pl.BlockSpec((seqs, cps * chunk, d_rwkv), lambda q, c: (q, c, 0))
    state = pl.BlockSpec((seqs, n_heads, HEAD, HEAD), lambda q, c: (q, 0, 0, 0))
    y, s_new = pl.pallas_call(
        functools.partial(_wkv_kernel, seqs=seqs, chunk=chunk, cps=cps, n_heads=n_heads),
        grid=(n_seq // seqs, seq_len // (cps * chunk)),
        in_specs=[tok] * 7 + [state] + [pl.BlockSpec((1, d_rwkv), lambda q, c: (0, 0))] * 3,
        out_specs=[tok, state],
        out_shape=[jax.ShapeDtypeStruct((n_seq, seq_len, d_rwkv), BF16),
                   jax.ShapeDtypeStruct(s0.shape, F32)],
        scratch_shapes=[pltpu.VMEM((seqs, n_heads // 2, HEAD, 2 * HEAD), F32)],
        compiler_params=pltpu.CompilerParams(
            dimension_semantics=("arbitrary", "arbitrary"),
            vmem_limit_bytes=VMEM_LIMIT_BYTES),
        name="wkv",
    )(*ops, s0, rk, lng, lnb)
    return y.reshape(n, d_rwkv), s_new


def _wkv_steps_kernel(r_ref, nkk_ref, b_ref, k_ref, v_ref, lw_ref, g_ref, s0_ref,
                      rk_ref, lng_ref, lnb_ref, y_ref, sout_ref, vt_scr, o_scr, *, n_t):
    n_seq = s0_ref.shape[-1]

    def feature_major(ref, t):
        return ref[t * n_seq:(t + 1) * n_seq, :].T

    for t in range(n_t):
        src = s0_ref if t == 0 else sout_ref
        nkk = feature_major(nkk_ref, t)
        beta = feature_major(b_ref, t)
        key = feature_major(k_ref, t)
        rec = feature_major(r_ref, t)
        decay = jnp.exp(feature_major(lw_ref, t))
        vt_scr[...] = feature_major(v_ref, t)
        for h in range(2):
            hs = slice(h * HEAD, (h + 1) * HEAD)
            nkk_h, beta_h, key_h, rec_h, decay_h = nkk[hs], beta[hs], key[hs], rec[hs], decay[hs]

            def one_row(i, carry):
                s = src[h, i]
                sa = jnp.sum(s * nkk_h, axis=0, keepdims=True)
                v_i = vt_scr[pl.ds(h * HEAD + i, 1), :]
                s = s * decay_h + sa * beta_h + v_i * key_h
                sout_ref[h, i] = s
                o_scr[pl.ds(h * HEAD + i, 1), :] = jnp.sum(s * rec_h, axis=0, keepdims=True)
                return carry

            lax.fori_loop(0, HEAD, one_row, 0, unroll=8)

        v_t = vt_scr[...]
        bonus_prod = rec * key * rk_ref[...]
        ys = []
        for h in range(2):
            hs = slice(h * HEAD, (h + 1) * HEAD)
            o = o_scr[hs, :]
            mu = jnp.mean(o, axis=0, keepdims=True)
            d = o - mu
            var = jnp.mean(d * d, axis=0, keepdims=True)
            gn = d * lax.rsqrt(var + GN_EPS) * lng_ref[hs, :] + lnb_ref[hs, :]
            bonus = jnp.sum(bonus_prod[hs], axis=0, keepdims=True)
            ys.append(gn + bonus * v_t[hs])
        y_t = jnp.concatenate(ys, axis=0) * feature_major(g_ref, t)
        y_ref[t * n_seq:(t + 1) * n_seq, :] = y_t.T.astype(y_ref.dtype)


def _wkv_steps(ops, s0, rk, lng, lnb, *, n_t):
    n, d_rwkv = ops[0].shape
    n_heads, n_seq = s0.shape[0], s0.shape[-1]
    lanes = 2 * HEAD
    assert n == n_t * n_seq and n_seq == lanes

    def bcast(a):
        return jnp.broadcast_to(a.reshape(d_rwkv, 1), (d_rwkv, n_seq))

    tok = pl.BlockSpec((n, lanes), lambda p: (0, p))
    state = pl.BlockSpec((2, HEAD, HEAD, n_seq), lambda p: (p, 0, 0, 0))
    par = pl.BlockSpec((lanes, n_seq), lambda p: (p, 0))
    return pl.pallas_call(
        functools.partial(_wkv_steps_kernel, n_t=n_t),
        grid=(n_heads // 2,),
        in_specs=[tok] * 7 + [state] + [par] * 3,
        out_specs=[tok, state],
        out_shape=[jax.ShapeDtypeStruct((n, d_rwkv), BF16),
                   jax.ShapeDtypeStruct(s0.shape, F32)],
        scratch_shapes=[pltpu.VMEM((lanes, n_seq), F32), pltpu.VMEM((lanes, n_seq), F32)],
        compiler_params=pltpu.CompilerParams(
            dimension_semantics=("arbitrary",), vmem_limit_bytes=VMEM_LIMIT_BYTES),
        name="wkv_steps",
    )(*ops, s0, bcast(rk), bcast(lng), bcast(lnb))


def _out_ffn_kernel(xa_ref, yra_ref, yca_ref, xb_ref, yrb_ref, ycb_ref, wo_ref,
                    g2_ref, b2_ref, wg_ref, wu_ref, wd_ref, g3_ref, b3_ref, oa_ref, ob_ref, *, alpha):
    last = pl.num_programs(0) - 1
    d_rwkv = yra_ref.shape[1]
    for x_ref, yr_ref, yc_ref, o_ref, cond in (
            (xa_ref, yra_ref, yca_ref, oa_ref, pl.program_id(0) < last),
            (xb_ref, yrb_ref, ycb_ref, ob_ref, pl.program_id(0) == last)):
        @pl.when(cond)
        def _(x_ref=x_ref, yr_ref=yr_ref, yc_ref=yc_ref, o_ref=o_ref):
            mix = jnp.dot(yr_ref[...], wo_ref[:d_rwkv, :], preferred_element_type=F32)
            mix = mix + jnp.dot(yc_ref[...], wo_ref[d_rwkv:, :], preferred_element_type=F32)

            def mixed_rows(sl):
                return _layer_norm(alpha * x_ref[sl, :] + mix[sl], g2_ref[...], b2_ref[...])

            _ffn_post_ln(mixed_rows, wg_ref, wu_ref, wd_ref, g3_ref, b3_ref, o_ref, alpha,
                         FFN2_GROUP)


def _out_ffn(group_a, group_b, w_out, g2, b2, wg, wu, wd, g3, b3, alpha):
    d = group_a[0].shape[1]
    dff = wg.shape[1]
    d_rwkv, d_conv = group_a[1].shape[1], group_a[2].shape[1]
    grid, long_rows, short_rows = _two_group_grid(group_a[0].shape[0], group_b[0].shape[0])
    return pl.pallas_call(
        functools.partial(_out_ffn_kernel, alpha=alpha),
        grid=grid,
        in_specs=[long_rows(d), long_rows(d_rwkv), long_rows(d_conv),
                  short_rows(d), short_rows(d_rwkv), short_rows(d_conv),
                  _resident((d_rwkv + d_conv, d)),
                  _resident((1, d)), _resident((1, d)),
                  _resident((d, dff)), _resident((d, dff)), _resident((dff, d)),
                  _resident((1, d)), _resident((1, d))],
        out_specs=[long_rows(d), short_rows(d, single_buffer=False)],
        out_shape=[jax.ShapeDtypeStruct(group_a[0].shape, F32),
                   jax.ShapeDtypeStruct(group_b[0].shape, F32)],
        compiler_params=pltpu.CompilerParams(
            dimension_semantics=("arbitrary",), vmem_limit_bytes=VMEM_LIMIT_BYTES),
        name="out_ffn",
    )(*group_a, *group_b, w_out, g2, b2, wg, wu, wd, g3, b3)


def _trunk_layer(xp, xs, p, alpha, *, tp, ts, wkv0_p, wkv0_s, init_s):
    x1p, x1s = _ffn_ln(xp, xs, p["ffn1_wg"], p["ffn1_wu"], p["ffn1_wd"], p["ln1_g"], p["ln1_b"],
                       alpha)
    pm_p = _premix(x1p, p, seq_len=tp, init=None)
    pm_s = _premix(x1s, p, seq_len=ts, init=init_s)
    yr_p, wkv_p = _wkv(pm_p[:7], wkv0_p, p["r_k"], p["lnx_g"], p["lnx_b"],
                       seq_len=tp, chunk=CHUNK, seqs=wkv0_p.shape[0], cps=CHUNKS_PER_STEP)
    yr_s, wkv_s = _wkv_steps(pm_s[:7], wkv0_s, p["r_k"], p["lnx_g"], p["lnx_b"], n_t=ts)
    yp, ys = _out_ffn((x1p, yr_p, pm_p[7]), (x1s, yr_s, pm_s[7]), p["w_out"],
                      p["ln2_g"], p["ln2_b"], p["ffn2_wg"], p["ffn2_wu"], p["ffn2_wd"],
                      p["ln3_g"], p["ln3_b"], alpha)
    return (yp, wkv_p, pm_p[8], pm_p[9]), (ys, wkv_s, pm_s[8], pm_s[9])


def kernel(x_prompt, x_sample, state_wkv, state_shift, state_conv, ln1_g, ln1_b, ffn1_wg, ffn1_wu, ffn1_wd, w_in, mu_shift, w0, w_lora_up, a0, a_lora_up, g_lora_up, k_k, k_a, r_k, lnx_g, lnx_b, conv_w, w_out, ln2_g, ln2_b, ffn2_wg, ffn2_wu, ffn2_wd, ln3_g, ln3_b):
    depth = ln1_g.shape[0]
    bp, tp, d = x_prompt.shape
    bs, ts, _ = x_sample.shape
    n_heads = state_wkv.shape[2]
    alpha = (2.0 * depth) ** 0.25
    assert ts >= CONV_W - 1

    xp = x_prompt.reshape(bp * tp, d)
    xs = x_sample.transpose(1, 0, 2).reshape(ts * bs, d)

    outs = {k: [] for k in ("wkv_p", "shift_p", "conv_p", "wkv_s", "shift_s", "conv_s")}
    for l in range(depth):
        vec = lambda a: a[l].reshape(1, -1).astype(F32)
        p = {
            "ln1_g": vec(ln1_g), "ln1_b": vec(ln1_b),
            "ffn1_wg": ffn1_wg[l].astype(BF16), "ffn1_wu": ffn1_wu[l].astype(BF16),
            "ffn1_wd": ffn1_wd[l].astype(BF16),
            "w_in": w_in[l].astype(BF16), "mu_shift": vec(mu_shift), "w0": vec(w0),
            "w_lora_up": w_lora_up[l].astype(BF16), "a0": vec(a0),
            "a_lora_up": a_lora_up[l].astype(BF16), "g_lora_up": g_lora_up[l].astype(BF16),
            "k_k": vec(k_k), "k_a": vec(k_a), "r_k": vec(r_k),
            "lnx_g": vec(lnx_g), "lnx_b": vec(lnx_b),
            "conv_w": conv_w[l].astype(F32), "w_out": w_out[l].astype(BF16),
            "ln2_g": vec(ln2_g), "ln2_b": vec(ln2_b),
            "ffn2_wg": ffn2_wg[l].astype(BF16), "ffn2_wu": ffn2_wu[l].astype(BF16),
            "ffn2_wd": ffn2_wd[l].astype(BF16),
            "ln3_g": vec(ln3_g), "ln3_b": vec(ln3_b),
        }
        wkv0_p = jnp.zeros((bp, n_heads, HEAD, HEAD), F32)
        init = (state_shift[l].astype(F32), state_conv[l].astype(F32).transpose(1, 0, 2))
        wkv0_s = state_wkv[l].astype(F32).transpose(1, 2, 3, 0)
        (xp, wp, sp, cp), (xs, wsm, ps_s, u_s) = _trunk_layer(
            xp, xs, p, alpha, tp=tp, ts=ts, wkv0_p=wkv0_p, wkv0_s=wkv0_s, init_s=init)
        outs["wkv_p"].append(wp)
        outs["shift_p"].append(sp[:, 0, :])
        outs["conv_p"].append(cp)
        outs["wkv_s"].append(wsm.transpose(3, 0, 1, 2))
        outs["shift_s"].append(ps_s[(ts - 1) * bs:])
        outs["conv_s"].append(
            u_s[(ts - (CONV_W - 1)) * bs:].reshape(CONV_W - 1, bs, -1).transpose(1, 0, 2))

    y_prompt = xp.reshape(bp, tp, d)
    y_sample = xs.reshape(ts, bs, d).transpose(1, 0, 2)
    return (y_prompt, y_sample,
            jnp.stack(outs["wkv_p"]), jnp.stack(outs["shift_p"]), jnp.stack(outs["conv_p"]),
            jnp.stack(outs["wkv_s"]), jnp.stack(outs["shift_s"]), jnp.stack(outs["conv_s"]))
```

```python
import functools
import math

import jax
import jax.numpy as jnp
from jax import lax
from jax.experimental import pallas as pl
from jax.experimental.pallas import tpu as pltpu

F32 = jnp.float32
BF16 = jnp.bfloat16

HEAD = 64
LORA_W = 64
LORA_A = 64
LORA_G = 128
CONV_W = 3
LN_EPS = 1e-5
GN_EPS = 1e-5 * HEAD

VMEM_LIMIT_BYTES = 56 * 1024 * 1024
TOKEN_TILE = 512
FFN_TILE = 1024
FFN1_GROUP = 256
FFN2_GROUP = 512
CHUNK = 64
CHUNKS_PER_STEP = 1


def _resident(shape):
    nd = len(shape)
    return pl.BlockSpec(shape, lambda *_: (0,) * nd, pipeline_mode=pl.Buffered(1))


def _layer_norm(x, g, b):
    mu = jnp.mean(x, axis=-1, keepdims=True)
    d = x - mu
    var = jnp.mean(d * d, axis=-1, keepdims=True)
    return d * lax.rsqrt(var + LN_EPS) * g + b


def _ffn_post_ln(rows_of, wg_ref, wu_ref, wd_ref, g_ref, b_ref, o_ref, alpha, group):
    for s in range(o_ref.shape[0] // group):
        sl = slice(s * group, (s + 1) * group)
        xs = rows_of(sl)
        xb = xs.astype(BF16)
        g = jnp.dot(xb, wg_ref[...], preferred_element_type=F32)
        u = jnp.dot(xb, wu_ref[...], preferred_element_type=F32)
        h = (jax.nn.silu(g) * u).astype(BF16)
        y = alpha * xs + 0.5 * jnp.dot(h, wd_ref[...], preferred_element_type=F32)
        o_ref[sl, :] = _layer_norm(y, g_ref[...], b_ref[...])


def _two_group_grid(n_long, n_short):
    tm = FFN_TILE
    assert n_long % tm == 0 and n_short % max(FFN1_GROUP, FFN2_GROUP) == 0
    n_tiles = n_long // tm

    def long_rows(c):
        return pl.BlockSpec((tm, c), lambda i: (jnp.minimum(i, n_tiles - 1), 0))

    def short_rows(c, single_buffer=True):
        mode = dict(pipeline_mode=pl.Buffered(1)) if single_buffer else {}
        return pl.BlockSpec((n_short, c), lambda i: (0, 0), **mode)

    return (n_tiles + 1,), long_rows, short_rows


WEIGHT_CAST_PARTS = 8


def _load_bf16(src_hbm, dst_ref, stage, sem):
    rows = stage.shape[1]

    def copy(i):
        return pltpu.make_async_copy(src_hbm.at[pl.ds(i * rows, rows), :], stage.at[i % 2],
                                     sem.at[i % 2])

    copy(0).start()
    for i in range(WEIGHT_CAST_PARTS):
        if i + 1 < WEIGHT_CAST_PARTS:
            copy(i + 1).start()
        copy(i).wait()
        dst_ref[i * rows:(i + 1) * rows, :] = stage[i % 2].astype(BF16)


def _ffn_ln_kernel(xa_ref, xb_ref, wg_hbm, wu_hbm, wd_hbm, g_ref, b_ref, oa_ref, ob_ref,
                   wg_ref, wu_ref, wd_ref, stage_up, stage_down, sem, *, alpha):
    @pl.when(pl.program_id(0) == 0)
    def _():
        _load_bf16(wg_hbm, wg_ref, stage_up, sem)
        _load_bf16(wu_hbm, wu_ref, stage_up, sem)
        _load_bf16(wd_hbm, wd_ref, stage_down, sem)

    last = pl.num_programs(0) - 1
    for x_ref, o_ref, cond in ((xa_ref, oa_ref, pl.program_id(0) < last),
                               (xb_ref, ob_ref, pl.program_id(0) == last)):
        @pl.when(cond)
        def _(x_ref=x_ref, o_ref=o_ref):
            _ffn_post_ln(lambda sl: x_ref[sl, :], wg_ref, wu_ref, wd_ref, g_ref, b_ref,
                         o_ref, alpha, FFN1_GROUP)


def _ffn_ln(xa, xb, wg, wu, wd, g, b, alpha):
    d = xa.shape[1]
    dff = wg.shape[1]
    grid, long_rows, short_rows = _two_group_grid(xa.shape[0], xb.shape[0])
    return pl.pallas_call(
        functools.partial(_ffn_ln_kernel, alpha=alpha),
        grid=grid,
        in_specs=[long_rows(d), short_rows(d)] + [pl.BlockSpec(memory_space=pl.ANY)] * 3
                 + [_resident((1, d)), _resident((1, d))],
        out_specs=[long_rows(d), short_rows(d, single_buffer=False)],
        out_shape=[jax.ShapeDtypeStruct(xa.shape, F32), jax.ShapeDtypeStruct(xb.shape, F32)],
        scratch_shapes=[pltpu.VMEM((d, dff), BF16), pltpu.VMEM((d, dff), BF16),
                        pltpu.VMEM((dff, d), BF16),
                        pltpu.VMEM((2, d // WEIGHT_CAST_PARTS, dff), F32),
                        pltpu.VMEM((2, dff // WEIGHT_CAST_PARTS, d), F32),
                        pltpu.SemaphoreType.DMA((2,))],
        compiler_params=pltpu.CompilerParams(
            dimension_semantics=("arbitrary",), vmem_limit_bytes=VMEM_LIMIT_BYTES),
        name="ffn_ln",
    )(xa, xb, wg, wu, wd, g, b)


HDR = 8


def _shift_rows(x, head, k):
    rolled = pltpu.roll(x, k, axis=0)
    row8 = lax.broadcasted_iota(jnp.int32, (HDR, 1), 0)
    top = jnp.where(row8 < k, pltpu.roll(head, k, axis=0), rolled[:HDR])
    return jnp.concatenate([top, rolled[HDR:]], axis=0)


class _PremixRefs:
    def __init__(self, refs, tiles_per_seq, full_out):
        refs = list(refs)
        (self.x, self.win, self.mu, self.w0, self.wup, self.a0, self.aup, self.gup, self.kk,
         self.ka, self.cw) = refs[:11]
        refs = refs[11:]
        if tiles_per_seq == 0:
            self.sh0, self.cv0 = refs[:2]
            refs = refs[2:]
        (self.r, self.nkk, self.b, self.k, self.v, self.lw, self.g, self.yc) = refs[:8]
        refs = refs[8:]
        if full_out:
            self.ps_out, self.u_out = refs[:2]
        else:
            self.shift, self.tail = refs[:2]
        self.bufs = refs[2:]


def _premix_gates(rf, buf, first, *, tiles_per_seq, d_rwkv, d_conv):
    p_shift = 3 * d_rwkv + LORA_W + LORA_A + LORA_G
    tm = rf.x.shape[0]
    ps = buf[HDR:, :p_shift]
    c_b = buf[HDR:, p_shift:p_shift + d_conv]
    u = buf[HDR:, p_shift + d_conv:p_shift + 2 * d_conv] * buf[HDR:, p_shift + 2 * d_conv:]

    if tiles_per_seq:
        head = jnp.where(first, 0.0, buf[0:HDR, :])
        head_u = head[:, p_shift + d_conv:p_shift + 2 * d_conv] * head[:, p_shift + 2 * d_conv:]
        prev = _shift_rows(ps, head[:, :p_shift], 1)
        um1 = _shift_rows(u, head_u, 1)
        um2 = _shift_rows(u, head_u, 2)
    else:
        n_seq = rf.sh0.shape[0]
        prev = jnp.concatenate([rf.sh0[...], ps[:tm - n_seq]], axis=0)
        hist = jnp.concatenate([rf.cv0[t] for t in range(CONV_W - 1)] + [u], axis=0)
        um2 = hist[:tm]
        um1 = hist[n_seq:n_seq + tm]

    cw = rf.cw[...]
    z = cw[0:1, :] * um2
    z = z + cw[1:2, :] * um1
    z = z + cw[2:3, :] * u
    rf.yc[...] = (c_b * z).astype(BF16)
    if tiles_per_seq:
        rf.shift[0] = ps[tm - 1:tm, :]
        rf.tail[0] = u[tm - (CONV_W - 1):tm, :]
    else:
        rf.ps_out[...] = ps
        rf.u_out[...] = u

    mix = ps + rf.mu[...] * (prev - ps)
    r = mix[:, 0:d_rwkv]
    k = mix[:, d_rwkv:2 * d_rwkv]
    v = mix[:, 2 * d_rwkv:3 * d_rwkv]
    o = 3 * d_rwkv
    wd = mix[:, o:o + LORA_W]
    ad = mix[:, o + LORA_W:o + LORA_W + LORA_A]
    gd = mix[:, o + LORA_W + LORA_A:p_shift]

    rf.r[...] = r
    kk = k * rf.kk[...]
    dot_in = (jnp.tanh(wd).astype(BF16), ad.astype(BF16), jax.nn.sigmoid(gd).astype(BF16),
              kk * kk)
    return k, v, kk, dot_in


def _premix_dots(rf, dot_in):
    tanh_wd, ad, sig_gd, kk_sq = dot_in
    w_pre = jnp.dot(tanh_wd, rf.wup[...], preferred_element_type=F32)
    a_pre = jnp.dot(ad, rf.aup[...], preferred_element_type=F32)
    rf.g[...] = jnp.dot(sig_gd, rf.gup[...], preferred_element_type=F32)
    low = lax.broadcasted_iota(jnp.int32, (1, 2 * HEAD), 1) < HEAD
    parts = []
    for c0 in range(0, kk_sq.shape[1], 2 * HEAD):
        x = kk_sq[:, c0:c0 + 2 * HEAD]
        s_lo = jnp.sum(jnp.where(low, x, 0.0), axis=-1, keepdims=True)
        s_hi = jnp.sum(jnp.where(low, 0.0, x), axis=-1, keepdims=True)
        parts.append(jnp.where(low, s_lo, s_hi))
    ss = jnp.concatenate(parts, axis=1)
    return w_pre, a_pre, ss


def _premix_finish(rf, gates, dots):
    k, v, kk, _ = gates
    w_pre, a_pre, ss = dots
    log_decay = -math.exp(-0.5) * jax.nn.sigmoid(rf.w0[...] + w_pre)
    a = jax.nn.sigmoid(rf.a0[...] + a_pre)
    kk = kk * lax.rsqrt(jnp.maximum(ss, 1e-24))
    k = k * (1.0 + (a - 1.0) * rf.ka[...])
    nkk = -kk
    b = kk * a
    rf.nkk[...] = nkk
    rf.b[...] = b
    rf.k[...] = k
    rf.v[...] = v
    rf.lw[...] = log_decay


def _premix_kernel(*refs, tiles_per_seq, d_rwkv, d_conv):
    rf = _PremixRefs(refs, tiles_per_seq, full_out=tiles_per_seq == 0)
    buf, = rf.bufs
    tm = rf.x.shape[0]
    i = pl.program_id(0)
    cfg = dict(tiles_per_seq=tiles_per_seq, d_rwkv=d_rwkv, d_conv=d_conv)

    if tiles_per_seq:
        @pl.when(i == 0)
        def _():
            buf[0:HDR, :] = jnp.zeros((HDR, buf.shape[1]), F32)

    buf[HDR:, :] = jnp.dot(rf.x[...].astype(BF16), rf.win[...], preferred_element_type=F32)
    first = (i % tiles_per_seq == 0) if tiles_per_seq else None
    gates = _premix_gates(rf, buf, first, **cfg)
    _premix_finish(rf, gates, _premix_dots(rf, gates[3]))
    if tiles_per_seq:
        buf[0:HDR, :] = buf[tm:tm + HDR, :]


def _premix(x1, p, *, seq_len, init=None):
    n, d = x1.shape
    tm = TOKEN_TILE
    d_rwkv = p["w0"].shape[1]
    d_conv = p["conv_w"].shape[1]
    p_shift = 3 * d_rwkv + LORA_W + LORA_A + LORA_G
    p_total = p_shift + 3 * d_conv
    n_seq = n // seq_len
    if init is None:
        assert seq_len % tm == 0
        tiles_per_seq = seq_len // tm
    else:
        assert n == tm and n_seq % HDR == 0
        tiles_per_seq = 0
    full_out = init is not None

    def row(c):
        return pl.BlockSpec((tm, c), lambda i: (i, 0))

    in_specs = [row(d), _resident((d, p_total)), _resident((1, p_shift)),
                _resident((1, d_rwkv)), _resident((LORA_W, d_rwkv)),
                _resident((1, d_rwkv)), _resident((LORA_A, d_rwkv)),
                _resident((LORA_G, d_rwkv)), _resident((1, d_rwkv)), _resident((1, d_rwkv)),
                _resident((CONV_W, d_conv))]
    args = [x1, p["w_in"], p["mu_shift"], p["w0"], p["w_lora_up"], p["a0"], p["a_lora_up"],
            p["g_lora_up"], p["k_k"], p["k_a"], p["conv_w"]]
    if init is not None:
        in_specs += [_resident((n_seq, p_shift)), _resident((CONV_W - 1, n_seq, d_conv))]
        args += list(init)

    out_specs = [row(d_rwkv)] * 7 + [row(d_conv)]
    out_shape = [jax.ShapeDtypeStruct((n, d_rwkv), F32)] * 7 + [
        jax.ShapeDtypeStruct((n, d_conv), BF16)]
    if full_out:
        out_specs += [row(p_shift), row(d_conv)]
        out_shape += [jax.ShapeDtypeStruct((n, p_shift), F32),
                      jax.ShapeDtypeStruct((n, d_conv), F32)]
    else:
        out_specs += [
            pl.BlockSpec((1, 1, p_shift), lambda i: (i // tiles_per_seq, 0, 0)),
            pl.BlockSpec((1, CONV_W - 1, d_conv), lambda i: (i // tiles_per_seq, 0, 0))]
        out_shape += [jax.ShapeDtypeStruct((n_seq, 1, p_shift), F32),
                      jax.ShapeDtypeStruct((n_seq, CONV_W - 1, d_conv), F32)]

    return pl.pallas_call(
        functools.partial(_premix_kernel, tiles_per_seq=tiles_per_seq, d_rwkv=d_rwkv,
                          d_conv=d_conv),
        grid=(n // tm,),
        in_specs=in_specs,
        out_specs=out_specs,
        out_shape=out_shape,
        scratch_shapes=[pltpu.VMEM((HDR + tm, p_total), F32)],
        compiler_params=pltpu.CompilerParams(
            dimension_semantics=("arbitrary",), vmem_limit_bytes=VMEM_LIMIT_BYTES),
        name="premix",
    )(*args)


def _bdot(a, b):
    return jnp.dot(a.astype(BF16), b.astype(BF16), preferred_element_type=F32)


def _bdot_nt(a, b):
    return lax.dot_general(a.astype(BF16), b.astype(BF16), (((1,), (1,)), ((), ())),
                           preferred_element_type=F32)


def _bdot_tn(a, b):
    return lax.dot_general(a.astype(BF16), b.astype(BF16), (((0,), (0,)), ((), ())),
                           preferred_element_type=F32)


def _wkv_kernel(r_ref, nkk_ref, b_ref, k_ref, v_ref, lw_ref, g_ref, s0_ref,
                rk_ref, lng_ref, lnb_ref, y_ref, sout_ref, s_scr, *, seqs, chunk, cps, n_heads):
    c = pl.program_id(1)
    n_pairs = n_heads // 2
    c2 = 2 * chunk
    lanes = 2 * HEAD

    @pl.when(c == 0)
    def _():
        for q in range(seqs):
            for p in range(n_pairs):
                s_scr[q, p] = jnp.concatenate([s0_ref[q, 2 * p], s0_ref[q, 2 * p + 1]], axis=1)

    t_row = lax.broadcasted_iota(jnp.int32, (chunk, 1), 0)
    n_doublings = chunk.bit_length() - 1

    ri = lax.broadcasted_iota(jnp.int32, (c2, c2), 0)
    ci = lax.broadcasted_iota(jnp.int32, (c2, c2), 1)
    same = (ri // chunk) == (ci // chunk)
    strict = same & (ri % chunk > ci % chunk)
    incl = same & (ri % chunk >= ci % chunk)
    row_head = lax.broadcasted_iota(jnp.int32, (c2, lanes), 0) // chunk
    lane_head = lax.broadcasted_iota(jnp.int32, (c2, lanes), 1) // HEAD
    own = row_head == lane_head
    lo = lane_head == 0
    g_own = (lax.broadcasted_iota(jnp.int32, (lanes, lanes), 0) // HEAD
             == lax.broadcasted_iota(jnp.int32, (lanes, lanes), 1) // HEAD)
    lo_c = lax.broadcasted_iota(jnp.int32, (chunk, lanes), 1) < HEAD

    def stack(x):
        return jnp.concatenate([x, x], axis=0)

    def bd(x):
        return jnp.where(own, stack(x), 0.0)

    def own_half(x, low):
        swapped = pltpu.roll(x, HEAD, axis=1)
        rows = jnp.concatenate([x, swapped] if low else [swapped, x], axis=0)
        return jnp.where(lo if low else ~lo, rows, 0.0)

    ar_bd, bk_st, a_lo, r_lo, v_hi, bkh_bd, gam, unit = [], [], [], [], [], [], [], []
    for cc, q in [(cc, q) for cc in range(cps) for q in range(seqs)]:
        rows = slice(cc * chunk, (cc + 1) * chunk)
        lw = lw_ref[q, rows, :]
        cum = lw
        for step in range(n_doublings):
            shift = 1 << step
            cum = cum + jnp.where(t_row >= shift, pltpu.roll(cum, shift, axis=0), 0.0)
        last = cum[chunk - 1:chunk, :]
        e_in = jnp.exp(cum)
        e_out = jnp.exp(-cum)
        e_tail = jnp.exp(last - cum)
        a_t = nkk_ref[q, rows, :] * jnp.exp(cum - lw)
        r_t = r_ref[q, rows, :] * e_in
        b_t = b_ref[q, rows, :] * e_out
        k_t = k_ref[q, rows, :] * e_out
        b_h = b_ref[q, rows, :] * e_tail
        k_h = k_ref[q, rows, :] * e_tail
        gamma = jnp.exp(last)
        v_all = v_ref[q, rows, :]
        for p in range(n_pairs):
            ps = slice(p * lanes, (p + 1) * lanes)
            ar_bd.append(jnp.concatenate([bd(a_t[:, ps]), bd(r_t[:, ps])], axis=0))
            bk_st.append(jnp.concatenate([stack(b_t[:, ps]), stack(k_t[:, ps])], axis=0))
            a_lo.append(own_half(a_t[:, ps], True))
            r_lo.append(own_half(r_t[:, ps], True))
            v_hi.append(own_half(v_all[:, ps], False))
            bkh_bd.append(jnp.concatenate([bd(b_h[:, ps]), bd(k_h[:, ps])], axis=0))
            gam.append(gamma[:, ps])
            unit.append((q, p, ps, rows))

    units = range(cps * seqs * n_pairs)
    m = [_bdot_nt(ar_bd[u], bk_st[u]) for u in units]
    l_ab = [jnp.where(strict, m[u][:c2, :c2], 0.0) for u in units]
    l_ak = [jnp.where(strict, m[u][:c2, c2:], 0.0) for u in units]
    l_rbk = [jnp.concatenate([jnp.where(incl, m[u][c2:, :c2], 0.0),
                              jnp.where(incl, m[u][c2:, c2:], 0.0)], axis=1) for u in units]
    y = [a_lo[u] + _bdot(l_ak[u], v_hi[u]) for u in units]
    pw = l_ab
    for step in range(n_doublings):
        skip = (1 << step) if (1 << step) % 8 == 0 else 0

        def live_rows(x):
            return x if skip == 0 else jnp.concatenate([x[skip:chunk], x[chunk + skip:]], axis=0)

        def all_rows(x):
            if skip == 0:
                return x
            zero = jnp.zeros((skip, x.shape[1]), x.dtype)
            return jnp.concatenate([zero, x[:chunk - skip], zero, x[chunk - skip:]], axis=0)

        if step < n_doublings - 1:
            py = [all_rows(_bdot(live_rows(pw[u]), jnp.concatenate([y[u], pw[u]], axis=1)))
                  for u in units]
            pw = [py[u][:, lanes:] for u in units]
            y = [y[u] + py[u][:, :lanes] for u in units]
        else:
            y = [y[u] + all_rows(_bdot(live_rows(pw[u]), y[u])) for u in units]
    yv = [jnp.concatenate([y[u], v_hi[u]], axis=0) for u in units]
    gh = [_bdot_tn(yv[u], bkh_bd[u]) for u in units]
    lyv = [_bdot(l_rbk[u], yv[u]) for u in units]
    q_lo = [jnp.where(lo, r_lo[u] + lyv[u], 0.0) for u in units]
    q_bd = [jnp.where(own, q_lo[u] + pltpu.roll(q_lo[u], HEAD, axis=1), 0.0) for u in units]
    g_bd = [jnp.where(g_own, stack(gh[u][:HEAD]), 0.0) for u in units]
    zero_s = jnp.zeros((HEAD, lanes), F32)
    per_chunk = seqs * n_pairs
    state = [s_scr[unit[u][0], unit[u][1]] for u in range(per_chunk)]
    o = []
    for cc in range(cps):
        us = range(cc * per_chunk, (cc + 1) * per_chunk)
        o += [_bdot_nt(q_bd[u], jnp.concatenate([zero_s, state[u % per_chunk]], axis=0))
              for u in us]
        sg = [_bdot(state[u % per_chunk], g_bd[u]) for u in us]
        state = [state[u % per_chunk] * gam[u] + sg[u % per_chunk] + gh[u][HEAD:] for u in us]
    for u in range(per_chunk):
        s_scr[unit[u][0], unit[u][1]] = state[u]

    head_sum = g_own.astype(BF16)
    o = [o[u] + lyv[u] for u in units]
    o = [jnp.where(lo_c, pltpu.roll(o[u][:chunk], HEAD, axis=1), o[u][chunk:]) for u in units]
    o_hi = [o[u].astype(BF16) for u in units]
    mu = [(jnp.dot(o_hi[u], head_sum, preferred_element_type=F32)
           + _bdot(o[u] - o_hi[u].astype(F32), head_sum)) * (1.0 / HEAD) for u in units]
    d = [o[u] - mu[u] for u in units]
    var = [_bdot(d[u] * d[u], head_sum) * (1.0 / HEAD) for u in units]
    bonus = [_bdot(r_ref[unit[u][0], unit[u][3], unit[u][2]] * k_ref[unit[u][0], unit[u][3], unit[u][2]]
                   * rk_ref[:, unit[u][2]], head_sum) for u in units]
    for cc, q in [(cc, q) for cc in range(cps) for q in range(seqs)]:
        y_pairs = []
        for p in range(n_pairs):
            u = (cc * seqs + q) * n_pairs + p
            _, _, ps, rows = unit[u]
            gn = d[u] * lax.rsqrt(var[u] + GN_EPS) * lng_ref[:, ps] + lnb_ref[:, ps]
            y_pairs.append((gn + bonus[u] * v_ref[q, rows, ps]) * g_ref[q, rows, ps])
        y_ref[q, cc * chunk:(cc + 1) * chunk, :] = jnp.concatenate(y_pairs, axis=1).astype(BF16)

    @pl.when(c == pl.num_programs(1) - 1)
    def _():
        for q in range(seqs):
            for p in range(n_pairs):
                sp = s_scr[q, p]
                sout_ref[q, 2 * p] = sp[:, :HEAD]
                sout_ref[q, 2 * p + 1] = sp[:, HEAD:]


def _wkv(ops, s0, rk, lng, lnb, *, seq_len, chunk, seqs, cps):
    n, d_rwkv = ops[0].shape
    n_seq, n_heads = s0.shape[0], s0.shape[1]
    assert n_seq % seqs == 0 and seq_len % (cps * chunk) == 0
    ops = [o.reshape(n_seq, seq_len, d_rwkv) for o in ops]
    tok = pl.BlockSpec((seqs, cps * chunk, d_rwkv), lambda q, c: (q, c, 0))
    state = pl.BlockSpec((seqs, n_heads, HEAD, HEAD), lambda q, c: (q, 0, 0, 0))
    y, s_new = pl.pallas_call(
        functools.partial(_wkv_kernel, seqs=seqs, chunk=chunk, cps=cps, n_heads=n_heads),
        grid=(n_seq // seqs, seq_len // (cps * chunk)),
        in_specs=[tok] * 7 + [state] + [pl.BlockSpec((1, d_rwkv), lambda q, c: (0, 0))] * 3,
        out_specs=[tok, state],
        out_shape=[jax.ShapeDtypeStruct((n_seq, seq_len, d_rwkv), BF16),
                   jax.ShapeDtypeStruct(s0.shape, F32)],
        scratch_shapes=[pltpu.VMEM((seqs, n_heads // 2, HEAD, 2 * HEAD), F32)],
        compiler_params=pltpu.CompilerParams(
            dimension_semantics=("arbitrary", "arbitrary"),
            vmem_limit_bytes=VMEM_LIMIT_BYTES),
        name="wkv",
    )(*ops, s0, rk, lng, lnb)
    return y.reshape(n, d_rwkv), s_new


def _wkv_steps_kernel(r_ref, nkk_ref, b_ref, k_ref, v_ref, lw_ref, g_ref, s0_ref,
                      rk_ref, lng_ref, lnb_ref, y_ref, sout_ref, vt_scr, o_scr, *, n_t):
    n_seq = s0_ref.shape[-1]

    def feature_major(ref, t):
        return ref[t * n_seq:(t + 1) * n_seq, :].T

    for t in range(n_t):
        src = s0_ref if t == 0 else sout_ref
        nkk = feature_major(nkk_ref, t)
        beta = feature_major(b_ref, t)
        key = feature_major(k_ref, t)
        rec = feature_major(r_ref, t)
        decay = jnp.exp(feature_major(lw_ref, t))
        vt_scr[...] = feature_major(v_ref, t)
        for h in range(2):
            hs = slice(h * HEAD, (h + 1) * HEAD)
            nkk_h, beta_h, key_h, rec_h, decay_h = nkk[hs], beta[hs], key[hs], rec[hs], decay[hs]

            def one_row(i, carry):
                s = src[h, i]
                sa = jnp.sum(s * nkk_h, axis=0, keepdims=True)
                v_i = vt_scr[pl.ds(h * HEAD + i, 1), :]
                s = s * decay_h + sa * beta_h + v_i * key_h
                sout_ref[h, i] = s
                o_scr[pl.ds(h * HEAD + i, 1), :] = jnp.sum(s * rec_h, axis=0, keepdims=True)
                return carry

            lax.fori_loop(0, HEAD, one_row, 0, unroll=8)

        v_t = vt_scr[...]
        bonus_prod = rec * key * rk_ref[...]
        ys = []
        for h in range(2):
            hs = slice(h * HEAD, (h + 1) * HEAD)
            o = o_scr[hs, :]
            mu = jnp.mean(o, axis=0, keepdims=True)
            d = o - mu
            var = jnp.mean(d * d, axis=0, keepdims=True)
            gn = d * lax.rsqrt(var + GN_EPS) * lng_ref[hs, :] + lnb_ref[hs, :]
            bonus = jnp.sum(bonus_prod[hs], axis=0, keepdims=True)
            ys.append(gn + bonus * v_t[hs])
        y_t = jnp.concatenate(ys, axis=0) * feature_major(g_ref, t)
        y_ref[t * n_seq:(t + 1) * n_seq, :] = y_t.T.astype(y_ref.dtype)


def _wkv_steps(ops, s0, rk, lng, lnb, *, n_t):
    n, d_rwkv = ops[0].shape
    n_heads, n_seq = s0.shape[0], s0.shape[-1]
    lanes = 2 * HEAD
    assert n == n_t * n_seq and n_seq == lanes

    def bcast(a):
        return jnp.broadcast_to(a.reshape(d_rwkv, 1), (d_rwkv, n_seq))

    tok = pl.BlockSpec((n, lanes), lambda p: (0, p))
    state = pl.BlockSpec((2, HEAD, HEAD, n_seq), lambda p: (p, 0, 0, 0))
    par = pl.BlockSpec((lanes, n_seq), lambda p: (p, 0))
    return pl.pallas_call(
        functools.partial(_wkv_steps_kernel, n_t=n_t),
        grid=(n_heads // 2,),
        in_specs=[tok] * 7 + [state] + [par] * 3,
        out_specs=[tok, state],
        out_shape=[jax.ShapeDtypeStruct((n, d_rwkv), BF16),
                   jax.ShapeDtypeStruct(s0.shape, F32)],
        scratch_shapes=[pltpu.VMEM((lanes, n_seq), F32), pltpu.VMEM((lanes, n_seq), F32)],
        compiler_params=pltpu.CompilerParams(
            dimension_semantics=("arbitrary",), vmem_limit_bytes=VMEM_LIMIT_BYTES),
        name="wkv_steps",
    )(*ops, s0, bcast(rk), bcast(lng), bcast(lnb))


def _out_ffn_kernel(xa_ref, yra_ref, yca_ref, xb_ref, yrb_ref, ycb_ref, wo_ref,
                    g2_ref, b2_ref, wg_ref, wu_ref, wd_ref, g3_ref, b3_ref, oa_ref, ob_ref, *, alpha):
    last = pl.num_programs(0) - 1
    d_rwkv = yra_ref.shape[1]
    for x_ref, yr_ref, yc_ref, o_ref, cond in (
            (xa_ref, yra_ref, yca_ref, oa_ref, pl.program_id(0) < last),
            (xb_ref, yrb_ref, ycb_ref, ob_ref, pl.program_id(0) == last)):
        @pl.when(cond)
        def _(x_ref=x_ref, yr_ref=yr_ref, yc_ref=yc_ref, o_ref=o_ref):
            mix = jnp.dot(yr_ref[...], wo_ref[:d_rwkv, :], preferred_element_type=F32)
            mix = mix + jnp.dot(yc_ref[...], wo_ref[d_rwkv:, :], preferred_element_type=F32)

            def mixed_rows(sl):
                return _layer_norm(alpha * x_ref[sl, :] + mix[sl], g2_ref[...], b2_ref[...])

            _ffn_post_ln(mixed_rows, wg_ref, wu_ref, wd_ref, g3_ref, b3_ref, o_ref, alpha,
                         FFN2_GROUP)


def _out_ffn(group_a, group_b, w_out, g2, b2, wg, wu, wd, g3, b3, alpha):
    d = group_a[0].shape[1]
    dff = wg.shape[1]
    d_rwkv, d_conv = group_a[1].shape[1], group_a[2].shape[1]
    grid, long_rows, short_rows = _two_group_grid(group_a[0].shape[0], group_b[0].shape[0])
    return pl.pallas_call(
        functools.partial(_out_ffn_kernel, alpha=alpha),
        grid=grid,
        in_specs=[long_rows(d), long_rows(d_rwkv), long_rows(d_conv),
                  short_rows(d), short_rows(d_rwkv), short_rows(d_conv),
                  _resident((d_rwkv + d_conv, d)),
                  _resident((1, d)), _resident((1, d)),
                  _resident((d, dff)), _resident((d, dff)), _resident((dff, d)),
                  _resident((1, d)), _resident((1, d))],
        out_specs=[long_rows(d), short_rows(d, single_buffer=False)],
        out_shape=[jax.ShapeDtypeStruct(group_a[0].shape, F32),
                   jax.ShapeDtypeStruct(group_b[0].shape, F32)],
        compiler_params=pltpu.CompilerParams(
            dimension_semantics=("arbitrary",), vmem_limit_bytes=VMEM_LIMIT_BYTES),
        name="out_ffn",
    )(*group_a, *group_b, w_out, g2, b2, wg, wu, wd, g3, b3)


def _trunk_layer(xp, xs, p, alpha, *, tp, ts, wkv0_p, wkv0_s, init_s):
    x1p, x1s = _ffn_ln(xp, xs, p["ffn1_wg"], p["ffn1_wu"], p["ffn1_wd"], p["ln1_g"], p["ln1_b"],
                       alpha)
    pm_p = _premix(x1p, p, seq_len=tp, init=None)
    pm_s = _premix(x1s, p, seq_len=ts, init=init_s)
    yr_p, wkv_p = _wkv(pm_p[:7], wkv0_p, p["r_k"], p["lnx_g"], p["lnx_b"],
                       seq_len=tp, chunk=CHUNK, seqs=wkv0_p.shape[0], cps=CHUNKS_PER_STEP)
    yr_s, wkv_s = _wkv_steps(pm_s[:7], wkv0_s, p["r_k"], p["lnx_g"], p["lnx_b"], n_t=ts)
    yp, ys = _out_ffn((x1p, yr_p, pm_p[7]), (x1s, yr_s, pm_s[7]), p["w_out"],
                      p["ln2_g"], p["ln2_b"], p["ffn2_wg"], p["ffn2_wu"], p["ffn2_wd"],
                      p["ln3_g"], p["ln3_b"], alpha)
    return (yp, wkv_p, pm_p[8], pm_p[9]), (ys, wkv_s, pm_s[8], pm_s[9])


def kernel(x_prompt, x_sample, state_wkv, state_shift, state_conv, ln1_g, ln1_b, ffn1_wg, ffn1_wu, ffn1_wd, w_in, mu_shift, w0, w_lora_up, a0, a_lora_up, g_lora_up, k_k, k_a, r_k, lnx_g, lnx_b, conv_w, w_out, ln2_g, ln2_b, ffn2_wg, ffn2_wu, ffn2_wd, ln3_g, ln3_b):
    depth = ln1_g.shape[0]
    bp, tp, d = x_prompt.shape
    bs, ts, _ = x_sample.shape
    n_heads = state_wkv.shape[2]
    alpha = (2.0 * depth) ** 0.25
    assert ts >= CONV_W - 1

    xp = x_prompt.reshape(bp * tp, d)
    xs = x_sample.transpose(1, 0, 2).reshape(ts * bs, d)

    outs = {k: [] for k in ("wkv_p", "shift_p", "conv_p", "wkv_s", "shift_s", "conv_s")}
    for l in range(depth):
        vec = lambda a: a[l].reshape(1, -1).astype(F32)
        p = {
            "ln1_g": vec(ln1_g), "ln1_b": vec(ln1_b),
            "ffn1_wg": ffn1_wg[l].astype(F32), "ffn1_wu": ffn1_wu[l].astype(F32),
            "ffn1_wd": ffn1_wd[l].astype(F32),
            "w_in": w_in[l].astype(BF16), "mu_shift": vec(mu_shift), "w0": vec(w0),
            "w_lora_up": w_lora_up[l].astype(BF16), "a0": vec(a0),
            "a_lora_up": a_lora_up[l].astype(BF16), "g_lora_up": g_lora_up[l].astype(BF16),
            "k_k": vec(k_k), "k_a": vec(k_a), "r_k": vec(r_k),
            "lnx_g": vec(lnx_g), "lnx_b": vec(lnx_b),
            "conv_w": conv_w[l].astype(F32), "w_out": w_out[l].astype(BF16),
            "ln2_g": vec(ln2_g), "ln2_b": vec(ln2_b),
            "ffn2_wg": ffn2_wg[l].astype(BF16), "ffn2_wu": ffn2_wu[l].astype(BF16),
            "ffn2_wd": ffn2_wd[l].astype(BF16),
            "ln3_g": vec(ln3_g), "ln3_b": vec(ln3_b),
        }
        wkv0_p = jnp.zeros((bp, n_heads, HEAD, HEAD), F32)
        init = (state_shift[l].astype(F32), state_conv[l].astype(F32).transpose(1, 0, 2))
        wkv0_s = state_wkv[l].astype(F32).transpose(1, 2, 3, 0)
        (xp, wp, sp, cp), (xs, wsm, ps_s, u_s) = _trunk_layer(
            xp, xs, p, alpha, tp=tp, ts=ts, wkv0_p=wkv0_p, wkv0_s=wkv0_s, init_s=init)
        outs["wkv_p"].append(wp)
        outs["shift_p"].append(sp[:, 0, :])
        outs["conv_p"].append(cp)
        outs["wkv_s"].append(wsm.transpose(3, 0, 1, 2))
        outs["shift_s"].append(ps_s[(ts - 1) * bs:])
        outs["conv_s"].append(
            u_s[(ts - (CONV_W - 1)) * bs:].reshape(CONV_W - 1, bs, -1).transpose(1, 0, 2))

    y_prompt = xp.reshape(bp, tp, d)
    y_sample = xs.reshape(ts, bs, d).transpose(1, 0, 2)
    return (y_prompt, y_sample,
            jnp.stack(outs["wkv_p"]), jnp.stack(outs["shift_p"]), jnp.stack(outs["conv_p"]),
            jnp.stack(outs["wkv_s"]), jnp.stack(outs["shift_s"]), jnp.stack(outs["conv_s"]))
```

```python
import functools
import math

import jax
import jax.numpy as jnp
from jax import lax
from jax.experimental import pallas as pl
from jax.experimental.pallas import tpu as pltpu

F32 = jnp.float32
BF16 = jnp.bfloat16

HEAD = 64
LORA_W = 64
LORA_A = 64
LORA_G = 128
CONV_W = 3
LN_EPS = 1e-5
GN_EPS = 1e-5 * HEAD

VMEM_LIMIT_BYTES = 56 * 1024 * 1024
TOKEN_TILE = 512
FFN_TILE = 1024
FFN1_GROUP = 256
FFN2_GROUP = 512
CHUNK = 64
CHUNKS_PER_STEP = 1


def _resident(shape):
    nd = len(shape)
    return pl.BlockSpec(shape, lambda *_: (0,) * nd, pipeline_mode=pl.Buffered(1))


def _layer_norm(x, g, b):
    mu = jnp.mean(x, axis=-1, keepdims=True)
    d = x - mu
    var = jnp.mean(d * d, axis=-1, keepdims=True)
    return d * lax.rsqrt(var + LN_EPS) * g + b


def _ffn_post_ln(rows_of, wg_ref, wu_ref, wd_ref, g_ref, b_ref, o_ref, alpha, group):
    for s in range(o_ref.shape[0] // group):
        sl = slice(s * group, (s + 1) * group)
        xs = rows_of(sl)
        xb = xs.astype(BF16)
        g = jnp.dot(xb, wg_ref[...], preferred_element_type=F32)
        u = jnp.dot(xb, wu_ref[...], preferred_element_type=F32)
        h = (jax.nn.silu(g) * u).astype(BF16)
        y = alpha * xs + 0.5 * jnp.dot(h, wd_ref[...], preferred_element_type=F32)
        o_ref[sl, :] = _layer_norm(y, g_ref[...], b_ref[...])


def _two_group_grid(n_long, n_short):
    tm = FFN_TILE
    assert n_long % tm == 0 and n_short % max(FFN1_GROUP, FFN2_GROUP) == 0
    n_tiles = n_long // tm

    def long_rows(c):
        return pl.BlockSpec((tm, c), lambda i: (jnp.minimum(i, n_tiles - 1), 0))

    def short_rows(c, single_buffer=True):
        mode = dict(pipeline_mode=pl.Buffered(1)) if single_buffer else {}
        return pl.BlockSpec((n_short, c), lambda i: (0, 0), **mode)

    return (n_tiles + 1,), long_rows, short_rows


WEIGHT_CAST_PARTS = 16


def _load_bf16(src_hbm, dst_ref, stage, sem):
    rows = stage.shape[1]

    def copy(i):
        return pltpu.make_async_copy(src_hbm.at[pl.ds(i * rows, rows), :], stage.at[i % 2],
                                     sem.at[i % 2])

    copy(0).start()
    for i in range(WEIGHT_CAST_PARTS):
        if i + 1 < WEIGHT_CAST_PARTS:
            copy(i + 1).start()
        copy(i).wait()
        dst_ref[i * rows:(i + 1) * rows, :] = stage[i % 2].astype(BF16)


def _ffn_ln_kernel(xa_ref, xb_ref, wg_hbm, wu_hbm, wd_hbm, g_ref, b_ref, oa_ref, ob_ref,
                   wg_ref, wu_ref, wd_ref, stage_up, stage_down, sem, *, alpha):
    @pl.when(pl.program_id(0) == 0)
    def _():
        _load_bf16(wg_hbm, wg_ref, stage_up, sem)
        _load_bf16(wu_hbm, wu_ref, stage_up, sem)
        _load_bf16(wd_hbm, wd_ref, stage_down, sem)

    last = pl.num_programs(0) - 1
    for x_ref, o_ref, cond in ((xa_ref, oa_ref, pl.program_id(0) < last),
                               (xb_ref, ob_ref, pl.program_id(0) == last)):
        @pl.when(cond)
        def _(x_ref=x_ref, o_ref=o_ref):
            _ffn_post_ln(lambda sl: x_ref[sl, :], wg_ref, wu_ref, wd_ref, g_ref, b_ref,
                         o_ref, alpha, FFN1_GROUP)


def _ffn_ln(xa, xb, wg, wu, wd, g, b, alpha):
    d = xa.shape[1]
    dff = wg.shape[1]
    grid, long_rows, short_rows = _two_group_grid(xa.shape[0], xb.shape[0])
    return pl.pallas_call(
        functools.partial(_ffn_ln_kernel, alpha=alpha),
        grid=grid,
        in_specs=[long_rows(d), short_rows(d)] + [pl.BlockSpec(memory_space=pl.ANY)] * 3
                 + [_resident((1, d)), _resident((1, d))],
        out_specs=[long_rows(d), short_rows(d, single_buffer=False)],
        out_shape=[jax.ShapeDtypeStruct(xa.shape, F32), jax.ShapeDtypeStruct(xb.shape, F32)],
        scratch_shapes=[pltpu.VMEM((d, dff), BF16), pltpu.VMEM((d, dff), BF16),
                        pltpu.VMEM((dff, d), BF16),
                        pltpu.VMEM((2, d // WEIGHT_CAST_PARTS, dff), F32),
                        pltpu.VMEM((2, dff // WEIGHT_CAST_PARTS, d), F32),
                        pltpu.SemaphoreType.DMA((2,))],
        compiler_params=pltpu.CompilerParams(
            dimension_semantics=("arbitrary",), vmem_limit_bytes=VMEM_LIMIT_BYTES),
        name="ffn_ln",
    )(xa, xb, wg, wu, wd, g, b)


HDR = 8


def _shift_rows(x, head, k):
    rolled = pltpu.roll(x, k, axis=0)
    row8 = lax.broadcasted_iota(jnp.int32, (HDR, 1), 0)
    top = jnp.where(row8 < k, pltpu.roll(head, k, axis=0), rolled[:HDR])
    return jnp.concatenate([top, rolled[HDR:]], axis=0)


class _PremixRefs:
    def __init__(self, refs, tiles_per_seq, full_out):
        refs = list(refs)
        (self.x, self.win, self.mu, self.w0, self.wup, self.a0, self.aup, self.gup, self.kk,
         self.ka, self.cw) = refs[:11]
        refs = refs[11:]
        if tiles_per_seq == 0:
            self.sh0, self.cv0 = refs[:2]
            refs = refs[2:]
        (self.r, self.nkk, self.b, self.k, self.v, self.lw, self.g, self.yc) = refs[:8]
        refs = refs[8:]
        if full_out:
            self.ps_out, self.u_out = refs[:2]
        else:
            self.shift, self.tail = refs[:2]
        self.bufs = refs[2:]


def _premix_gates(rf, buf, first, *, tiles_per_seq, d_rwkv, d_conv):
    p_shift = 3 * d_rwkv + LORA_W + LORA_A + LORA_G
    tm = rf.x.shape[0]
    ps = buf[HDR:, :p_shift]
    c_b = buf[HDR:, p_shift:p_shift + d_conv]
    u = buf[HDR:, p_shift + d_conv:p_shift + 2 * d_conv] * buf[HDR:, p_shift + 2 * d_conv:]

    if tiles_per_seq:
        head = jnp.where(first, 0.0, buf[0:HDR, :])
        head_u = head[:, p_shift + d_conv:p_shift + 2 * d_conv] * head[:, p_shift + 2 * d_conv:]
        prev = _shift_rows(ps, head[:, :p_shift], 1)
        um1 = _shift_rows(u, head_u, 1)
        um2 = _shift_rows(u, head_u, 2)
    else:
        n_seq = rf.sh0.shape[0]
        prev = jnp.concatenate([rf.sh0[...], ps[:tm - n_seq]], axis=0)
        hist = jnp.concatenate([rf.cv0[t] for t in range(CONV_W - 1)] + [u], axis=0)
        um2 = hist[:tm]
        um1 = hist[n_seq:n_seq + tm]

    cw = rf.cw[...]
    z = cw[0:1, :] * um2
    z = z + cw[1:2, :] * um1
    z = z + cw[2:3, :] * u
    rf.yc[...] = (c_b * z).astype(BF16)
    if tiles_per_seq:
        rf.shift[0] = ps[tm - 1:tm, :]
        rf.tail[0] = u[tm - (CONV_W - 1):tm, :]
    else:
        rf.ps_out[...] = ps
        rf.u_out[...] = u

    mix = ps + rf.mu[...] * (prev - ps)
    r = mix[:, 0:d_rwkv]
    k = mix[:, d_rwkv:2 * d_rwkv]
    v = mix[:, 2 * d_rwkv:3 * d_rwkv]
    o = 3 * d_rwkv
    wd = mix[:, o:o + LORA_W]
    ad = mix[:, o + LORA_W:o + LORA_W + LORA_A]
    gd = mix[:, o + LORA_W + LORA_A:p_shift]

    rf.r[...] = r
    kk = k * rf.kk[...]
    dot_in = (jnp.tanh(wd).astype(BF16), ad.astype(BF16), jax.nn.sigmoid(gd).astype(BF16),
              kk * kk)
    return k, v, kk, dot_in


def _premix_dots(rf, dot_in):
    tanh_wd, ad, sig_gd, kk_sq = dot_in
    w_pre = jnp.dot(tanh_wd, rf.wup[...], preferred_element_type=F32)
    a_pre = jnp.dot(ad, rf.aup[...], preferred_element_type=F32)
    rf.g[...] = jnp.dot(sig_gd, rf.gup[...], preferred_element_type=F32)
    low = lax.broadcasted_iota(jnp.int32, (1, 2 * HEAD), 1) < HEAD
    parts = []
    for c0 in range(0, kk_sq.shape[1], 2 * HEAD):
        x = kk_sq[:, c0:c0 + 2 * HEAD]
        s_lo = jnp.sum(jnp.where(low, x, 0.0), axis=-1, keepdims=True)
        s_hi = jnp.sum(jnp.where(low, 0.0, x), axis=-1, keepdims=True)
        parts.append(jnp.where(low, s_lo, s_hi))
    ss = jnp.concatenate(parts, axis=1)
    return w_pre, a_pre, ss


def _premix_finish(rf, gates, dots):
    k, v, kk, _ = gates
    w_pre, a_pre, ss = dots
    log_decay = -math.exp(-0.5) * jax.nn.sigmoid(rf.w0[...] + w_pre)
    a = jax.nn.sigmoid(rf.a0[...] + a_pre)
    kk = kk * lax.rsqrt(jnp.maximum(ss, 1e-24))
    k = k * (1.0 + (a - 1.0) * rf.ka[...])
    nkk = -kk
    b = kk * a
    rf.nkk[...] = nkk
    rf.b[...] = b
    rf.k[...] = k
    rf.v[...] = v
    rf.lw[...] = log_decay


def _premix_kernel(*refs, tiles_per_seq, d_rwkv, d_conv):
    rf = _PremixRefs(refs, tiles_per_seq, full_out=tiles_per_seq == 0)
    buf, = rf.bufs
    tm = rf.x.shape[0]
    i = pl.program_id(0)
    cfg = dict(tiles_per_seq=tiles_per_seq, d_rwkv=d_rwkv, d_conv=d_conv)

    if tiles_per_seq:
        @pl.when(i == 0)
        def _():
            buf[0:HDR, :] = jnp.zeros((HDR, buf.shape[1]), F32)

    buf[HDR:, :] = jnp.dot(rf.x[...].astype(BF16), rf.win[...], preferred_element_type=F32)
    first = (i % tiles_per_seq == 0) if tiles_per_seq else None
    gates = _premix_gates(rf, buf, first, **cfg)
    _premix_finish(rf, gates, _premix_dots(rf, gates[3]))
    if tiles_per_seq:
        buf[0:HDR, :] = buf[tm:tm + HDR, :]


def _premix(x1, p, *, seq_len, init=None):
    n, d = x1.shape
    tm = TOKEN_TILE
    d_rwkv = p["w0"].shape[1]
    d_conv = p["conv_w"].shape[1]
    p_shift = 3 * d_rwkv + LORA_W + LORA_A + LORA_G
    p_total = p_shift + 3 * d_conv
    n_seq = n // seq_len
    if init is None:
        assert seq_len % tm == 0
        tiles_per_seq = seq_len // tm
    else:
        assert n == tm and n_seq % HDR == 0
        tiles_per_seq = 0
    full_out = init is not None

    def row(c):
        return pl.BlockSpec((tm, c), lambda i: (i, 0))

    in_specs = [row(d), _resident((d, p_total)), _resident((1, p_shift)),
                _resident((1, d_rwkv)), _resident((LORA_W, d_rwkv)),
                _resident((1, d_rwkv)), _resident((LORA_A, d_rwkv)),
                _resident((LORA_G, d_rwkv)), _resident((1, d_rwkv)), _resident((1, d_rwkv)),
                _resident((CONV_W, d_conv))]
    args = [x1, p["w_in"], p["mu_shift"], p["w0"], p["w_lora_up"], p["a0"], p["a_lora_up"],
            p["g_lora_up"], p["k_k"], p["k_a"], p["conv_w"]]
    if init is not None:
        in_specs += [_resident((n_seq, p_shift)), _resident((CONV_W - 1, n_seq, d_conv))]
        args += list(init)

    out_specs = [row(d_rwkv)] * 7 + [row(d_conv)]
    out_shape = [jax.ShapeDtypeStruct((n, d_rwkv), F32)] * 7 + [
        jax.ShapeDtypeStruct((n, d_conv), BF16)]
    if full_out:
        out_specs += [row(p_shift), row(d_conv)]
        out_shape += [jax.ShapeDtypeStruct((n, p_shift), F32),
                      jax.ShapeDtypeStruct((n, d_conv), F32)]
    else:
        out_specs += [
            pl.BlockSpec((1, 1, p_shift), lambda i: (i // tiles_per_seq, 0, 0)),
            pl.BlockSpec((1, CONV_W - 1, d_conv), lambda i: (i // tiles_per_seq, 0, 0))]
        out_shape += [jax.ShapeDtypeStruct((n_seq, 1, p_shift), F32),
                      jax.ShapeDtypeStruct((n_seq, CONV_W - 1, d_conv), F32)]

    return pl.pallas_call(
        functools.partial(_premix_kernel, tiles_per_seq=tiles_per_seq, d_rwkv=d_rwkv,
                          d_conv=d_conv),
        grid=(n // tm,),
        in_specs=in_specs,
        out_specs=out_specs,
        out_shape=out_shape,
        scratch_shapes=[pltpu.VMEM((HDR + tm, p_total), F32)],
        compiler_params=pltpu.CompilerParams(
            dimension_semantics=("arbitrary",), vmem_limit_bytes=VMEM_LIMIT_BYTES),
        name="premix",
    )(*args)


def _bdot(a, b):
    return jnp.dot(a.astype(BF16), b.astype(BF16), preferred_element_type=F32)


def _bdot_nt(a, b):
    return lax.dot_general(a.astype(BF16), b.astype(BF16), (((1,), (1,)), ((), ())),
                           preferred_element_type=F32)


def _bdot_tn(a, b):
    return lax.dot_general(a.astype(BF16), b.astype(BF16), (((0,), (0,)), ((), ())),
                           preferred_element_type=F32)


def _wkv_kernel(r_ref, nkk_ref, b_ref, k_ref, v_ref, lw_ref, g_ref, s0_ref,
                rk_ref, lng_ref, lnb_ref, y_ref, sout_ref, s_scr, *, seqs, chunk, cps, n_heads):
    c = pl.program_id(1)
    n_pairs = n_heads // 2
    c2 = 2 * chunk
    lanes = 2 * HEAD

    @pl.when(c == 0)
    def _():
        for q in range(seqs):
            for p in range(n_pairs):
                s_scr[q, p] = jnp.concatenate([s0_ref[q, 2 * p], s0_ref[q, 2 * p + 1]], axis=1)

    t_row = lax.broadcasted_iota(jnp.int32, (chunk, 1), 0)
    n_doublings = chunk.bit_length() - 1

    ri = lax.broadcasted_iota(jnp.int32, (c2, c2), 0)
    ci = lax.broadcasted_iota(jnp.int32, (c2, c2), 1)
    same = (ri // chunk) == (ci // chunk)
    strict = same & (ri % chunk > ci % chunk)
    incl = same & (ri % chunk >= ci % chunk)
    row_head = lax.broadcasted_iota(jnp.int32, (c2, lanes), 0) // chunk
    lane_head = lax.broadcasted_iota(jnp.int32, (c2, lanes), 1) // HEAD
    own = row_head == lane_head
    lo = lane_head == 0
    g_own = (lax.broadcasted_iota(jnp.int32, (lanes, lanes), 0) // HEAD
             == lax.broadcasted_iota(jnp.int32, (lanes, lanes), 1) // HEAD)
    lo_c = lax.broadcasted_iota(jnp.int32, (chunk, lanes), 1) < HEAD

    def stack(x):
        return jnp.concatenate([x, x], axis=0)

    def bd(x):
        return jnp.where(own, stack(x), 0.0)

    def own_half(x, low):
        swapped = pltpu.roll(x, HEAD, axis=1)
        rows = jnp.concatenate([x, swapped] if low else [swapped, x], axis=0)
        return jnp.where(lo if low else ~lo, rows, 0.0)

    ar_bd, bk_st, a_lo, r_lo, v_hi, bkh_bd, gam, unit = [], [], [], [], [], [], [], []
    for cc, q in [(cc, q) for cc in range(cps) for q in range(seqs)]:
        rows = slice(cc * chunk, (cc + 1) * chunk)
        lw = lw_ref[q, rows, :]
        cum = lw
        for step in range(n_doublings):
            shift = 1 << step
            cum = cum + jnp.where(t_row >= shift, pltpu.roll(cum, shift, axis=0), 0.0)
        last = cum[chunk - 1:chunk, :]
        e_in = jnp.exp(cum)
        e_out = jnp.exp(-cum)
        e_tail = jnp.exp(last - cum)
        a_t = nkk_ref[q, rows, :] * jnp.exp(cum - lw)
        r_t = r_ref[q, rows, :] * e_in
        b_t = b_ref[q, rows, :] * e_out
        k_t = k_ref[q, rows, :] * e_out
        b_h = b_ref[q, rows, :] * e_tail
        k_h = k_ref[q, rows, :] * e_tail
        gamma = jnp.exp(last)
        v_all = v_ref[q, rows, :]
        for p in range(n_pairs):
            ps = slice(p * lanes, (p + 1) * lanes)
            ar_bd.append(jnp.concatenate([bd(a_t[:, ps]), bd(r_t[:, ps])], axis=0))
            bk_st.append(jnp.concatenate([stack(b_t[:, ps]), stack(k_t[:, ps])], axis=0))
            a_lo.append(own_half(a_t[:, ps], True))
            r_lo.append(own_half(r_t[:, ps], True))
            v_hi.append(own_half(v_all[:, ps], False))
            bkh_bd.append(jnp.concatenate([bd(b_h[:, ps]), bd(k_h[:, ps])], axis=0))
            gam.append(gamma[:, ps])
            unit.append((q, p, ps, rows))

    units = range(cps * seqs * n_pairs)
    m = [_bdot_nt(ar_bd[u], bk_st[u]) for u in units]
    l_ab = [jnp.where(strict, m[u][:c2, :c2], 0.0) for u in units]
    l_ak = [jnp.where(strict, m[u][:c2, c2:], 0.0) for u in units]
    l_rbk = [jnp.concatenate([jnp.where(incl, m[u][c2:, :c2], 0.0),
                              jnp.where(incl, m[u][c2:, c2:], 0.0)], axis=1) for u in units]
    y = [a_lo[u] + _bdot(l_ak[u], v_hi[u]) for u in units]
    pw = l_ab
    for step in range(n_doublings):
        skip = (1 << step) if (1 << step) % 8 == 0 else 0

        def live_rows(x):
            return x if skip == 0 else jnp.concatenate([x[skip:chunk], x[chunk + skip:]], axis=0)

        def all_rows(x):
            if skip == 0:
                return x
            zero = jnp.zeros((skip, x.shape[1]), x.dtype)
            return jnp.concatenate([zero, x[:chunk - skip], zero, x[chunk - skip:]], axis=0)

        if step < n_doublings - 1:
            py = [all_rows(_bdot(live_rows(pw[u]), jnp.concatenate([y[u], pw[u]], axis=1)))
                  for u in units]
            pw = [py[u][:, lanes:] for u in units]
            y = [y[u] + py[u][:, :lanes] for u in units]
        else:
            y = [y[u] + all_rows(_bdot(live_rows(pw[u]), y[u])) for u in units]
    yv = [jnp.concatenate([y[u], v_hi[u]], axis=0) for u in units]
    gh = [_bdot_tn(yv[u], bkh_bd[u]) for u in units]
    lyv = [_bdot(l_rbk[u], yv[u]) for u in units]
    q_lo = [jnp.where(lo, r_lo[u] + lyv[u], 0.0) for u in units]
    q_bd = [jnp.where(own, q_lo[u] + pltpu.roll(q_lo[u], HEAD, axis=1), 0.0) for u in units]
    g_bd = [jnp.where(g_own, stack(gh[u][:HEAD]), 0.0) for u in units]
    zero_s = jnp.zeros((HEAD, lanes), F32)
    per_chunk = seqs * n_pairs
    state = [s_scr[unit[u][0], unit[u][1]] for u in range(per_chunk)]
    o = []
    for cc in range(cps):
        us = range(cc * per_chunk, (cc + 1) * per_chunk)
        o += [_bdot_nt(q_bd[u], jnp.concatenate([zero_s, state[u % per_chunk]], axis=0))
              for u in us]
        sg = [_bdot(state[u % per_chunk], g_bd[u]) for u in us]
        state = [state[u % per_chunk] * gam[u] + sg[u % per_chunk] + gh[u][HEAD:] for u in us]
    for u in range(per_chunk):
        s_scr[unit[u][0], unit[u][1]] = state[u]

    head_sum = g_own.astype(BF16)
    o = [o[u] + lyv[u] for u in units]
    o = [jnp.where(lo_c, pltpu.roll(o[u][:chunk], HEAD, axis=1), o[u][chunk:]) for u in units]
    o_hi = [o[u].astype(BF16) for u in units]
    mu = [(jnp.dot(o_hi[u], head_sum, preferred_element_type=F32)
           + _bdot(o[u] - o_hi[u].astype(F32), head_sum)) * (1.0 / HEAD) for u in units]
    d = [o[u] - mu[u] for u in units]
    var = [_bdot(d[u] * d[u], head_sum) * (1.0 / HEAD) for u in units]
    bonus = [_bdot(r_ref[unit[u][0], unit[u][3], unit[u][2]] * k_ref[unit[u][0], unit[u][3], unit[u][2]]
                   * rk_ref[:, unit[u][2]], head_sum) for u in units]
    for cc, q in [(cc, q) for cc in range(cps) for q in range(seqs)]:
        y_pairs = []
        for p in range(n_pairs):
            u = (cc * seqs + q) * n_pairs + p
            _, _, ps, rows = unit[u]
            gn = d[u] * lax.rsqrt(var[u] + GN_EPS) * lng_ref[:, ps] + lnb_ref[:, ps]
            y_pairs.append((gn + bonus[u] * v_ref[q, rows, ps]) * g_ref[q, rows, ps])
        y_ref[q, cc * chunk:(cc + 1) * chunk, :] = jnp.concatenate(y_pairs, axis=1).astype(BF16)

    @pl.when(c == pl.num_programs(1) - 1)
    def _():
        for q in range(seqs):
            for p in range(n_pairs):
                sp = s_scr[q, p]
                sout_ref[q, 2 * p] = sp[:, :HEAD]
                sout_ref[q, 2 * p + 1] = sp[:, HEAD:]


def _wkv(ops, s0, rk, lng, lnb, *, seq_len, chunk, seqs, cps):
    n, d_rwkv = ops[0].shape
    n_seq, n_heads = s0.shape[0], s0.shape[1]
    assert n_seq % seqs == 0 and seq_len % (cps * chunk) == 0
    ops = [o.reshape(n_seq, seq_len, d_rwkv) for o in ops]
    tok = pl.BlockSpec((seqs, cps * chunk, d_rwkv), lambda q, c: (q, c, 0))
    state = pl.BlockSpec((seqs, n_heads, HEAD, HEAD), lambda q, c: (q, 0, 0, 0))
    y, s_new = pl.pallas_call(
        functools.partial(_wkv_kernel, seqs=seqs, chunk=chunk, cps=cps, n_heads=n_heads),
        grid=(n_seq // seqs, seq_len // (cps * chunk)),
        in_specs=[tok] * 7 + [state] + [pl.BlockSpec((1, d_rwkv), lambda q, c: (0, 0))] * 3,
        out_specs=[tok, state],
        out_shape=[jax.ShapeDtypeStruct((n_seq, seq_len, d_rwkv), BF16),
                   jax.ShapeDtypeStruct(s0.shape, F32)],
        scratch_shapes=[pltpu.VMEM((seqs, n_heads // 2, HEAD, 2 * HEAD), F32)],
        compiler_params=pltpu.CompilerParams(
            dimension_semantics=("arbitrary", "arbitrary"),
            vmem_limit_bytes=VMEM_LIMIT_BYTES),
        name="wkv",
    )(*ops, s0, rk, lng, lnb)
    return y.reshape(n, d_rwkv), s_new


def _wkv_steps_kernel(r_ref, nkk_ref, b_ref, k_ref, v_ref, lw_ref, g_ref, s0_ref,
                      rk_ref, lng_ref, lnb_ref, y_ref, sout_ref, vt_scr, o_scr, *, n_t):
    n_seq = s0_ref.shape[-1]

    def feature_major(ref, t):
        return ref[t * n_seq:(t + 1) * n_seq, :].T

    for t in range(n_t):
        src = s0_ref if t == 0 else sout_ref
        nkk = feature_major(nkk_ref, t)
        beta = feature_major(b_ref, t)
        key = feature_major(k_ref, t)
        rec = feature_major(r_ref, t)
        decay = jnp.exp(feature_major(lw_ref, t))
        vt_scr[...] = feature_major(v_ref, t)
        for h in range(2):
            hs = slice(h * HEAD, (h + 1) * HEAD)
            nkk_h, beta_h, key_h, rec_h, decay_h = nkk[hs], beta[hs], key[hs], rec[hs], decay[hs]

            def one_row(i, carry):
                s = src[h, i]
                sa = jnp.sum(s * nkk_h, axis=0, keepdims=True)
                v_i = vt_scr[pl.ds(h * HEAD + i, 1), :]
                s = s * decay_h + sa * beta_h + v_i * key_h
                sout_ref[h, i] = s
                o_scr[pl.ds(h * HEAD + i, 1), :] = jnp.sum(s * rec_h, axis=0, keepdims=True)
                return carry

            lax.fori_loop(0, HEAD, one_row, 0, unroll=8)

        v_t = vt_scr[...]
        bonus_prod = rec * key * rk_ref[...]
        ys = []
        for h in range(2):
            hs = slice(h * HEAD, (h + 1) * HEAD)
            o = o_scr[hs, :]
            mu = jnp.mean(o, axis=0, keepdims=True)
            d = o - mu
            var = jnp.mean(d * d, axis=0, keepdims=True)
            gn = d * lax.rsqrt(var + GN_EPS) * lng_ref[hs, :] + lnb_ref[hs, :]
            bonus = jnp.sum(bonus_prod[hs], axis=0, keepdims=True)
            ys.append(gn + bonus * v_t[hs])
        y_t = jnp.concatenate(ys, axis=0) * feature_major(g_ref, t)
        y_ref[t * n_seq:(t + 1) * n_seq, :] = y_t.T.astype(y_ref.dtype)


def _wkv_steps(ops, s0, rk, lng, lnb, *, n_t):
    n, d_rwkv = ops[0].shape
    n_heads, n_seq = s0.shape[0], s0.shape[-1]
    lanes = 2 * HEAD
    assert n == n_t * n_seq and n_seq == lanes

    def bcast(a):
        return jnp.broadcast_to(a.reshape(d_rwkv, 1), (d_rwkv, n_seq))

    tok = pl.BlockSpec((n, lanes), lambda p: (0, p))
    state = pl.BlockSpec((2, HEAD, HEAD, n_seq), lambda p: (p, 0, 0, 0))
    par = pl.BlockSpec((lanes, n_seq), lambda p: (p, 0))
    return pl.pallas_call(
        functools.partial(_wkv_steps_kernel, n_t=n_t),
        grid=(n_heads // 2,),
        in_specs=[tok] * 7 + [state] + [par] * 3,
        out_specs=[tok, state],
        out_shape=[jax.ShapeDtypeStruct((n, d_rwkv), BF16),
                   jax.ShapeDtypeStruct(s0.shape, F32)],
        scratch_shapes=[pltpu.VMEM((lanes, n_seq), F32), pltpu.VMEM((lanes, n_seq), F32)],
        compiler_params=pltpu.CompilerParams(
            dimension_semantics=("arbitrary",), vmem_limit_bytes=VMEM_LIMIT_BYTES),
        name="wkv_steps",
    )(*ops, s0, bcast(rk), bcast(lng), bcast(lnb))


def _out_ffn_kernel(xa_ref, yra_ref, yca_ref, xb_ref, yrb_ref, ycb_ref, wo_ref,
                    g2_ref, b2_ref, wg_hbm, wu_hbm, wd_hbm, g3_ref, b3_ref, oa_ref, ob_ref,
                    wg_ref, wu_ref, wd_ref, stage_up, stage_down, sem, *, alpha):
    @pl.when(pl.program_id(0) == 0)
    def _():
        _load_bf16(wg_hbm, wg_ref, stage_up, sem)
        _load_bf16(wu_hbm, wu_ref, stage_up, sem)
        _load_bf16(wd_hbm, wd_ref, stage_down, sem)

    last = pl.num_programs(0) - 1
    d_rwkv = yra_ref.shape[1]
    for x_ref, yr_ref, yc_ref, o_ref, cond in (
            (xa_ref, yra_ref, yca_ref, oa_ref, pl.program_id(0) < last),
            (xb_ref, yrb_ref, ycb_ref, ob_ref, pl.program_id(0) == last)):
        @pl.when(cond)
        def _(x_ref=x_ref, yr_ref=yr_ref, yc_ref=yc_ref, o_ref=o_ref):
            mix = jnp.dot(yr_ref[...], wo_ref[:d_rwkv, :], preferred_element_type=F32)
            mix = mix + jnp.dot(yc_ref[...], wo_ref[d_rwkv:, :], preferred_element_type=F32)

            def mixed_rows(sl):
                return _layer_norm(alpha * x_ref[sl, :] + mix[sl], g2_ref[...], b2_ref[...])

            _ffn_post_ln(mixed_rows, wg_ref, wu_ref, wd_ref, g3_ref, b3_ref, o_ref, alpha,
                         FFN2_GROUP)


def _out_ffn(group_a, group_b, w_out, g2, b2, wg, wu, wd, g3, b3, alpha):
    d = group_a[0].shape[1]
    dff = wg.shape[1]
    d_rwkv, d_conv = group_a[1].shape[1], group_a[2].shape[1]
    grid, long_rows, short_rows = _two_group_grid(group_a[0].shape[0], group_b[0].shape[0])
    return pl.pallas_call(
        functools.partial(_out_ffn_kernel, alpha=alpha),
        grid=grid,
        in_specs=[long_rows(d), long_rows(d_rwkv), long_rows(d_conv),
                  short_rows(d), short_rows(d_rwkv), short_rows(d_conv),
                  _resident((d_rwkv + d_conv, d)),
                  _resident((1, d)), _resident((1, d)),
                  *[pl.BlockSpec(memory_space=pl.ANY)] * 3,
                  _resident((1, d)), _resident((1, d))],
        out_specs=[long_rows(d), short_rows(d, single_buffer=False)],
        out_shape=[jax.ShapeDtypeStruct(group_a[0].shape, F32),
                   jax.ShapeDtypeStruct(group_b[0].shape, F32)],
        scratch_shapes=[pltpu.VMEM((d, dff), BF16), pltpu.VMEM((d, dff), BF16),
                        pltpu.VMEM((dff, d), BF16),
                        pltpu.VMEM((2, d // WEIGHT_CAST_PARTS, dff), F32),
                        pltpu.VMEM((2, dff // WEIGHT_CAST_PARTS, d), F32),
                        pltpu.SemaphoreType.DMA((2,))],
        compiler_params=pltpu.CompilerParams(
            dimension_semantics=("arbitrary",),
            vmem_limit_bytes=VMEM_LIMIT_BYTES + 4 * 1024 * 1024),
        name="out_ffn",
    )(*group_a, *group_b, w_out, g2, b2, wg, wu, wd, g3, b3)


def _trunk_layer(xp, xs, p, alpha, *, tp, ts, wkv0_p, wkv0_s, init_s):
    x1p, x1s = _ffn_ln(xp, xs, p["ffn1_wg"], p["ffn1_wu"], p["ffn1_wd"], p["ln1_g"], p["ln1_b"],
                       alpha)
    pm_p = _premix(x1p, p, seq_len=tp, init=None)
    pm_s = _premix(x1s, p, seq_len=ts, init=init_s)
    yr_p, wkv_p = _wkv(pm_p[:7], wkv0_p, p["r_k"], p["lnx_g"], p["lnx_b"],
                       seq_len=tp, chunk=CHUNK, seqs=wkv0_p.shape[0], cps=CHUNKS_PER_STEP)
    yr_s, wkv_s = _wkv_steps(pm_s[:7], wkv0_s, p["r_k"], p["lnx_g"], p["lnx_b"], n_t=ts)
    yp, ys = _out_ffn((x1p, yr_p, pm_p[7]), (x1s, yr_s, pm_s[7]), p["w_out"],
                      p["ln2_g"], p["ln2_b"], p["ffn2_wg"], p["ffn2_wu"], p["ffn2_wd"],
                      p["ln3_g"], p["ln3_b"], alpha)
    return (yp, wkv_p, pm_p[8], pm_p[9]), (ys, wkv_s, pm_s[8], pm_s[9])


def kernel(x_prompt, x_sample, state_wkv, state_shift, state_conv, ln1_g, ln1_b, ffn1_wg, ffn1_wu, ffn1_wd, w_in, mu_shift, w0, w_lora_up, a0, a_lora_up, g_lora_up, k_k, k_a, r_k, lnx_g, lnx_b, conv_w, w_out, ln2_g, ln2_b, ffn2_wg, ffn2_wu, ffn2_wd, ln3_g, ln3_b):
    depth = ln1_g.shape[0]
    bp, tp, d = x_prompt.shape
    bs, ts, _ = x_sample.shape
    n_heads = state_wkv.shape[2]
    alpha = (2.0 * depth) ** 0.25
    assert ts >= CONV_W - 1

    xp = x_prompt.reshape(bp * tp, d)
    xs = x_sample.transpose(1, 0, 2).reshape(ts * bs, d)

    outs = {k: [] for k in ("wkv_p", "shift_p", "conv_p", "wkv_s", "shift_s", "conv_s")}
    for l in range(depth):
        vec = lambda a: a[l].reshape(1, -1).astype(F32)
        p = {
            "ln1_g": vec(ln1_g), "ln1_b": vec(ln1_b),
            "ffn1_wg": ffn1_wg[l].astype(F32), "ffn1_wu": ffn1_wu[l].astype(F32),
            "ffn1_wd": ffn1_wd[l].astype(F32),
            "w_in": w_in[l].astype(BF16), "mu_shift": vec(mu_shift), "w0": vec(w0),
            "w_lora_up": w_lora_up[l].astype(BF16), "a0": vec(a0),
            "a_lora_up": a_lora_up[l].astype(BF16), "g_lora_up": g_lora_up[l].astype(BF16),
            "k_k": vec(k_k), "k_a": vec(k_a), "r_k": vec(r_k),
            "lnx_g": vec(lnx_g), "lnx_b": vec(lnx_b),
            "conv_w": conv_w[l].astype(F32), "w_out": w_out[l].astype(BF16),
            "ln2_g": vec(ln2_g), "ln2_b": vec(ln2_b),
            "ffn2_wg": ffn2_wg[l].astype(F32), "ffn2_wu": ffn2_wu[l].astype(F32),
            "ffn2_wd": ffn2_wd[l].astype(F32),
            "ln3_g": vec(ln3_g), "ln3_b": vec(ln3_b),
        }
        wkv0_p = jnp.zeros((bp, n_heads, HEAD, HEAD), F32)
        init = (state_shift[l].astype(F32), state_conv[l].astype(F32).transpose(1, 0, 2))
        wkv0_s = state_wkv[l].astype(F32).transpose(1, 2, 3, 0)
        (xp, wp, sp, cp), (xs, wsm, ps_s, u_s) = _trunk_layer(
            xp, xs, p, alpha, tp=tp, ts=ts, wkv0_p=wkv0_p, wkv0_s=wkv0_s, init_s=init)
        outs["wkv_p"].append(wp)
        outs["shift_p"].append(sp[:, 0, :])
        outs["conv_p"].append(cp)
        outs["wkv_s"].append(wsm.transpose(3, 0, 1, 2))
        outs["shift_s"].append(ps_s[(ts - 1) * bs:])
        outs["conv_s"].append(
            u_s[(ts - (CONV_W - 1)) * bs:].reshape(CONV_W - 1, bs, -1).transpose(1, 0, 2))

    y_prompt = xp.reshape(bp, tp, d)
    y_sample = xs.reshape(ts, bs, d).transpose(1, 0, 2)
    return (y_prompt, y_sample,
            jnp.stack(outs["wkv_p"]), jnp.stack(outs["shift_p"]), jnp.stack(outs["conv_p"]),
            jnp.stack(outs["wkv_s"]), jnp.stack(outs["shift_s"]), jnp.stack(outs["conv_s"]))
```
